```python
import math
import jax
import jax.numpy as jnp
from jax import lax
import numpy as np

D_MODEL = 2048
BATCH = 8
SEQ = 2048
DEPTH = 2
DEC_BATCH = 8
DEC_SEQ = 4096
PAST_LEN = 128

PLE_DIM = 256
GRID_W = 64
A_HEADS = 8
A_QK_DIM = 64
A_V_DIM = 2 * A_QK_DIM
A_WIDTH = A_HEADS * A_V_DIM
Q_BLOCK = 128
T5_BUCKETS = 32
T5_MAX_DIST = 128
B_HEADS = 8
B_HEAD_DIM = 128
B_WIDTH = B_HEADS * B_HEAD_DIM
NA_ROWS = 8
NA_COLS = 16
AB_IN = 3 * A_WIDTH + 3 * B_WIDTH
AB_OUT = A_WIDTH + B_WIDTH
C_WIDTH = D_MODEL
SHORT_CONV = 3
FILTER_EMB = 33
FILTER_HIDDEN = 64
FILTER_TARGET = 1e-2
FAST_DECAY_PCT = 0.3
SLOW_DECAY_PCT = 1.5
D_FF = ((8 * D_MODEL + 3 * 256 - 1) // (3 * 256)) * 256
N_EVEN = (DEPTH + 1) // 2
N_ODD = DEPTH // 2
EPS = 1e-6

kernel_name = "hybrid_diffattn_natten_hyena_encoder"


def rms_norm(x, g):
    xf = x.astype(jnp.float32)
    y = xf * lax.rsqrt(jnp.mean(xf * xf, axis=-1, keepdims=True) + EPS)
    return (y * g.astype(jnp.float32)).astype(x.dtype)


def t5_bucket(rel):
    nb = T5_BUCKETS // 2
    max_exact = nb // 2
    ret = jnp.where(rel > 0, nb, 0)
    n = jnp.abs(rel)
    nf = jnp.maximum(n, 1).astype(jnp.float32)
    large = max_exact + (jnp.log(nf / max_exact) / math.log(T5_MAX_DIST / max_exact)
                         * (nb - max_exact)).astype(jnp.int32)
    large = jnp.minimum(large, nb - 1)
    return ret + jnp.where(n < max_exact, n, large)


def diff_attention(q1, q2, k1, k2, v, lam, rel_table):
    B, L, H, dk = q1.shape
    scale = dk ** -0.5
    n_blk = L // Q_BLOCK
    kpos = jnp.arange(L, dtype=jnp.int32)

    def block(i):
        qs = i * Q_BLOCK
        q1b = lax.dynamic_slice_in_dim(q1, qs, Q_BLOCK, axis=1)
        q2b = lax.dynamic_slice_in_dim(q2, qs, Q_BLOCK, axis=1)
        qpos = qs + jnp.arange(Q_BLOCK, dtype=jnp.int32)
        bias = rel_table[t5_bucket(kpos[None, :] - qpos[:, None])]
        bias = jnp.transpose(bias, (2, 0, 1)).astype(jnp.float32)
        s1 = jnp.einsum('bqhd,bkhd->bhqk', q1b, k1).astype(jnp.float32) * scale + bias
        s2 = jnp.einsum('bqhd,bkhd->bhqk', q2b, k2).astype(jnp.float32) * scale + bias
        a = jax.nn.softmax(s1, axis=-1) - lam * jax.nn.softmax(s2, axis=-1)
        return jnp.einsum('bhqk,bkhd->bqhd', a.astype(v.dtype), v)

    out = lax.map(block, jnp.arange(n_blk, dtype=jnp.int32))
    return jnp.transpose(out, (1, 0, 2, 3, 4)).reshape(B, L, H, v.shape[-1])


def neighbourhood_attention(q, k, v, na_table):
    B, L, H, d = q.shape
    rows = L // GRID_W
    kh = min(NA_ROWS, rows)
    kw = NA_COLS
    n_keys = kh * kw
    scale = d ** -0.5
    cols = jnp.arange(GRID_W, dtype=jnp.int32)
    c_start = jnp.clip(cols - kw // 2, 0, GRID_W - kw)
    key_cols = c_start[:, None] + jnp.arange(kw, dtype=jnp.int32)
    dc = key_cols - cols[:, None] + (NA_COLS - 1)
    qg = jnp.transpose(q.reshape(B, rows, GRID_W, H, d), (1, 0, 2, 3, 4))

    def row_block(args):
        r, q_row = args
        r_start = jnp.clip(r - kh // 2, 0, rows - kh)
        key_rows = r_start + jnp.arange(kh, dtype=jnp.int32)
        idx = (key_rows[None, :, None] * GRID_W + key_cols[:, None, :]).reshape(GRID_W, n_keys)
        kb = jnp.take(k, idx, axis=1)
        vb = jnp.take(v, idx, axis=1)
        dr = key_rows - r + (NA_ROWS - 1)
        bias = na_table[:, dr[None, :, None], dc[:, None, :]]
        bias = bias.reshape(H, GRID_W, n_keys).astype(jnp.float32)
        s = jnp.einsum('bwhd,bwkhd->bhwk', q_row, kb).astype(jnp.float32) * scale + bias
        p = jax.nn.softmax(s, axis=-1)
        return jnp.einsum('bhwk,bwkhd->bwhd', p.astype(v.dtype), vb)

    out = lax.map(row_block, (jnp.arange(rows, dtype=jnp.int32), qg))
    return jnp.transpose(out, (1, 0, 2, 3, 4)).reshape(B, L, H, d)


def even_mixer(u, w_in, w_out, lam_vec, subln_g, na_table, rel_table, layer_idx):
    B, L, _ = u.shape
    z = u @ w_in
    qa, ka, va, qb, kb, vb = jnp.split(z, 6, axis=-1)
    qa = qa.reshape(B, L, A_HEADS, 2, A_QK_DIM)
    ka = ka.reshape(B, L, A_HEADS, 2, A_QK_DIM)
    va = va.reshape(B, L, A_HEADS, A_V_DIM)
    lam_init = 0.8 - 0.6 * math.exp(-0.3 * layer_idx)
    lv = lam_vec.astype(jnp.float32)
    lam = jnp.exp(jnp.sum(lv[0] * lv[1])) - jnp.exp(jnp.sum(lv[2] * lv[3])) + lam_init
    oa = diff_attention(qa[..., 0, :], qa[..., 1, :], ka[..., 0, :], ka[..., 1, :], va, lam, rel_table)
    oa = (rms_norm(oa, subln_g) * (1.0 - lam_init)).reshape(B, L, A_WIDTH)
    ob = neighbourhood_attention(qb.reshape(B, L, B_HEADS, B_HEAD_DIM),
                                 kb.reshape(B, L, B_HEADS, B_HEAD_DIM),
                                 vb.reshape(B, L, B_HEADS, B_HEAD_DIM), na_table)
    ob = ob.reshape(B, L, B_WIDTH)
    return jnp.concatenate([oa, ob], axis=-1) @ w_out


def hyena_filter(L, w1, b1, freq, w2, b2, w3):
    t = jnp.linspace(0.0, 1.0, L, dtype=jnp.float32)[:, None]
    bands = (FILTER_EMB - 1) // 2
    w = 2.0 * math.pi * jnp.arange(L, dtype=jnp.float32) / L
    f = jnp.linspace(1e-4, bands - 1, bands, dtype=jnp.float32)
    ang = w[:, None] * f[None, :]
    feats = jnp.concatenate([t, jnp.cos(ang), -jnp.sin(ang)], axis=-1)
    hdn = jnp.sin(freq[0] * (feats @ w1 + b1))
    hdn = jnp.sin(freq[1] * (hdn @ w2 + b2))
    h = (hdn @ w3).astype(jnp.float32)
    min_decay = math.log(FILTER_TARGET) / FAST_DECAY_PCT
    max_decay = math.log(FILTER_TARGET) / SLOW_DECAY_PCT
    deltas = jnp.abs(jnp.linspace(min_decay, max_decay, C_WIDTH, dtype=jnp.float32))
    decay = jnp.exp(-t * deltas[None, :])
    h_fwd = h[:, :C_WIDTH] * decay
    h_bwd = h[:, C_WIDTH:] * decay
    zero = jnp.zeros((1, C_WIDTH), jnp.float32)
    return jnp.concatenate([h_fwd, zero, h_bwd[:0:-1]], axis=0)


def hyena_mixer(u, w_in, conv_w, conv_b, w1, b1, freq, w2, b2, w3, skip, w_out):
    B, L, _ = u.shape
    z = u @ w_in
    zp = jnp.pad(z, ((0, 0), (1, 1), (0, 0)))
    z = zp[:, :-2] * conv_w[0] + zp[:, 1:-1] * conv_w[1] + zp[:, 2:] * conv_w[2] + conv_b
    x0, x1, v = jnp.split(z, 3, axis=-1)
    v = v * x1
    h = hyena_filter(L, w1, b1, freq, w2, b2, w3)
    V = jnp.fft.rfft(v.astype(jnp.float32), n=2 * L, axis=1)
    Hf = jnp.fft.rfft(h, axis=0)
    y = jnp.fft.irfft(V * Hf[None], n=2 * L, axis=1)[:, :L].astype(u.dtype)
    y = (y + v * skip) * x0
    return y @ w_out


def swiglu(u, w_in, w_out):
    g, up = jnp.split(u @ w_in, 2, axis=-1)
    return (jax.nn.silu(g) * up) @ w_out


def trunk(x, p, W):
    h = x
    for i in range(DEPTH):
        j = i // 2
        hn = rms_norm(h, W['norm_mix'][i])
        if i % 2 == 0:
            mix = even_mixer(hn, W['ab_w_in'][j], W['ab_w_out'][j], W['diff_lambda'][j],
                             W['diff_subln'][j], W['na_bias'][j], W['rel_bias_table'], i)
        else:
            mix = hyena_mixer(hn, W['c_w_in'][j], W['c_conv_w'][j], W['c_conv_b'][j],
                              W['c_filt_w1'][j], W['c_filt_b1'][j], W['c_filt_freq'][j],
                              W['c_filt_w2'][j], W['c_filt_b2'][j], W['c_filt_w3'][j],
                              W['c_skip'][j], W['c_w_out'][j])
        h = h + mix
        h = h + swiglu(rms_norm(h, W['norm_ffn'][i]), W['ffn_w_in'][i], W['ffn_w_out'][i])
        gate = jax.nn.sigmoid(rms_norm(h, W['norm_ple'][i]) @ W['ple_w_gate'][i])
        h = h + (p[i] @ W['ple_w_proj'][i]) * gate
    return rms_norm(h, W['final_norm'])


def setup_inputs(seed: int = 0) -> dict:
    key = jax.random.key(seed)
    keys = iter(jax.random.split(key, 32))

    def nrm(shape, scale):
        return scale * jax.random.normal(next(keys), shape, jnp.float32)

    def gain(shape):
        return 1.0 + nrm(shape, 0.02)

    return {
        "x_prompt": nrm((BATCH, SEQ, D_MODEL), 1.0),
        "x_sample": nrm((DEC_BATCH, DEC_SEQ, D_MODEL), 1.0),
        "p_prompt": nrm((DEPTH, BATCH, SEQ, PLE_DIM), 1.0),
        "p_sample": nrm((DEPTH, DEC_BATCH, DEC_SEQ, PLE_DIM), 1.0),
        "rel_bias_table": nrm((T5_BUCKETS, A_HEADS), 0.1),
        "norm_mix": gain((DEPTH, D_MODEL)),
        "norm_ffn": gain((DEPTH, D_MODEL)),
        "norm_ple": gain((DEPTH, D_MODEL)),
        "final_norm": gain((D_MODEL,)),
        "ab_w_in": nrm((N_EVEN, D_MODEL, AB_IN), D_MODEL ** -0.5),
        "ab_w_out": nrm((N_EVEN, AB_OUT, D_MODEL), AB_OUT ** -0.5),
        "diff_lambda": nrm((N_EVEN, 4, A_QK_DIM), 0.1),
        "diff_subln": gain((N_EVEN, A_V_DIM)),
        "na_bias": nrm((N_EVEN, B_HEADS, 2 * NA_ROWS - 1, 2 * NA_COLS - 1), 0.1),
        "c_w_in": nrm((N_ODD, D_MODEL, 3 * C_WIDTH), D_MODEL ** -0.5),
        "c_conv_w": nrm((N_ODD, SHORT_CONV, 3 * C_WIDTH), SHORT_CONV ** -0.5),
        "c_conv_b": nrm((N_ODD, 3 * C_WIDTH), 0.02),
        "c_filt_w1": nrm((N_ODD, FILTER_EMB, FILTER_HIDDEN), FILTER_EMB ** -0.5),
        "c_filt_b1": nrm((N_ODD, FILTER_HIDDEN), 0.02),
        "c_filt_freq": gain((N_ODD, 2, FILTER_HIDDEN)),
        "c_filt_w2": nrm((N_ODD, FILTER_HIDDEN, FILTER_HIDDEN), FILTER_HIDDEN ** -0.5),
        "c_filt_b2": nrm((N_ODD, FILTER_HIDDEN), 0.02),
        "c_filt_w3": nrm((N_ODD, FILTER_HIDDEN, 2 * C_WIDTH), 0.05 * FILTER_HIDDEN ** -0.5),
        "c_skip": nrm((N_ODD, C_WIDTH), 0.5),
        "c_w_out": nrm((N_ODD, C_WIDTH, D_MODEL), C_WIDTH ** -0.5),
        "ffn_w_in": nrm((DEPTH, D_MODEL, 2 * D_FF), D_MODEL ** -0.5),
        "ffn_w_out": nrm((DEPTH, D_FF, D_MODEL), D_FF ** -0.5),
        "ple_w_proj": nrm((DEPTH, PLE_DIM, D_MODEL), PLE_DIM ** -0.5),
        "ple_w_gate": nrm((DEPTH, D_MODEL, D_MODEL), D_MODEL ** -0.5),
    }


def reference(x_prompt, x_sample, p_prompt, p_sample, rel_bias_table, norm_mix, norm_ffn,
              norm_ple, final_norm, ab_w_in, ab_w_out, diff_lambda, diff_subln, na_bias,
              c_w_in, c_conv_w, c_conv_b, c_filt_w1, c_filt_b1, c_filt_freq, c_filt_w2,
              c_filt_b2, c_filt_w3, c_skip, c_w_out, ffn_w_in, ffn_w_out, ple_w_proj, ple_w_gate):
    W = dict(rel_bias_table=rel_bias_table, norm_mix=norm_mix, norm_ffn=norm_ffn,
             norm_ple=norm_ple, final_norm=final_norm, ab_w_in=ab_w_in, ab_w_out=ab_w_out,
             diff_lambda=diff_lambda, diff_subln=diff_subln, na_bias=na_bias,
             c_w_in=c_w_in, c_conv_w=c_conv_w, c_conv_b=c_conv_b, c_filt_w1=c_filt_w1,
             c_filt_b1=c_filt_b1, c_filt_freq=c_filt_freq, c_filt_w2=c_filt_w2,
             c_filt_b2=c_filt_b2, c_filt_w3=c_filt_w3, c_skip=c_skip, c_w_out=c_w_out,
             ffn_w_in=ffn_w_in, ffn_w_out=ffn_w_out, ple_w_proj=ple_w_proj,
             ple_w_gate=ple_w_gate)
    y_prompt = trunk(x_prompt, p_prompt, W)
    y_sample = trunk(x_sample, p_sample, W)
    return (y_prompt, y_sample)
```

```python
import functools
import math

import jax
import jax.numpy as jnp
from jax import lax
from jax.experimental import pallas as pl
from jax.experimental.pallas import tpu as pltpu

F32 = jnp.float32
BF16 = jnp.bfloat16

D_MODEL = 2048
GRID_W = 64
A_HEADS = 8
A_QK_DIM = 64
A_V_DIM = 2 * A_QK_DIM
A_WIDTH = A_HEADS * A_V_DIM
T5_BUCKETS = 32
T5_MAX_DIST = 128
B_HEADS = 8
B_HEAD_DIM = 128
B_WIDTH = B_HEADS * B_HEAD_DIM
NA_ROWS = 8
NA_COLS = 16
C_WIDTH = D_MODEL
FILTER_EMB = 33
FILTER_TARGET = 1e-2
FAST_DECAY_PCT = 0.3
SLOW_DECAY_PCT = 1.5
EPS = 1e-6

LANES = 128
NA_KEYS = NA_ROWS * GRID_W
MASK_VALUE = -1e30
VMEM_LIMIT = 56 * 1024 * 1024


def _cparams(*sem):
    return pltpu.CompilerParams(dimension_semantics=sem, vmem_limit_bytes=VMEM_LIMIT)


def _rms(x, g):
    ms = jnp.mean(x * x, axis=-1, keepdims=True)
    return x * lax.rsqrt(ms + EPS) * g


def _dot(a, b):
    return jnp.dot(a, b, preferred_element_type=F32)


def _dot_nt(a, b):
    return lax.dot_general(a, b, (((1,), (1,)), ((), ())), preferred_element_type=F32)


def _norm_matmul_kernel(x_ref, g_ref, w_ref, o_ref, xn_ref):
    @pl.when(pl.program_id(1) == 0)
    def _():
        xn_ref[...] = _rms(x_ref[...], g_ref[...]).astype(BF16)

    o_ref[...] = _dot(xn_ref[...], w_ref[...]).astype(o_ref.dtype)


def norm_matmul(x, g, w, out_dtype, tm=512, tn=1024):
    n, d = x.shape
    d_out = w.shape[1]
    return pl.pallas_call(
        _norm_matmul_kernel,
        grid=(n // tm, d_out // tn),
        in_specs=[
            pl.BlockSpec((tm, d), lambda i, j: (i, 0)),
            pl.BlockSpec((1, d), lambda i, j: (0, 0)),
            pl.BlockSpec((d, tn), lambda i, j: (0, j)),
        ],
        out_specs=pl.BlockSpec((tm, tn), lambda i, j: (i, j)),
        out_shape=jax.ShapeDtypeStruct((n, d_out), out_dtype),
        scratch_shapes=[pltpu.VMEM((tm, d), BF16)],
        compiler_params=_cparams("parallel", "arbitrary"),
        name="norm_matmul",
    )(x, g.reshape(1, d), w)


def _proj_res_kernel(*refs, n_in):
    a_refs = refs[:n_in]
    w_refs = refs[n_in:2 * n_in]
    res_ref, o_ref = refs[2 * n_in], refs[2 * n_in + 1]
    acc = res_ref[...]
    for a_ref, w_ref in zip(a_refs, w_refs):
        acc = acc + _dot(a_ref[...], w_ref[...])
    o_ref[...] = acc


def proj_residual(a_list, w_list, res, tm=512):
    n, d = res.shape
    n_in = len(a_list)
    in_specs = [pl.BlockSpec((tm, a.shape[1]), lambda i: (i, 0)) for a in a_list]
    in_specs += [pl.BlockSpec(w.shape, lambda i: (0, 0)) for w in w_list]
    in_specs += [pl.BlockSpec((tm, d), lambda i: (i, 0))]
    return pl.pallas_call(
        functools.partial(_proj_res_kernel, n_in=n_in),
        grid=(n // tm,),
        in_specs=in_specs,
        out_specs=pl.BlockSpec((tm, d), lambda i: (i, 0)),
        out_shape=jax.ShapeDtypeStruct((n, d), F32),
        compiler_params=_cparams("parallel"),
        name="proj_residual",
    )(*a_list, *w_list, res)


def _ffn_kernel(x_ref, g_ref, wg_ref, wu_ref, wo_ref, o_ref, xn_ref, acc_ref):
    f = pl.program_id(1)

    @pl.when(f == 0)
    def _():
        xn_ref[...] = _rms(x_ref[...], g_ref[...]).astype(BF16)
        acc_ref[...] = jnp.zeros_like(acc_ref)

    xn = xn_ref[...]
    gate = _dot(xn, wg_ref[...])
    up = _dot(xn, wu_ref[...])
    act = (gate * jax.nn.sigmoid(gate) * up).astype(BF16)
    acc_ref[...] += _dot(act, wo_ref[...])

    @pl.when(f == pl.num_programs(1) - 1)
    def _():
        o_ref[...] = x_ref[...] + acc_ref[...]


def ffn(x, g, w_in, w_out, tm=512, tf=512):
    n, d = x.shape
    d_ff = w_out.shape[0]
    nf = d_ff // tf
    return pl.pallas_call(
        _ffn_kernel,
        grid=(n // tm, nf),
        in_specs=[
            pl.BlockSpec((tm, d), lambda i, f: (i, 0)),
            pl.BlockSpec((1, d), lambda i, f: (0, 0)),
            pl.BlockSpec((d, tf), lambda i, f: (0, f)),
            pl.BlockSpec((d, tf), lambda i, f: (0, nf + f)),
            pl.BlockSpec((tf, d), lambda i, f: (f, 0)),
        ],
        out_specs=pl.BlockSpec((tm, d), lambda i, f: (i, 0)),
        out_shape=jax.ShapeDtypeStruct((n, d), F32),
        scratch_shapes=[pltpu.VMEM((tm, d), BF16), pltpu.VMEM((tm, d), F32)],
        compiler_params=_cparams("parallel", "arbitrary"),
        name="ffn",
    )(x, g.reshape(1, d), w_in, w_in, w_out)


def _ple_kernel(x_ref, p_ref, g_ref, wg_ref, wp_ref, gf_ref, o_ref, *, final_norm):
    x = x_ref[...]
    gate = jax.nn.sigmoid(_dot(_rms(x, g_ref[...]).astype(BF16), wg_ref[...]))
    proj = _dot(p_ref[...].astype(BF16), wp_ref[...])
    y = x + proj * gate
    if final_norm:
        y = _rms(y, gf_ref[...])
    o_ref[...] = y


def ple(x, p, g, w_gate, w_proj, g_final, final_norm, tm=512):
    n, d = x.shape
    dp = p.shape[1]
    return pl.pallas_call(
        functools.partial(_ple_kernel, final_norm=final_norm),
        grid=(n // tm,),
        in_specs=[
            pl.BlockSpec((tm, d), lambda i: (i, 0)),
            pl.BlockSpec((tm, dp), lambda i: (i, 0)),
            pl.BlockSpec((1, d), lambda i: (0, 0)),
            pl.BlockSpec((d, d), lambda i: (0, 0)),
            pl.BlockSpec((dp, d), lambda i: (0, 0)),
            pl.BlockSpec((1, d), lambda i: (0, 0)),
        ],
        out_specs=pl.BlockSpec((tm, d), lambda i: (i, 0)),
        out_shape=jax.ShapeDtypeStruct((n, d), F32),
        compiler_params=_cparams("parallel"),
        name="ple",
    )(x, p, g.reshape(1, d), w_gate, w_proj, g_final.reshape(1, d))


def t5_bucket(rel):
    nb = T5_BUCKETS // 2
    max_exact = nb // 2
    ret = jnp.where(rel > 0, nb, 0)
    n = jnp.abs(rel)
    nf = jnp.maximum(n, 1).astype(jnp.float32)
    large = max_exact + (jnp.log(nf / max_exact) / math.log(T5_MAX_DIST / max_exact)
                         * (nb - max_exact)).astype(jnp.int32)
    large = jnp.minimum(large, nb - 1)
    return ret + jnp.where(n < max_exact, n, large)


def _diff_attn_kernel(lam_ref, q_ref, k_ref, v_ref, e_ref, g_ref, o_ref,
                      s1_ref, s2_ref, a_ref, *, tq, tk, seq, pad, width, rs, out_scale):
    i = pl.program_id(2)
    q = q_ref[0] * (A_QK_DIM ** -0.5)
    lane = lax.broadcasted_iota(jnp.int32, q.shape, 1)
    zero = jnp.zeros_like(q)
    q1 = jnp.where(lane < A_QK_DIM, q, zero)
    q2 = jnp.where(lane < A_QK_DIM, zero, q)
    for c in range(seq // tk):
        kc = k_ref[0, c * tk:(c + 1) * tk, :]
        start = jnp.clip(c * tk - i * tq + pad, 0, width - tk)
        bias = e_ref[0, :, pl.ds(pl.multiple_of(start, LANES), tk)]
        s1_ref[:, c * tk:(c + 1) * tk] = _dot_nt(q1, kc) + bias
        s2_ref[:, c * tk:(c + 1) * tk] = _dot_nt(q2, kc) + bias
    lam = lam_ref[0]

    def strip(r, carry):
        r0 = pl.multiple_of(r * rs, rs)
        x1 = s1_ref[pl.ds(r0, rs), :]
        p1 = jnp.exp(x1 - jnp.max(x1, axis=-1, keepdims=True))
        c1 = 1.0 / jnp.sum(p1, axis=-1, keepdims=True)
        x2 = s2_ref[pl.ds(r0, rs), :]
        p2 = jnp.exp(x2 - jnp.max(x2, axis=-1, keepdims=True))
        c2 = lam / jnp.sum(p2, axis=-1, keepdims=True)
        a_ref[pl.ds(r0, rs), :] = (p1 * c1 - p2 * c2).astype(BF16)
        return carry

    lax.fori_loop(0, tq // rs, strip, 0)
    o = _dot(a_ref[...], v_ref[0])
    o = _rms(o, g_ref[...]) * out_scale
    o_ref[0] = o.astype(o_ref.dtype)


def diff_attention(z, lam, rel_table, subln_g, lam_init, tq=256, tk=512, rs=16):
    b, seq, _ = z.shape
    pad = tk + T5_MAX_DIST
    width = 2 * tk + tq + 2 * T5_MAX_DIST
    rel = (jnp.arange(width, dtype=jnp.int32)[None, :]
           - jnp.arange(tq, dtype=jnp.int32)[:, None] - pad)
    band = jnp.transpose(rel_table[t5_bucket(rel)], (2, 0, 1)).astype(F32)
    hq = A_WIDTH // LANES
    return pl.pallas_call(
        functools.partial(_diff_attn_kernel, tq=tq, tk=tk, seq=seq, pad=pad, width=width,
                          rs=rs, out_scale=1.0 - lam_init),
        grid=(b, A_HEADS, seq // tq),
        in_specs=[
            pl.BlockSpec(memory_space=pltpu.SMEM),
            pl.BlockSpec((1, tq, LANES), lambda bi, h, i: (bi, i, h)),
            pl.BlockSpec((1, seq, LANES), lambda bi, h, i: (bi, 0, hq + h)),
            pl.BlockSpec((1, seq, LANES), lambda bi, h, i: (bi, 0, 2 * hq + h)),
            pl.BlockSpec((1, tq, width), lambda bi, h, i: (h, 0, 0)),
            pl.BlockSpec((1, A_V_DIM), lambda bi, h, i: (0, 0)),
        ],
        out_specs=pl.BlockSpec((1, tq, LANES), lambda bi, h, i: (bi, i, h)),
        out_shape=jax.ShapeDtypeStruct((b, seq, A_WIDTH), BF16),
        scratch_shapes=[pltpu.VMEM((tq, seq), F32), pltpu.VMEM((tq, seq), F32),
                        pltpu.VMEM((tq, seq), BF16)],
        compiler_params=_cparams("parallel", "parallel", "arbitrary"),
        name="diff_attention",
    )(lam.reshape(1).astype(F32), z, z, z, band, subln_g.reshape(1, A_V_DIM))


def _na_kernel(q_ref, k_ref, v_ref, nb_ref, o_ref, *, rows):
    scale = B_HEAD_DIM ** -0.5

    def row(r, carry):
        r_start = jnp.clip(r - NA_ROWS // 2, 0, rows - NA_ROWS)
        q0 = pl.multiple_of(r * GRID_W, GRID_W)
        k0 = pl.multiple_of(r_start * GRID_W, GRID_W)
        q = q_ref[0, pl.ds(q0, GRID_W), :]
        kb = k_ref[0, pl.ds(k0, NA_KEYS), :]
        vb = v_ref[0, pl.ds(k0, NA_KEYS), :]
        s = _dot_nt(q, kb) * scale + nb_ref[0, r - r_start]
        p = jnp.exp(s - jnp.max(s, axis=-1, keepdims=True))
        p = p * (1.0 / jnp.sum(p, axis=-1, keepdims=True))
        o_ref[0, pl.ds(q0, GRID_W), :] = _dot(p.astype(BF16), vb).astype(o_ref.dtype)
        return carry

    lax.fori_loop(0, rows, row, 0)


def _na_bias(na_table):
    var = jnp.arange(NA_ROWS)[:, None, None, None]
    c = jnp.arange(GRID_W)[None, :, None, None]
    kr = jnp.arange(NA_ROWS)[None, None, :, None]
    kc = jnp.arange(GRID_W)[None, None, None, :]
    c_start = jnp.clip(c - NA_COLS // 2, 0, GRID_W - NA_COLS)
    valid = (kc >= c_start) & (kc < c_start + NA_COLS)
    dr = jnp.broadcast_to(kr - var + (NA_ROWS - 1), (NA_ROWS, GRID_W, NA_ROWS, GRID_W))
    dc = jnp.clip(kc - c + (NA_COLS - 1), 0, 2 * NA_COLS - 2)
    dc = jnp.broadcast_to(dc, dr.shape)
    bias = na_table[:, dr, dc].astype(F32)
    bias = jnp.where(valid[None], bias, MASK_VALUE)
    return bias.reshape(B_HEADS, NA_ROWS, GRID_W, NA_KEYS)


def neighbourhood_attention(z, na_table):
    b, seq, _ = z.shape
    rows = seq // GRID_W
    assert rows >= NA_ROWS
    hq = B_WIDTH // LANES
    base = 3 * A_WIDTH // LANES
    return pl.pallas_call(
        functools.partial(_na_kernel, rows=rows),
        grid=(b, B_HEADS),
        in_specs=[
            pl.BlockSpec((1, seq, LANES), lambda bi, h: (bi, 0, base + h)),
            pl.BlockSpec((1, seq, LANES), lambda bi, h: (bi, 0, base + hq + h)),
            pl.BlockSpec((1, seq, LANES), lambda bi, h: (bi, 0, base + 2 * hq + h)),
            pl.BlockSpec((1, NA_ROWS, GRID_W, NA_KEYS), lambda bi, h: (h, 0, 0, 0)),
        ],
        out_specs=pl.BlockSpec((1, seq, LANES), lambda bi, h: (bi, 0, h)),
        out_shape=jax.ShapeDtypeStruct((b, seq, B_WIDTH), BF16),
        compiler_params=_cparams("parallel", "parallel"),
        name="neighbourhood_attention",
    )(z, z, z, _na_bias(na_table))


def _hyena_pre_kernel(x0_ref, x1_ref, v_ref, w0_ref, w1_ref, wv_ref, b0_ref, b1_ref, bv_ref,
                      x0_out, vv_out, *, seq):
    row = lax.broadcasted_iota(jnp.int32, x0_ref.shape[1:], 0)

    def conv(z_ref, w_ref, b_ref):
        z = z_ref[0]
        prev = jnp.where(row == 0, 0.0, pltpu.roll(z, 1, 0))
        nxt = jnp.where(row == seq - 1, 0.0, pltpu.roll(z, seq - 1, 0))
        return prev * w_ref[0:1, :] + z * w_ref[1:2, :] + nxt * w_ref[2:3, :] + b_ref[...]

    x0_out[0] = conv(x0_ref, w0_ref, b0_ref).astype(x0_out.dtype)
    vv = conv(v_ref, wv_ref, bv_ref) * conv(x1_ref, w1_ref, b1_ref)
    vv_out[0] = vv.astype(vv_out.dtype)


def hyena_pre(z, conv_w, conv_b, tc=128):
    b, seq, _ = z.shape
    nc = C_WIDTH // tc
    zspec = lambda o: pl.BlockSpec((1, seq, tc), lambda bi, c: (bi, 0, o * nc + c))
    wspec = lambda o: pl.BlockSpec((3, tc), lambda bi, c: (0, o * nc + c))
    bspec = lambda o: pl.BlockSpec((1, tc), lambda bi, c: (0, o * nc + c))
    cb = conv_b.reshape(1, -1)
    out = jax.ShapeDtypeStruct((b, seq, C_WIDTH), BF16)
    return pl.pallas_call(
        functools.partial(_hyena_pre_kernel, seq=seq),
        grid=(b, nc),
        in_specs=[zspec(0), zspec(1), zspec(2), wspec(0), wspec(1), wspec(2),
                  bspec(0), bspec(1), bspec(2)],
        out_specs=[pl.BlockSpec((1, seq, tc), lambda bi, c: (bi, 0, c))] * 2,
        out_shape=[out, out],
        compiler_params=_cparams("parallel", "parallel"),
        name="hyena_pre",
    )(z, z, z, conv_w, conv_w, conv_w, cb, cb, cb)


def _filter_kernel(hdn_ref, wf_ref, wb_ref, dl_ref, a_ref, b_ref, *, tl, seq):
    hd = hdn_ref[...].astype(BF16)
    row = pl.program_id(0) * tl + lax.broadcasted_iota(jnp.int32, a_ref.shape, 0)
    t = row.astype(F32) * (1.0 / (seq - 1))
    decay = jnp.exp(-t * dl_ref[...])
    hf = _dot(hd, wf_ref[...]) * decay
    hb = jnp.where(row == 0, 0.0, _dot(hd, wb_ref[...]) * decay)
    a_ref[...] = (hf + hb).astype(a_ref.dtype)
    b_ref[...] = (hf - hb).astype(b_ref.dtype)


def hyena_filter_taps(hdn, w3, deltas, tl=512, tc=512):
    seq = hdn.shape[0]
    nc = C_WIDTH // tc
    out = jax.ShapeDtypeStruct((seq, C_WIDTH), BF16)
    return pl.pallas_call(
        functools.partial(_filter_kernel, tl=tl, seq=seq),
        grid=(seq // tl, nc),
        in_specs=[
            pl.BlockSpec((tl, hdn.shape[1]), lambda l, c: (l, 0)),
            pl.BlockSpec((w3.shape[0], tc), lambda l, c: (0, c)),
            pl.BlockSpec((w3.shape[0], tc), lambda l, c: (0, nc + c)),
            pl.BlockSpec((1, tc), lambda l, c: (0, c)),
        ],
        out_specs=[pl.BlockSpec((tl, tc), lambda l, c: (l, c))] * 2,
        out_shape=[out, out],
        compiler_params=_cparams("parallel", "parallel"),
        name="hyena_filter_taps",
    )(hdn, w3, w3, deltas.reshape(1, C_WIDTH))


NYQ_ROWS = 16


def _filter_dft_kernel(f_ref, a_ref, b_ref, p1_ref, p2_ref, hn_ref, *, tf, seq):
    fi = pl.program_id(1)
    row = fi * tf + lax.broadcasted_iota(jnp.int32, p1_ref.shape, 0)
    w = jnp.where(row == 0, 0.5 / seq, 1.0 / seq)
    p1_ref[...] = _dot(f_ref[0], a_ref[...]) * w
    p2_ref[...] = jnp.where(row == 0, 0.0, _dot(f_ref[1], b_ref[...]) * w)

    @pl.when(fi == 0)
    def _():
        hn_ref[...] = _dot(f_ref[1, 0:NYQ_ROWS, :], a_ref[...]) * (0.5 / seq)


def hyena_filter_dft(fmat, a, bm, tf=256, tc=512):
    seq = a.shape[0]
    return pl.pallas_call(
        functools.partial(_filter_dft_kernel, tf=tf, seq=seq),
        grid=(C_WIDTH // tc, seq // tf),
        in_specs=[
            pl.BlockSpec((2, tf, seq), lambda c, f: (0, f, 0)),
            pl.BlockSpec((seq, tc), lambda c, f: (0, c)),
            pl.BlockSpec((seq, tc), lambda c, f: (0, c)),
        ],
        out_specs=[
            pl.BlockSpec((tf, tc), lambda c, f: (f, c)),
            pl.BlockSpec((tf, tc), lambda c, f: (f, c)),
            pl.BlockSpec((NYQ_ROWS, tc), lambda c, f: (0, c)),
        ],
        out_shape=[
            jax.ShapeDtypeStruct((seq, C_WIDTH), F32),
            jax.ShapeDtypeStruct((seq, C_WIDTH), F32),
            jax.ShapeDtypeStruct((NYQ_ROWS, C_WIDTH), F32),
        ],
        compiler_params=_cparams("parallel", "arbitrary"),
        name="hyena_filter_dft",
    )(fmat, a, bm)


def _dft_fwd_kernel(f_ref, v_ref, p1_ref, p2_ref, hn_ref, y_ref, *, tf, seq):
    acc = _dot(f_ref[...].reshape(2 * tf, seq), v_ref[0])
    vr, vi = acc[:tf], acc[tf:]
    p1, p2 = p1_ref[...], p2_ref[...]
    row = pl.program_id(2) * tf + lax.broadcasted_iota(jnp.int32, p1.shape, 0)
    p3 = jnp.where(row == 0, hn_ref[0:1, :], p1)
    y_ref[0, 0] = (vr * p1 - vi * p2).astype(y_ref.dtype)
    y_ref[0, 1] = (vr * p2 + vi * p3).astype(y_ref.dtype)


def hyena_dft_fwd(fmat, vv, p1, p2, hn, tf=256, tc=512):
    b, seq, _ = vv.shape
    return pl.pallas_call(
        functools.partial(_dft_fwd_kernel, tf=tf, seq=seq),
        grid=(b, C_WIDTH // tc, seq // tf),
        in_specs=[
            pl.BlockSpec((2, tf, seq), lambda bi, c, f: (0, f, 0)),
            pl.BlockSpec((1, seq, tc), lambda bi, c, f: (bi, 0, c)),
            pl.BlockSpec((tf, tc), lambda bi, c, f: (f, c)),
            pl.BlockSpec((tf, tc), lambda bi, c, f: (f, c)),
            pl.BlockSpec((NYQ_ROWS, tc), lambda bi, c, f: (0, c)),
        ],
        out_specs=pl.BlockSpec((1, 2, tf, tc), lambda bi, c, f: (bi, 0, f, c)),
        out_shape=jax.ShapeDtypeStruct((b, 2, seq, C_WIDTH), BF16),
        compiler_params=_cparams("parallel", "parallel", "parallel"),
        name="hyena_dft_fwd",
    )(fmat, vv, p1, p2, hn)


def _dft_inv_kernel(g_ref, y_ref, vv_ref, x0_ref, skip_ref, u_ref):
    y = _dot(g_ref[...], y_ref[0])
    vv = vv_ref[0].astype(F32)
    x0 = x0_ref[0].astype(F32)
    u_ref[0] = ((y + vv * skip_ref[...]) * x0).astype(u_ref.dtype)


def hyena_dft_inv(gmat, ys, vv, x0, skip, tt=256, tc=512):
    b, seq, _ = vv.shape
    return pl.pallas_call(
        _dft_inv_kernel,
        grid=(b, C_WIDTH // tc, seq // tt),
        in_specs=[
            pl.BlockSpec((tt, 2 * seq), lambda bi, c, t: (t, 0)),
            pl.BlockSpec((1, 2 * seq, tc), lambda bi, c, t: (bi, 0, c)),
            pl.BlockSpec((1, tt, tc), lambda bi, c, t: (bi, t, c)),
            pl.BlockSpec((1, tt, tc), lambda bi, c, t: (bi, t, c)),
            pl.BlockSpec((1, tc), lambda bi, c, t: (0, c)),
        ],
        out_specs=pl.BlockSpec((1, tt, tc), lambda bi, c, t: (bi, t, c)),
        out_shape=jax.ShapeDtypeStruct((b, seq, C_WIDTH), BF16),
        compiler_params=_cparams("parallel", "parallel", "parallel"),
        name="hyena_dft_inv",
    )(gmat, ys, vv, x0, skip.reshape(1, C_WIDTH))


def _dft_matrices(seq):
    idx = jnp.arange(seq, dtype=jnp.int32)
    ang = ((idx[:, None] * idx[None, :]) % (2 * seq)).astype(F32) * (math.pi / seq)
    sin_part = (-jnp.sin(ang)).at[0].set((1 - 2 * (idx % 2)).astype(F32))
    fmat = jnp.stack([jnp.cos(ang), sin_part]).astype(BF16)
    return fmat, fmat.reshape(2 * seq, seq).T


def _filter_hidden(seq, w1, b1, freq, w2, b2):
    t = jnp.linspace(0.0, 1.0, seq, dtype=F32)[:, None]
    bands = (FILTER_EMB - 1) // 2
    w = 2.0 * math.pi * jnp.arange(seq, dtype=F32) / seq
    f = jnp.linspace(1e-4, bands - 1, bands, dtype=F32)
    ang = w[:, None] * f[None, :]
    feats = jnp.concatenate([t, jnp.cos(ang), -jnp.sin(ang)], axis=-1)
    hdn = jnp.sin(freq[0] * (feats @ w1 + b1))
    return jnp.sin(freq[1] * (hdn @ w2 + b2))


def hyena_mixer(h, batch, seq, g, w_in, conv_w, conv_b, w1, b1, freq, w2, b2, w3, skip, w_out):
    z = norm_matmul(h, g, w_in, F32).reshape(batch, seq, 3 * C_WIDTH)
    x0, vv = hyena_pre(z, conv_w, conv_b)
    hdn = _filter_hidden(seq, w1, b1, freq, w2, b2)
    min_decay = math.log(FILTER_TARGET) / FAST_DECAY_PCT
    max_decay = math.log(FILTER_TARGET) / SLOW_DECAY_PCT
    deltas = jnp.abs(jnp.linspace(min_decay, max_decay, C_WIDTH, dtype=F32))
    a, bm = hyena_filter_taps(hdn, w3, deltas)
    fmat, gmat = _dft_matrices(seq)
    p1, p2, hn = hyena_filter_dft(fmat, a, bm)
    ys = hyena_dft_fwd(fmat, vv, p1, p2, hn).reshape(batch, 2 * seq, C_WIDTH)
    u = hyena_dft_inv(gmat, ys, vv, x0, skip)
    return proj_residual([u.reshape(batch * seq, C_WIDTH)], [w_out], h)


def even_mixer(h, batch, seq, g, w_in, w_out, lam_vec, subln_g, na_table, rel_table, layer_idx):
    z = norm_matmul(h, g, w_in, BF16).reshape(batch, seq, w_in.shape[1])
    lam_init = 0.8 - 0.6 * math.exp(-0.3 * layer_idx)
    lv = lam_vec.astype(F32)
    lam = jnp.exp(jnp.sum(lv[0] * lv[1])) - jnp.exp(jnp.sum(lv[2] * lv[3])) + lam_init
    oa = diff_attention(z, lam, rel_table, subln_g, lam_init)
    ob = neighbourhood_attention(z, na_table)
    n = batch * seq
    return proj_residual([oa.reshape(n, A_WIDTH), ob.reshape(n, B_WIDTH)],
                         [w_out[:A_WIDTH], w_out[A_WIDTH:]], h)


def _trunk(x, p, W):
    batch, seq, d = x.shape
    depth = p.shape[0]
    h = x.reshape(batch * seq, d)
    for i in range(depth):
        j = i // 2
        if i % 2 == 0:
            h = even_mixer(h, batch, seq, W['norm_mix'][i], W['ab_w_in'][j], W['ab_w_out'][j],
                           W['diff_lambda'][j], W['diff_subln'][j], W['na_bias'][j],
                           W['rel_bias_table'], i)
        else:
            h = hyena_mixer(h, batch, seq, W['norm_mix'][i], W['c_w_in'][j], W['c_conv_w'][j],
                            W['c_conv_b'][j], W['c_filt_w1'][j], W['c_filt_b1'][j],
                            W['c_filt_freq'][j], W['c_filt_w2'][j], W['c_filt_b2'][j],
                            W['c_filt_w3'][j], W['c_skip'][j], W['c_w_out'][j])
        h = ffn(h, W['norm_ffn'][i], W['ffn_w_in'][i], W['ffn_w_out'][i])
        h = ple(h, p[i].reshape(batch * seq, -1), W['norm_ple'][i], W['ple_w_gate'][i],
                W['ple_w_proj'][i], W['final_norm'], final_norm=(i == depth - 1))
    return h.reshape(batch, seq, d)


def kernel(x_prompt, x_sample, p_prompt, p_sample, rel_bias_table, norm_mix, norm_ffn, norm_ple, final_norm, ab_w_in, ab_w_out, diff_lambda, diff_subln, na_bias, c_w_in, c_conv_w, c_conv_b, c_filt_w1, c_filt_b1, c_filt_freq, c_filt_w2, c_filt_b2, c_filt_w3, c_skip, c_w_out, ffn_w_in, ffn_w_out, ple_w_proj, ple_w_gate):
    bf = lambda w: w.astype(BF16)
    W = dict(rel_bias_table=rel_bias_table, norm_mix=norm_mix, norm_ffn=norm_ffn,
             norm_ple=norm_ple, final_norm=final_norm, ab_w_in=bf(ab_w_in), ab_w_out=bf(ab_w_out),
             diff_lambda=diff_lambda, diff_subln=diff_subln, na_bias=na_bias,
             c_w_in=bf(c_w_in), c_conv_w=c_conv_w, c_conv_b=c_conv_b, c_filt_w1=c_filt_w1,
             c_filt_b1=c_filt_b1, c_filt_freq=c_filt_freq, c_filt_w2=c_filt_w2,
             c_filt_b2=c_filt_b2, c_filt_w3=bf(c_filt_w3), c_skip=c_skip, c_w_out=bf(c_w_out),
             ffn_w_in=bf(ffn_w_in), ffn_w_out=bf(ffn_w_out), ple_w_proj=bf(ple_w_proj),
             ple_w_gate=bf(ple_w_gate))
    return (_trunk(x_prompt, p_prompt, W), _trunk(x_sample, p_sample, W))
```

```python
import functools
import math

import numpy as np
import jax
import jax.numpy as jnp
from jax import lax
from jax.experimental import pallas as pl
from jax.experimental.pallas import tpu as pltpu

F32 = jnp.float32
BF16 = jnp.bfloat16

D_MODEL = 2048
GRID_W = 64
A_HEADS = 8
A_QK_DIM = 64
A_V_DIM = 2 * A_QK_DIM
A_WIDTH = A_HEADS * A_V_DIM
T5_BUCKETS = 32
T5_MAX_DIST = 128
B_HEADS = 8
B_HEAD_DIM = 128
B_WIDTH = B_HEADS * B_HEAD_DIM
NA_ROWS = 8
NA_COLS = 16
C_WIDTH = D_MODEL
FILTER_EMB = 33
FILTER_TARGET = 1e-2
FAST_DECAY_PCT = 0.3
SLOW_DECAY_PCT = 1.5
EPS = 1e-6

LANES = 128
NA_KEYS = NA_ROWS * GRID_W
MASK_VALUE = -1e30
NA_UNROLL = 8
VMEM_LIMIT = 56 * 1024 * 1024


def _cparams(*sem):
    return pltpu.CompilerParams(dimension_semantics=sem, vmem_limit_bytes=VMEM_LIMIT)


def _rms(x, g):
    ms = jnp.mean(x * x, axis=-1, keepdims=True)
    return x * lax.rsqrt(ms + EPS) * g


def _dot(a, b):
    return jnp.dot(a, b, preferred_element_type=F32)


def _dot_nt(a, b):
    return lax.dot_general(a, b, (((1,), (1,)), ((), ())), preferred_element_type=F32)


def _norm_matmul_kernel(x_ref, g_ref, w_ref, o_ref, xn_ref):
    @pl.when(pl.program_id(1) == 0)
    def _():
        xn_ref[...] = _rms(x_ref[...], g_ref[...]).astype(BF16)

    o_ref[...] = _dot(xn_ref[...], w_ref[...]).astype(o_ref.dtype)


def norm_matmul(x, g, w, out_dtype, tm=512, tn=2048):
    n, d = x.shape
    d_out = w.shape[1]
    return pl.pallas_call(
        _norm_matmul_kernel,
        grid=(n // tm, d_out // tn),
        in_specs=[
            pl.BlockSpec((tm, d), lambda i, j: (i, 0)),
            pl.BlockSpec((1, d), lambda i, j: (0, 0)),
            pl.BlockSpec((d, tn), lambda i, j: (0, j)),
        ],
        out_specs=pl.BlockSpec((tm, tn), lambda i, j: (i, j)),
        out_shape=jax.ShapeDtypeStruct((n, d_out), out_dtype),
        scratch_shapes=[pltpu.VMEM((tm, d), BF16)],
        compiler_params=_cparams("parallel", "arbitrary"),
        name="norm_matmul",
    )(x, g.reshape(1, d), w)


def _proj_res_kernel(*refs, n_in):
    a_refs = refs[:n_in]
    w_refs = refs[n_in:2 * n_in]
    res_ref, o_ref = refs[2 * n_in], refs[2 * n_in + 1]
    acc = res_ref[...]
    for a_ref, w_ref in zip(a_refs, w_refs):
        acc = acc + _dot(a_ref[...], w_ref[...])
    o_ref[...] = acc


def proj_residual(a_list, w_list, res, tm=512):
    n, d = res.shape
    n_in = len(a_list)
    in_specs = [pl.BlockSpec((tm, a.shape[1]), lambda i: (i, 0)) for a in a_list]
    in_specs += [pl.BlockSpec(w.shape, lambda i: (0, 0)) for w in w_list]
    in_specs += [pl.BlockSpec((tm, d), lambda i: (i, 0))]
    return pl.pallas_call(
        functools.partial(_proj_res_kernel, n_in=n_in),
        grid=(n // tm,),
        in_specs=in_specs,
        out_specs=pl.BlockSpec((tm, d), lambda i: (i, 0)),
        out_shape=jax.ShapeDtypeStruct((n, d), F32),
        compiler_params=_cparams("parallel"),
        name="proj_residual",
    )(*a_list, *w_list, res)


def _ffn_kernel(x_ref, g_ref, wg_ref, wu_ref, wo_ref, o_ref, xn_ref, acc_ref):
    f = pl.program_id(1)

    @pl.when(f == 0)
    def _():
        xn_ref[...] = _rms(x_ref[...], g_ref[...]).astype(BF16)
        acc_ref[...] = jnp.zeros_like(acc_ref)

    xn = xn_ref[...]
    gate = _dot(xn, wg_ref[...])
    up = _dot(xn, wu_ref[...])
    act = (gate * jax.nn.sigmoid(gate) * up).astype(BF16)
    acc_ref[...] += _dot(act, wo_ref[...])

    @pl.when(f == pl.num_programs(1) - 1)
    def _():
        o_ref[...] = x_ref[...] + acc_ref[...]


def ffn(x, g, w_in, w_out, tm=512, tf=512):
    n, d = x.shape
    d_ff = w_out.shape[0]
    nf = d_ff // tf
    return pl.pallas_call(
        _ffn_kernel,
        grid=(n // tm, nf),
        in_specs=[
            pl.BlockSpec((tm, d), lambda i, f: (i, 0)),
            pl.BlockSpec((1, d), lambda i, f: (0, 0)),
            pl.BlockSpec((d, tf), lambda i, f: (0, f)),
            pl.BlockSpec((d, tf), lambda i, f: (0, nf + f)),
            pl.BlockSpec((tf, d), lambda i, f: (f, 0)),
        ],
        out_specs=pl.BlockSpec((tm, d), lambda i, f: (i, 0)),
        out_shape=jax.ShapeDtypeStruct((n, d), F32),
        scratch_shapes=[pltpu.VMEM((tm, d), BF16), pltpu.VMEM((tm, d), F32)],
        compiler_params=_cparams("parallel", "arbitrary"),
        name="ffn",
    )(x, g.reshape(1, d), w_in, w_in, w_out)


def _ple_kernel(x_ref, p_ref, g_ref, wg_ref, wp_ref, gf_ref, o_ref, *, final_norm):
    x = x_ref[...]
    gate = jax.nn.sigmoid(_dot(_rms(x, g_ref[...]).astype(BF16), wg_ref[...]))
    proj = _dot(p_ref[...].astype(BF16), wp_ref[...])
    y = x + proj * gate
    if final_norm:
        y = _rms(y, gf_ref[...])
    o_ref[...] = y


def ple(x, p, g, w_gate, w_proj, g_final, final_norm, tm=512):
    n, d = x.shape
    dp = p.shape[1]
    return pl.pallas_call(
        functools.partial(_ple_kernel, final_norm=final_norm),
        grid=(n // tm,),
        in_specs=[
            pl.BlockSpec((tm, d), lambda i: (i, 0)),
            pl.BlockSpec((tm, dp), lambda i: (i, 0)),
            pl.BlockSpec((1, d), lambda i: (0, 0)),
            pl.BlockSpec((d, d), lambda i: (0, 0)),
            pl.BlockSpec((dp, d), lambda i: (0, 0)),
            pl.BlockSpec((1, d), lambda i: (0, 0)),
        ],
        out_specs=pl.BlockSpec((tm, d), lambda i: (i, 0)),
        out_shape=jax.ShapeDtypeStruct((n, d), F32),
        compiler_params=_cparams("parallel"),
        name="ple",
    )(x, p, g.reshape(1, d), w_gate, w_proj, g_final.reshape(1, d))


def t5_bucket(rel):
    nb = T5_BUCKETS // 2
    max_exact = nb // 2
    ret = jnp.where(rel > 0, nb, 0)
    n = jnp.abs(rel)
    nf = jnp.maximum(n, 1).astype(jnp.float32)
    large = max_exact + (jnp.log(nf / max_exact) / math.log(T5_MAX_DIST / max_exact)
                         * (nb - max_exact)).astype(jnp.int32)
    large = jnp.minimum(large, nb - 1)
    return ret + jnp.where(n < max_exact, n, large)


def _lane_tiles(x):
    return [x[:, j * LANES:(j + 1) * LANES] for j in range(x.shape[1] // LANES)]


def _diff_attn_kernel(lam_ref, q_ref, k_ref, v_ref, e_ref, g_ref, o_ref,
                      s1_ref, s2_ref, p1_ref, p2_ref, va_ref, *, tq, tk, seq, pad, width,
                      out_scale):
    i = pl.program_id(2)

    @pl.when(i == 0)
    def _():
        va_ref[:, :A_V_DIM] = v_ref[0]
        va_ref[:, A_V_DIM:] = jnp.ones((seq, A_V_DIM), BF16)

    q = q_ref[0] * (A_QK_DIM ** -0.5)
    lane = lax.broadcasted_iota(jnp.int32, q.shape, 1)
    zero = jnp.zeros_like(q)
    q1 = jnp.where(lane < A_QK_DIM, q, zero)
    q2 = jnp.where(lane < A_QK_DIM, zero, q)
    m1 = m2 = None
    for c in range(seq // tk):
        cols = slice(c * tk, (c + 1) * tk)
        kc = k_ref[0, cols, :]
        start = jnp.clip(c * tk - i * tq + pad, 0, width - tk)
        bias = e_ref[0, :, pl.ds(pl.multiple_of(start, LANES), tk)]
        x1 = _dot_nt(q1, kc) + bias
        x2 = _dot_nt(q2, kc) + bias
        s1_ref[:, cols] = x1
        s2_ref[:, cols] = x2
        m1 = functools.reduce(jnp.maximum, _lane_tiles(x1) + ([] if m1 is None else [m1]))
        m2 = functools.reduce(jnp.maximum, _lane_tiles(x2) + ([] if m2 is None else [m2]))
    m1 = jnp.broadcast_to(jnp.max(m1, axis=-1, keepdims=True), (tq, LANES))
    m2 = jnp.broadcast_to(jnp.max(m2, axis=-1, keepdims=True), (tq, LANES))
    for j in range(seq // LANES):
        cols = slice(j * LANES, (j + 1) * LANES)
        p1_ref[:, cols] = jnp.exp(s1_ref[:, cols] - m1).astype(BF16)
        p2_ref[:, cols] = jnp.exp(s2_ref[:, cols] - m2).astype(BF16)
    o1 = _dot(p1_ref[...], va_ref[...])
    o2 = _dot(p2_ref[...], va_ref[...])
    o = (o1[:, :A_V_DIM] * (1.0 / o1[:, A_V_DIM:])
         - o2[:, :A_V_DIM] * (lam_ref[0] / o2[:, A_V_DIM:]))
    o = _rms(o, g_ref[...]) * out_scale
    o_ref[0] = o.astype(o_ref.dtype)


def _toeplitz(w, rows, cols):
    n = rows + cols
    flat = jnp.tile(w, (1,) * (w.ndim - 1) + (rows,))[..., :rows * (n - 1)]
    return flat.reshape(w.shape[:-1] + (rows, n - 1))[..., rows - 1:rows - 1 + cols]


def diff_attention(z, lam, rel_table, subln_g, lam_init, tq=256, tk=512):
    b, seq, _ = z.shape
    pad = tk + T5_MAX_DIST
    width = 2 * tk + tq + 2 * T5_MAX_DIST
    rel = jnp.arange(tq + width, dtype=jnp.int32) - (tq - 1) - pad
    onehot = (t5_bucket(rel)[:, None] == jnp.arange(T5_BUCKETS)[None, :]).astype(F32)
    diag = jnp.dot(onehot, rel_table.astype(F32), precision=lax.Precision.HIGHEST)
    band = _toeplitz(diag.T, tq, width)
    hq = A_WIDTH // LANES
    return pl.pallas_call(
        functools.partial(_diff_attn_kernel, tq=tq, tk=tk, seq=seq, pad=pad, width=width,
                          out_scale=1.0 - lam_init),
        grid=(b, A_HEADS, seq // tq),
        in_specs=[
            pl.BlockSpec(memory_space=pltpu.SMEM),
            pl.BlockSpec((1, tq, LANES), lambda bi, h, i: (bi, i, h)),
            pl.BlockSpec((1, seq, LANES), lambda bi, h, i: (bi, 0, hq + h)),
            pl.BlockSpec((1, seq, LANES), lambda bi, h, i: (bi, 0, 2 * hq + h)),
            pl.BlockSpec((1, tq, width), lambda bi, h, i: (h, 0, 0)),
            pl.BlockSpec((1, A_V_DIM), lambda bi, h, i: (0, 0)),
        ],
        out_specs=pl.BlockSpec((1, tq, LANES), lambda bi, h, i: (bi, i, h)),
        out_shape=jax.ShapeDtypeStruct((b, seq, A_WIDTH), BF16),
        scratch_shapes=[pltpu.VMEM((tq, seq), F32), pltpu.VMEM((tq, seq), F32),
                        pltpu.VMEM((tq, seq), BF16), pltpu.VMEM((tq, seq), BF16),
                        pltpu.VMEM((seq, 2 * A_V_DIM), BF16)],
        compiler_params=_cparams("parallel", "parallel", "arbitrary"),
        name="diff_attention",
    )(lam.reshape(1).astype(F32), z, z, z, band, subln_g.reshape(1, A_V_DIM))


def _na_kernel(q_ref, k_ref, v_ref, nb_ref, o_ref, *, rows):
    scale = B_HEAD_DIM ** -0.5

    def row(r):
        r_start = jnp.clip(r - NA_ROWS // 2, 0, rows - NA_ROWS)
        q0 = pl.multiple_of(r * GRID_W, GRID_W)
        k0 = pl.multiple_of(r_start * GRID_W, GRID_W)
        q = q_ref[0, pl.ds(q0, GRID_W), :]
        kb = k_ref[0, pl.ds(k0, NA_KEYS), :]
        vb = v_ref[0, pl.ds(k0, NA_KEYS), :]
        s = _dot_nt(q, kb) * scale + nb_ref[0, r - r_start]
        p = jnp.exp(s - jnp.max(s, axis=-1, keepdims=True))
        p = p * (1.0 / jnp.sum(p, axis=-1, keepdims=True))
        o_ref[0, pl.ds(q0, GRID_W), :] = _dot(p.astype(BF16), vb).astype(o_ref.dtype)

    def row_group(gi, carry):
        for u in range(NA_UNROLL):
            row(gi * NA_UNROLL + u)
        return carry

    lax.fori_loop(0, rows // NA_UNROLL, row_group, 0)


def _na_bias(na_table):
    var = np.arange(NA_ROWS)[:, None, None]
    kr = np.arange(NA_ROWS)[None, :, None]
    row_sel = (kr - var + (NA_ROWS - 1) == np.arange(2 * NA_ROWS - 1)).astype(np.float32)
    c = np.arange(GRID_W)[:, None, None]
    kc = np.arange(GRID_W)[None, :, None]
    col_sel = (kc - c + (NA_COLS - 1) == np.arange(2 * NA_COLS - 1)).astype(np.float32)
    c_start = np.clip(c - NA_COLS // 2, 0, GRID_W - NA_COLS)
    valid = ((kc >= c_start) & (kc < c_start + NA_COLS))[None, None, :, None, :, 0]
    bias = jnp.einsum('vka,hab,cqb->hvckq', row_sel, na_table.astype(F32), col_sel,
                      precision=lax.Precision.HIGHEST)
    bias = jnp.where(valid, bias, MASK_VALUE)
    return bias.reshape(B_HEADS, NA_ROWS, GRID_W, NA_KEYS)


def neighbourhood_attention(z, na_table):
    b, seq, _ = z.shape
    rows = seq // GRID_W
    assert rows >= NA_ROWS and rows % NA_UNROLL == 0
    hq = B_WIDTH // LANES
    base = 3 * A_WIDTH // LANES
    return pl.pallas_call(
        functools.partial(_na_kernel, rows=rows),
        grid=(b, B_HEADS),
        in_specs=[
            pl.BlockSpec((1, seq, LANES), lambda bi, h: (bi, 0, base + h)),
            pl.BlockSpec((1, seq, LANES), lambda bi, h: (bi, 0, base + hq + h)),
            pl.BlockSpec((1, seq, LANES), lambda bi, h: (bi, 0, base + 2 * hq + h)),
            pl.BlockSpec((1, NA_ROWS, GRID_W, NA_KEYS), lambda bi, h: (h, 0, 0, 0)),
        ],
        out_specs=pl.BlockSpec((1, seq, LANES), lambda bi, h: (bi, 0, h)),
        out_shape=jax.ShapeDtypeStruct((b, seq, B_WIDTH), BF16),
        compiler_params=_cparams("parallel", "parallel"),
        name="neighbourhood_attention",
    )(z, z, z, _na_bias(na_table))


def _hyena_pre_kernel(x0_ref, x1_ref, v_ref, w0_ref, w1_ref, wv_ref, b0_ref, b1_ref, bv_ref,
                      x0_out, vv_out, *, seq):
    row = lax.broadcasted_iota(jnp.int32, x0_ref.shape[1:], 0)

    def conv(z_ref, w_ref, b_ref):
        z = z_ref[0]
        prev = jnp.where(row == 0, 0.0, pltpu.roll(z, 1, 0))
        nxt = jnp.where(row == seq - 1, 0.0, pltpu.roll(z, seq - 1, 0))
        return prev * w_ref[0:1, :] + z * w_ref[1:2, :] + nxt * w_ref[2:3, :] + b_ref[...]

    x0_out[0] = conv(x0_ref, w0_ref, b0_ref).astype(x0_out.dtype)
    vv = conv(v_ref, wv_ref, bv_ref) * conv(x1_ref, w1_ref, b1_ref)
    vv_out[0] = vv.astype(vv_out.dtype)


def hyena_pre(z, conv_w, conv_b, tc=128):
    b, seq, _ = z.shape
    nc = C_WIDTH // tc
    zspec = lambda o: pl.BlockSpec((1, seq, tc), lambda bi, c: (bi, 0, o * nc + c))
    wspec = lambda o: pl.BlockSpec((3, tc), lambda bi, c: (0, o * nc + c))
    bspec = lambda o: pl.BlockSpec((1, tc), lambda bi, c: (0, o * nc + c))
    cb = conv_b.reshape(1, -1)
    out = jax.ShapeDtypeStruct((b, seq, C_WIDTH), BF16)
    return pl.pallas_call(
        functools.partial(_hyena_pre_kernel, seq=seq),
        grid=(b, nc),
        in_specs=[zspec(0), zspec(1), zspec(2), wspec(0), wspec(1), wspec(2),
                  bspec(0), bspec(1), bspec(2)],
        out_specs=[pl.BlockSpec((1, seq, tc), lambda bi, c: (bi, 0, c))] * 2,
        out_shape=[out, out],
        compiler_params=_cparams("parallel", "parallel"),
        name="hyena_pre",
    )(z, z, z, conv_w, conv_w, conv_w, cb, cb, cb)


def _filter_kernel(hdn_ref, wf_ref, wb_ref, dl_ref, a_ref, b_ref, *, tl, seq):
    hd = hdn_ref[...].astype(BF16)
    row = pl.program_id(0) * tl + lax.broadcasted_iota(jnp.int32, a_ref.shape, 0)
    t = row.astype(F32) * (1.0 / (seq - 1))
    decay = jnp.exp(-t * dl_ref[...])
    hf = _dot(hd, wf_ref[...]) * decay
    hb = jnp.where(row == 0, 0.0, _dot(hd, wb_ref[...]) * decay)
    a_ref[...] = (hf + hb).astype(a_ref.dtype)
    b_ref[...] = (hf - hb).astype(b_ref.dtype)


def hyena_filter_taps(hdn, w3, deltas, tl=512, tc=512):
    seq = hdn.shape[0]
    nc = C_WIDTH // tc
    out = jax.ShapeDtypeStruct((seq, C_WIDTH), BF16)
    return pl.pallas_call(
        functools.partial(_filter_kernel, tl=tl, seq=seq),
        grid=(seq // tl, nc),
        in_specs=[
            pl.BlockSpec((tl, hdn.shape[1]), lambda l, c: (l, 0)),
            pl.BlockSpec((w3.shape[0], tc), lambda l, c: (0, c)),
            pl.BlockSpec((w3.shape[0], tc), lambda l, c: (0, nc + c)),
            pl.BlockSpec((1, tc), lambda l, c: (0, c)),
        ],
        out_specs=[pl.BlockSpec((tl, tc), lambda l, c: (l, c))] * 2,
        out_shape=[out, out],
        compiler_params=_cparams("parallel", "parallel"),
        name="hyena_filter_taps",
    )(hdn, w3, w3, deltas.reshape(1, C_WIDTH))


NYQ_ROWS = 16


def _filter_dft_kernel(f_ref, a_ref, b_ref, p1_ref, p2_ref, hn_ref, *, tf, seq):
    fi = pl.program_id(1)
    row = fi * tf + lax.broadcasted_iota(jnp.int32, p1_ref.shape, 0)
    w = jnp.where(row == 0, 0.5 / seq, 1.0 / seq)
    p1_ref[...] = _dot(f_ref[0], a_ref[...]) * w
    p2_ref[...] = jnp.where(row == 0, 0.0, _dot(f_ref[1], b_ref[...]) * w)

    @pl.when(fi == 0)
    def _():
        hn_ref[...] = _dot(f_ref[1, 0:NYQ_ROWS, :], a_ref[...]) * (0.5 / seq)


def hyena_filter_dft(fmat, a, bm, tf=256, tc=512):
    seq = a.shape[0]
    return pl.pallas_call(
        functools.partial(_filter_dft_kernel, tf=tf, seq=seq),
        grid=(C_WIDTH // tc, seq // tf),
        in_specs=[
            pl.BlockSpec((2, tf, seq), lambda c, f: (0, f, 0)),
            pl.BlockSpec((seq, tc), lambda c, f: (0, c)),
            pl.BlockSpec((seq, tc), lambda c, f: (0, c)),
        ],
        out_specs=[
            pl.BlockSpec((tf, tc), lambda c, f: (f, c)),
            pl.BlockSpec((tf, tc), lambda c, f: (f, c)),
            pl.BlockSpec((NYQ_ROWS, tc), lambda c, f: (0, c)),
        ],
        out_shape=[
            jax.ShapeDtypeStruct((seq, C_WIDTH), F32),
            jax.ShapeDtypeStruct((seq, C_WIDTH), F32),
            jax.ShapeDtypeStruct((NYQ_ROWS, C_WIDTH), F32),
        ],
        compiler_params=_cparams("parallel", "arbitrary"),
        name="hyena_filter_dft",
    )(fmat, a, bm)


def _dft_fwd_kernel(f_ref, v_ref, p1_ref, p2_ref, hn_ref, y_ref, *, tf, seq):
    acc = _dot(f_ref[...].reshape(2 * tf, seq), v_ref[0])
    vr, vi = acc[:tf], acc[tf:]
    p1, p2 = p1_ref[...], p2_ref[...]
    row = pl.program_id(2) * tf + lax.broadcasted_iota(jnp.int32, p1.shape, 0)
    p3 = jnp.where(row == 0, hn_ref[0:1, :], p1)
    y_ref[0, 0] = (vr * p1 - vi * p2).astype(y_ref.dtype)
    y_ref[0, 1] = (vr * p2 + vi * p3).astype(y_ref.dtype)


def hyena_dft_fwd(fmat, vv, p1, p2, hn, tf=256, tc=1024):
    b, seq, _ = vv.shape
    return pl.pallas_call(
        functools.partial(_dft_fwd_kernel, tf=tf, seq=seq),
        grid=(b, C_WIDTH // tc, seq // tf),
        in_specs=[
            pl.BlockSpec((2, tf, seq), lambda bi, c, f: (0, f, 0)),
            pl.BlockSpec((1, seq, tc), lambda bi, c, f: (bi, 0, c)),
            pl.BlockSpec((tf, tc), lambda bi, c, f: (f, c)),
            pl.BlockSpec((tf, tc), lambda bi, c, f: (f, c)),
            pl.BlockSpec((NYQ_ROWS, tc), lambda bi, c, f: (0, c)),
        ],
        out_specs=pl.BlockSpec((1, 2, tf, tc), lambda bi, c, f: (bi, 0, f, c)),
        out_shape=jax.ShapeDtypeStruct((b, 2, seq, C_WIDTH), BF16),
        compiler_params=_cparams("parallel", "parallel", "parallel"),
        name="hyena_dft_fwd",
    )(fmat, vv, p1, p2, hn)


def _dft_inv_kernel(g_ref, y_ref, vv_ref, x0_ref, skip_ref, u_ref):
    y = _dot(g_ref[...], y_ref[0])
    vv = vv_ref[0].astype(F32)
    x0 = x0_ref[0].astype(F32)
    u_ref[0] = ((y + vv * skip_ref[...]) * x0).astype(u_ref.dtype)


def hyena_dft_inv(gmat, ys, vv, x0, skip, tt=512, tc=512):
    b, seq, _ = vv.shape
    return pl.pallas_call(
        _dft_inv_kernel,
        grid=(b, C_WIDTH // tc, seq // tt),
        in_specs=[
            pl.BlockSpec((tt, 2 * seq), lambda bi, c, t: (t, 0)),
            pl.BlockSpec((1, 2 * seq, tc), lambda bi, c, t: (bi, 0, c)),
            pl.BlockSpec((1, tt, tc), lambda bi, c, t: (bi, t, c)),
            pl.BlockSpec((1, tt, tc), lambda bi, c, t: (bi, t, c)),
            pl.BlockSpec((1, tc), lambda bi, c, t: (0, c)),
        ],
        out_specs=pl.BlockSpec((1, tt, tc), lambda bi, c, t: (bi, t, c)),
        out_shape=jax.ShapeDtypeStruct((b, seq, C_WIDTH), BF16),
        compiler_params=_cparams("parallel", "parallel", "parallel"),
        name="hyena_dft_inv",
    )(gmat, ys, vv, x0, skip.reshape(1, C_WIDTH))


def _dft_matrices(seq):
    idx = jnp.arange(seq, dtype=jnp.int32)
    ang = ((idx[:, None] * idx[None, :]) % (2 * seq)).astype(F32) * (math.pi / seq)
    sin_part = (-jnp.sin(ang)).at[0].set((1 - 2 * (idx % 2)).astype(F32))
    fmat = jnp.stack([jnp.cos(ang), sin_part]).astype(BF16)
    return fmat, fmat.reshape(2 * seq, seq).T


def _filter_hidden(seq, w1, b1, freq, w2, b2):
    t = jnp.linspace(0.0, 1.0, seq, dtype=F32)[:, None]
    bands = (FILTER_EMB - 1) // 2
    w = 2.0 * math.pi * jnp.arange(seq, dtype=F32) / seq
    f = jnp.linspace(1e-4, bands - 1, bands, dtype=F32)
    ang = w[:, None] * f[None, :]
    feats = jnp.concatenate([t, jnp.cos(ang), -jnp.sin(ang)], axis=-1)
    hdn = jnp.sin(freq[0] * (feats @ w1 + b1))
    return jnp.sin(freq[1] * (hdn @ w2 + b2))


def hyena_mixer(h, batch, seq, g, w_in, conv_w, conv_b, w1, b1, freq, w2, b2, w3, skip, w_out):
    z = norm_matmul(h, g, w_in, F32).reshape(batch, seq, 3 * C_WIDTH)
    x0, vv = hyena_pre(z, conv_w, conv_b)
    hdn = _filter_hidden(seq, w1, b1, freq, w2, b2)
    min_decay = math.log(FILTER_TARGET) / FAST_DECAY_PCT
    max_decay = math.log(FILTER_TARGET) / SLOW_DECAY_PCT
    deltas = jnp.abs(jnp.linspace(min_decay, max_decay, C_WIDTH, dtype=F32))
    a, bm = hyena_filter_taps(hdn, w3, deltas)
    fmat, gmat = _dft_matrices(seq)
    p1, p2, hn = hyena_filter_dft(fmat, a, bm)
    ys = hyena_dft_fwd(fmat, vv, p1, p2, hn).reshape(batch, 2 * seq, C_WIDTH)
    u = hyena_dft_inv(gmat, ys, vv, x0, skip)
    return proj_residual([u.reshape(batch * seq, C_WIDTH)], [w_out], h)


def even_mixer(h, batch, seq, g, w_in, w_out, lam_vec, subln_g, na_table, rel_table, layer_idx):
    z = norm_matmul(h, g, w_in, BF16).reshape(batch, seq, w_in.shape[1])
    lam_init = 0.8 - 0.6 * math.exp(-0.3 * layer_idx)
    lv = lam_vec.astype(F32)
    lam = jnp.exp(jnp.sum(lv[0] * lv[1])) - jnp.exp(jnp.sum(lv[2] * lv[3])) + lam_init
    oa = diff_attention(z, lam, rel_table, subln_g, lam_init)
    ob = neighbourhood_attention(z, na_table)
    n = batch * seq
    return proj_residual([oa.reshape(n, A_WIDTH), ob.reshape(n, B_WIDTH)],
                         [w_out[:A_WIDTH], w_out[A_WIDTH:]], h)


def _trunk(x, p, W):
    batch, seq, d = x.shape
    depth = p.shape[0]
    h = x.reshape(batch * seq, d)
    for i in range(depth):
        j = i // 2
        if i % 2 == 0:
            h = even_mixer(h, batch, seq, W['norm_mix'][i], W['ab_w_in'][j], W['ab_w_out'][j],
                           W['diff_lambda'][j], W['diff_subln'][j], W['na_bias'][j],
                           W['rel_bias_table'], i)
        else:
            h = hyena_mixer(h, batch, seq, W['norm_mix'][i], W['c_w_in'][j], W['c_conv_w'][j],
                            W['c_conv_b'][j], W['c_filt_w1'][j], W['c_filt_b1'][j],
                            W['c_filt_freq'][j], W['c_filt_w2'][j], W['c_filt_b2'][j],
                            W['c_filt_w3'][j], W['c_skip'][j], W['c_w_out'][j])
        h = ffn(h, W['norm_ffn'][i], W['ffn_w_in'][i], W['ffn_w_out'][i])
        h = ple(h, p[i].reshape(batch * seq, -1), W['norm_ple'][i], W['ple_w_gate'][i],
                W['ple_w_proj'][i], W['final_norm'], final_norm=(i == depth - 1))
    return h.reshape(batch, seq, d)


def kernel(x_prompt, x_sample, p_prompt, p_sample, rel_bias_table, norm_mix, norm_ffn, norm_ple, final_norm, ab_w_in, ab_w_out, diff_lambda, diff_subln, na_bias, c_w_in, c_conv_w, c_conv_b, c_filt_w1, c_filt_b1, c_filt_freq, c_filt_w2, c_filt_b2, c_filt_w3, c_skip, c_w_out, ffn_w_in, ffn_w_out, ple_w_proj, ple_w_gate):
    bf = lambda w: w.astype(BF16)
    W = dict(rel_bias_table=rel_bias_table, norm_mix=norm_mix, norm_ffn=norm_ffn,
             norm_ple=norm_ple, final_norm=final_norm, ab_w_in=bf(ab_w_in), ab_w_out=bf(ab_w_out),
             diff_lambda=diff_lambda, diff_subln=diff_subln, na_bias=na_bias,
             c_w_in=bf(c_w_in), c_conv_w=c_conv_w, c_conv_b=c_conv_b, c_filt_w1=c_filt_w1,
             c_filt_b1=c_filt_b1, c_filt_freq=c_filt_freq, c_filt_w2=c_filt_w2,
             c_filt_b2=c_filt_b2, c_filt_w3=bf(c_filt_w3), c_skip=c_skip, c_w_out=bf(c_w_out),
             ffn_w_in=bf(ffn_w_in), ffn_w_out=bf(ffn_w_out), ple_w_proj=bf(ple_w_proj),
             ple_w_gate=bf(ple_w_gate))
    return (_trunk(x_prompt, p_prompt, W), _trunk(x_sample, p_sample, W))
```

```python
import functools
import math

import numpy as np
import jax
import jax.numpy as jnp
from jax import lax
from jax.experimental import pallas as pl
from jax.experimental.pallas import tpu as pltpu

F32 = jnp.float32
BF16 = jnp.bfloat16

D_MODEL = 2048
GRID_W = 64
A_HEADS = 8
A_QK_DIM = 64
A_V_DIM = 2 * A_QK_DIM
A_WIDTH = A_HEADS * A_V_DIM
T5_BUCKETS = 32
T5_MAX_DIST = 128
B_HEADS = 8
B_HEAD_DIM = 128
B_WIDTH = B_HEADS * B_HEAD_DIM
NA_ROWS = 8
NA_COLS = 16
C_WIDTH = D_MODEL
FILTER_EMB = 33
FILTER_TARGET = 1e-2
FAST_DECAY_PCT = 0.3
SLOW_DECAY_PCT = 1.5
EPS = 1e-6

LANES = 128
NA_KEYS = NA_ROWS * GRID_W
MASK_VALUE = -1e30
LOG2E = math.log2(math.e)
NA_UNROLL = 8
VMEM_LIMIT = 56 * 1024 * 1024


def _cparams(*sem):
    return pltpu.CompilerParams(dimension_semantics=sem, vmem_limit_bytes=VMEM_LIMIT)


def _rms(x, g):
    ms = jnp.mean(x * x, axis=-1, keepdims=True)
    return x * lax.rsqrt(ms + EPS) * g


def _dot(a, b):
    return jnp.dot(a, b, preferred_element_type=F32)


def _dot_nt(a, b):
    return lax.dot_general(a, b, (((1,), (1,)), ((), ())), preferred_element_type=F32)


def _norm_matmul_kernel(x_ref, g_ref, w_ref, cs_ref, o_ref, xn_ref):
    @pl.when(pl.program_id(1) == 0)
    def _():
        xn_ref[...] = _rms(x_ref[...], g_ref[...]).astype(BF16)

    o_ref[...] = (_dot(xn_ref[...], w_ref[...]) * cs_ref[...]).astype(o_ref.dtype)


def norm_matmul(x, g, w, col_scale, out_dtype, tm=512, tn=2048):
    n, d = x.shape
    d_out = w.shape[1]
    return pl.pallas_call(
        _norm_matmul_kernel,
        grid=(n // tm, d_out // tn),
        in_specs=[
            pl.BlockSpec((tm, d), lambda i, j: (i, 0)),
            pl.BlockSpec((1, d), lambda i, j: (0, 0)),
            pl.BlockSpec((d, tn), lambda i, j: (0, j)),
            pl.BlockSpec((1, tn), lambda i, j: (0, j)),
        ],
        out_specs=pl.BlockSpec((tm, tn), lambda i, j: (i, j)),
        out_shape=jax.ShapeDtypeStruct((n, d_out), out_dtype),
        scratch_shapes=[pltpu.VMEM((tm, d), BF16)],
        compiler_params=_cparams("parallel", "arbitrary"),
        name="norm_matmul",
    )(x, g.reshape(1, d), w, col_scale.reshape(1, d_out))


def _proj_res_kernel(*refs, n_in):
    a_refs = refs[:n_in]
    w_refs = refs[n_in:2 * n_in]
    res_ref, o_ref = refs[2 * n_in], refs[2 * n_in + 1]
    acc = res_ref[...]
    for a_ref, w_ref in zip(a_refs, w_refs):
        acc = acc + _dot(a_ref[...], w_ref[...])
    o_ref[...] = acc


def proj_residual(a_list, w_list, res, tm=512):
    n, d = res.shape
    n_in = len(a_list)
    in_specs = [pl.BlockSpec((tm, a.shape[1]), lambda i: (i, 0)) for a in a_list]
    in_specs += [pl.BlockSpec(w.shape, lambda i: (0, 0)) for w in w_list]
    in_specs += [pl.BlockSpec((tm, d), lambda i: (i, 0))]
    return pl.pallas_call(
        functools.partial(_proj_res_kernel, n_in=n_in),
        grid=(n // tm,),
        in_specs=in_specs,
        out_specs=pl.BlockSpec((tm, d), lambda i: (i, 0)),
        out_shape=jax.ShapeDtypeStruct((n, d), F32),
        compiler_params=_cparams("parallel"),
        name="proj_residual",
    )(*a_list, *w_list, res)


def _ffn_kernel(x_ref, g_ref, wg_ref, wu_ref, wo_ref, o_ref, xn_ref):
    @pl.when(pl.program_id(1) == 0)
    def _():
        x = x_ref[...]
        xn_ref[...] = _rms(x, g_ref[...]).astype(BF16)
        o_ref[...] = x

    xn = xn_ref[...]
    gate = _dot(xn, wg_ref[...])
    up = _dot(xn, wu_ref[...])
    act = (gate * jax.nn.sigmoid(gate) * up).astype(BF16)
    o_ref[...] += _dot(act, wo_ref[...])


def ffn(x, g, w_in, w_out, tm=1024, tf=512):
    n, d = x.shape
    d_ff = w_out.shape[0]
    nf = d_ff // tf
    return pl.pallas_call(
        _ffn_kernel,
        grid=(n // tm, nf),
        in_specs=[
            pl.BlockSpec((tm, d), lambda i, f: (i, 0)),
            pl.BlockSpec((1, d), lambda i, f: (0, 0)),
            pl.BlockSpec((d, tf), lambda i, f: (0, f)),
            pl.BlockSpec((d, tf), lambda i, f: (0, nf + f)),
            pl.BlockSpec((tf, d), lambda i, f: (f, 0)),
        ],
        out_specs=pl.BlockSpec((tm, d), lambda i, f: (i, 0)),
        out_shape=jax.ShapeDtypeStruct((n, d), F32),
        scratch_shapes=[pltpu.VMEM((tm, d), BF16)],
        compiler_params=_cparams("parallel", "arbitrary"),
        name="ffn",
    )(x, g.reshape(1, d), w_in, w_in, w_out)


def _ple_kernel(x_ref, p_ref, g_ref, wg_ref, wp_ref, gf_ref, o_ref, *, final_norm):
    x = x_ref[...]
    gate = jax.nn.sigmoid(_dot(_rms(x, g_ref[...]).astype(BF16), wg_ref[...]))
    proj = _dot(p_ref[...].astype(BF16), wp_ref[...])
    y = x + proj * gate
    if final_norm:
        y = _rms(y, gf_ref[...])
    o_ref[...] = y


def ple(x, p, g, w_gate, w_proj, g_final, final_norm, tm=512):
    n, d = x.shape
    dp = p.shape[1]
    return pl.pallas_call(
        functools.partial(_ple_kernel, final_norm=final_norm),
        grid=(n // tm,),
        in_specs=[
            pl.BlockSpec((tm, d), lambda i: (i, 0)),
            pl.BlockSpec((tm, dp), lambda i: (i, 0)),
            pl.BlockSpec((1, d), lambda i: (0, 0)),
            pl.BlockSpec((d, d), lambda i: (0, 0)),
            pl.BlockSpec((dp, d), lambda i: (0, 0)),
            pl.BlockSpec((1, d), lambda i: (0, 0)),
        ],
        out_specs=pl.BlockSpec((tm, d), lambda i: (i, 0)),
        out_shape=jax.ShapeDtypeStruct((n, d), F32),
        compiler_params=_cparams("parallel"),
        name="ple",
    )(x, p, g.reshape(1, d), w_gate, w_proj, g_final.reshape(1, d))


def t5_bucket(rel):
    nb = T5_BUCKETS // 2
    max_exact = nb // 2
    ret = jnp.where(rel > 0, nb, 0)
    n = jnp.abs(rel)
    nf = jnp.maximum(n, 1).astype(jnp.float32)
    large = max_exact + (jnp.log(nf / max_exact) / math.log(T5_MAX_DIST / max_exact)
                         * (nb - max_exact)).astype(jnp.int32)
    large = jnp.minimum(large, nb - 1)
    return ret + jnp.where(n < max_exact, n, large)


def _lane_tiles(x):
    return [x[:, j * LANES:(j + 1) * LANES] for j in range(x.shape[1] // LANES)]


def _diff_attn_kernel(lam_ref, q_ref, k_ref, v_ref, e_ref, g_ref, o_ref,
                      s1_ref, s2_ref, p1_ref, p2_ref, va_ref, *, tq, tk, seq, pad, width,
                      out_scale):
    i = pl.program_id(2)

    @pl.when(i == 0)
    def _():
        va_ref[:, :A_V_DIM] = v_ref[0]
        va_ref[:, A_V_DIM:] = jnp.ones((seq, A_V_DIM), BF16)

    q = q_ref[0]
    lane = lax.broadcasted_iota(jnp.int32, q.shape, 1)
    zero = jnp.zeros_like(q)
    q1 = jnp.where(lane < A_QK_DIM, q, zero)
    q2 = jnp.where(lane < A_QK_DIM, zero, q)
    def bias_tile(j):
        c = j * LANES // tk
        start = jnp.clip(c * tk - i * tq + pad, 0, width - tk) + (j * LANES - c * tk)
        return e_ref[0, :, pl.ds(pl.multiple_of(start, LANES), LANES)]

    m1 = m2 = None
    for c in range(seq // tk):
        cols = slice(c * tk, (c + 1) * tk)
        kc = k_ref[0, cols, :]
        x1 = _dot_nt(q1, kc)
        x2 = _dot_nt(q2, kc)
        s1_ref[:, cols] = x1
        s2_ref[:, cols] = x2
        for jj, (t1, t2) in enumerate(zip(_lane_tiles(x1), _lane_tiles(x2))):
            bias = bias_tile(c * (tk // LANES) + jj)
            m1 = t1 + bias if m1 is None else jnp.maximum(m1, t1 + bias)
            m2 = t2 + bias if m2 is None else jnp.maximum(m2, t2 + bias)
    m1 = jnp.broadcast_to(jnp.max(m1, axis=-1, keepdims=True), (tq, LANES))
    m2 = jnp.broadcast_to(jnp.max(m2, axis=-1, keepdims=True), (tq, LANES))
    for j in range(seq // LANES):
        cols = slice(j * LANES, (j + 1) * LANES)
        shift = bias_tile(j)
        p1_ref[:, cols] = jnp.exp2(s1_ref[:, cols] + (shift - m1)).astype(BF16)
        p2_ref[:, cols] = jnp.exp2(s2_ref[:, cols] + (shift - m2)).astype(BF16)
    o1 = _dot(p1_ref[...], va_ref[...])
    o2 = _dot(p2_ref[...], va_ref[...])
    o = (o1[:, :A_V_DIM] * (1.0 / o1[:, A_V_DIM:])
         - o2[:, :A_V_DIM] * (lam_ref[0] / o2[:, A_V_DIM:]))
    o = _rms(o, g_ref[...]) * out_scale
    o_ref[0] = o.astype(o_ref.dtype)


def _toeplitz(w, rows, cols):
    n = rows + cols
    flat = jnp.tile(w, (1,) * (w.ndim - 1) + (rows,))[..., :rows * (n - 1)]
    return flat.reshape(w.shape[:-1] + (rows, n - 1))[..., rows - 1:rows - 1 + cols]


def diff_attention(z, lam, rel_table, subln_g, lam_init, tq=512, tk=512):
    b, seq, _ = z.shape
    pad = tk + T5_MAX_DIST
    width = 2 * tk + tq + 2 * T5_MAX_DIST
    rel = jnp.arange(tq + width, dtype=jnp.int32) - (tq - 1) - pad
    onehot = (t5_bucket(rel)[:, None] == jnp.arange(T5_BUCKETS)[None, :]).astype(F32)
    diag = jnp.dot(onehot, rel_table.astype(F32), precision=lax.Precision.HIGHEST)
    band = _toeplitz(diag.T * LOG2E, tq, width)
    hq = A_WIDTH // LANES
    return pl.pallas_call(
        functools.partial(_diff_attn_kernel, tq=tq, tk=tk, seq=seq, pad=pad, width=width,
                          out_scale=1.0 - lam_init),
        grid=(b, A_HEADS, seq // tq),
        in_specs=[
            pl.BlockSpec(memory_space=pltpu.SMEM),
            pl.BlockSpec((1, tq, LANES), lambda bi, h, i: (bi, i, h)),
            pl.BlockSpec((1, seq, LANES), lambda bi, h, i: (bi, 0, hq + h)),
            pl.BlockSpec((1, seq, LANES), lambda bi, h, i: (bi, 0, 2 * hq + h)),
            pl.BlockSpec((1, tq, width), lambda bi, h, i: (h, 0, 0)),
            pl.BlockSpec((1, A_V_DIM), lambda bi, h, i: (0, 0)),
        ],
        out_specs=pl.BlockSpec((1, tq, LANES), lambda bi, h, i: (bi, i, h)),
        out_shape=jax.ShapeDtypeStruct((b, seq, A_WIDTH), BF16),
        scratch_shapes=[pltpu.VMEM((tq, seq), F32), pltpu.VMEM((tq, seq), F32),
                        pltpu.VMEM((tq, seq), BF16), pltpu.VMEM((tq, seq), BF16),
                        pltpu.VMEM((seq, 2 * A_V_DIM), BF16)],
        compiler_params=_cparams("parallel", "parallel", "arbitrary"),
        name="diff_attention",
    )(lam.reshape(1).astype(F32), z, z, z, band, subln_g.reshape(1, A_V_DIM))


def _na_kernel(q_ref, k_ref, v_ref, nb_ref, o_ref, *, rows):
    def row_group(gi, carry):
        geom = []
        for u in range(NA_UNROLL):
            r = gi * NA_UNROLL + u
            r_start = jnp.clip(r - NA_ROWS // 2, 0, rows - NA_ROWS)
            geom.append((pl.multiple_of(r * GRID_W, GRID_W),
                         pl.multiple_of(r_start * GRID_W, GRID_W), r - r_start))
        scores = [_dot_nt(q_ref[0, pl.ds(q0, GRID_W), :], k_ref[0, pl.ds(k0, NA_KEYS), :])
                  + nb_ref[0, var] for q0, k0, var in geom]
        probs = []
        for s in scores:
            p = jnp.exp2(s - jnp.max(s, axis=-1, keepdims=True))
            probs.append((p * (1.0 / jnp.sum(p, axis=-1, keepdims=True))).astype(BF16))
        for p, (q0, k0, _) in zip(probs, geom):
            o = _dot(p, v_ref[0, pl.ds(k0, NA_KEYS), :])
            o_ref[0, pl.ds(q0, GRID_W), :] = o.astype(o_ref.dtype)
        return carry

    lax.fori_loop(0, rows // NA_UNROLL, row_group, 0)


def _na_bias(na_table):
    var = np.arange(NA_ROWS)[:, None, None]
    kr = np.arange(NA_ROWS)[None, :, None]
    row_sel = (kr - var + (NA_ROWS - 1) == np.arange(2 * NA_ROWS - 1)).astype(np.float32)
    c = np.arange(GRID_W)[:, None, None]
    kc = np.arange(GRID_W)[None, :, None]
    col_sel = (kc - c + (NA_COLS - 1) == np.arange(2 * NA_COLS - 1)).astype(np.float32)
    c_start = np.clip(c - NA_COLS // 2, 0, GRID_W - NA_COLS)
    valid = ((kc >= c_start) & (kc < c_start + NA_COLS))[None, None, :, None, :, 0]
    bias = jnp.einsum('vka,hab,cqb->hvckq', row_sel, na_table.astype(F32), col_sel,
                      precision=lax.Precision.HIGHEST)
    bias = jnp.where(valid, bias * LOG2E, MASK_VALUE)
    return bias.reshape(B_HEADS, NA_ROWS, GRID_W, NA_KEYS)


def neighbourhood_attention(z, na_table):
    b, seq, _ = z.shape
    rows = seq // GRID_W
    assert rows >= NA_ROWS and rows % NA_UNROLL == 0
    hq = B_WIDTH // LANES
    base = 3 * A_WIDTH // LANES
    return pl.pallas_call(
        functools.partial(_na_kernel, rows=rows),
        grid=(b, B_HEADS),
        in_specs=[
            pl.BlockSpec((1, seq, LANES), lambda bi, h: (bi, 0, base + h)),
            pl.BlockSpec((1, seq, LANES), lambda bi, h: (bi, 0, base + hq + h)),
            pl.BlockSpec((1, seq, LANES), lambda bi, h: (bi, 0, base + 2 * hq + h)),
            pl.BlockSpec((1, NA_ROWS, GRID_W, NA_KEYS), lambda bi, h: (h, 0, 0, 0)),
        ],
        out_specs=pl.BlockSpec((1, seq, LANES), lambda bi, h: (bi, 0, h)),
        out_shape=jax.ShapeDtypeStruct((b, seq, B_WIDTH), BF16),
        compiler_params=_cparams("parallel", "parallel"),
        name="neighbourhood_attention",
    )(z, z, z, _na_bias(na_table))


def _hyena_pre_kernel(x0_ref, x1_ref, v_ref, w0_ref, w1_ref, wv_ref, b0_ref, b1_ref, bv_ref,
                      x0_out, vv_out, *, seq):
    row = lax.broadcasted_iota(jnp.int32, x0_ref.shape[1:], 0)

    def conv(z_ref, w_ref, b_ref):
        z = z_ref[0]
        prev = jnp.where(row == 0, 0.0, pltpu.roll(z, 1, 0))
        nxt = jnp.where(row == seq - 1, 0.0, pltpu.roll(z, seq - 1, 0))
        return prev * w_ref[0:1, :] + z * w_ref[1:2, :] + nxt * w_ref[2:3, :] + b_ref[...]

    x0_out[0] = conv(x0_ref, w0_ref, b0_ref).astype(x0_out.dtype)
    vv = conv(v_ref, wv_ref, bv_ref) * conv(x1_ref, w1_ref, b1_ref)
    vv_out[0] = vv.astype(vv_out.dtype)


def hyena_pre(z, conv_w, conv_b, tc=128):
    b, seq, _ = z.shape
    nc = C_WIDTH // tc
    zspec = lambda o: pl.BlockSpec((1, seq, tc), lambda bi, c: (bi, 0, o * nc + c))
    wspec = lambda o: pl.BlockSpec((3, tc), lambda bi, c: (0, o * nc + c))
    bspec = lambda o: pl.BlockSpec((1, tc), lambda bi, c: (0, o * nc + c))
    cb = conv_b.reshape(1, -1)
    out = jax.ShapeDtypeStruct((b, seq, C_WIDTH), BF16)
    return pl.pallas_call(
        functools.partial(_hyena_pre_kernel, seq=seq),
        grid=(b, nc),
        in_specs=[zspec(0), zspec(1), zspec(2), wspec(0), wspec(1), wspec(2),
                  bspec(0), bspec(1), bspec(2)],
        out_specs=[pl.BlockSpec((1, seq, tc), lambda bi, c: (bi, 0, c))] * 2,
        out_shape=[out, out],
        compiler_params=_cparams("parallel", "parallel"),
        name="hyena_pre",
    )(z, z, z, conv_w, conv_w, conv_w, cb, cb, cb)


def _filter_kernel(hdn_ref, wf_ref, wb_ref, dl_ref, a_ref, b_ref, *, tl, seq):
    hd = hdn_ref[...].astype(BF16)
    row = pl.program_id(0) * tl + lax.broadcasted_iota(jnp.int32, a_ref.shape, 0)
    t = row.astype(F32) * (1.0 / (seq - 1))
    decay = jnp.exp(-t * dl_ref[...])
    hf = _dot(hd, wf_ref[...]) * decay
    hb = jnp.where(row == 0, 0.0, _dot(hd, wb_ref[...]) * decay)
    a_ref[...] = (hf + hb).astype(a_ref.dtype)
    b_ref[...] = (hf - hb).astype(b_ref.dtype)


def hyena_filter_taps(hdn, w3, deltas, tl=512, tc=512):
    seq = hdn.shape[0]
    nc = C_WIDTH // tc
    out = jax.ShapeDtypeStruct((seq, C_WIDTH), BF16)
    return pl.pallas_call(
        functools.partial(_filter_kernel, tl=tl, seq=seq),
        grid=(seq // tl, nc),
        in_specs=[
            pl.BlockSpec((tl, hdn.shape[1]), lambda l, c: (l, 0)),
            pl.BlockSpec((w3.shape[0], tc), lambda l, c: (0, c)),
            pl.BlockSpec((w3.shape[0], tc), lambda l, c: (0, nc + c)),
            pl.BlockSpec((1, tc), lambda l, c: (0, c)),
        ],
        out_specs=[pl.BlockSpec((tl, tc), lambda l, c: (l, c))] * 2,
        out_shape=[out, out],
        compiler_params=_cparams("parallel", "parallel"),
        name="hyena_filter_taps",
    )(hdn, w3, w3, deltas.reshape(1, C_WIDTH))


NYQ_ROWS = 16


def _filter_dft_kernel(f_ref, a_ref, b_ref, p1_ref, p2_ref, hn_ref, *, tf, seq):
    fi = pl.program_id(1)
    row = fi * tf + lax.broadcasted_iota(jnp.int32, p1_ref.shape, 0)
    w = jnp.where(row == 0, 0.5 / seq, 1.0 / seq)
    p1_ref[...] = _dot(f_ref[0], a_ref[...]) * w
    p2_ref[...] = jnp.where(row == 0, 0.0, _dot(f_ref[1], b_ref[...]) * w)

    @pl.when(fi == 0)
    def _():
        hn_ref[...] = _dot(f_ref[1, 0:NYQ_ROWS, :], a_ref[...]) * (0.5 / seq)


def hyena_filter_dft(fmat, a, bm, tf=256, tc=512):
    seq = a.shape[0]
    return pl.pallas_call(
        functools.partial(_filter_dft_kernel, tf=tf, seq=seq),
        grid=(C_WIDTH // tc, seq // tf),
        in_specs=[
            pl.BlockSpec((2, tf, seq), lambda c, f: (0, f, 0)),
            pl.BlockSpec((seq, tc), lambda c, f: (0, c)),
            pl.BlockSpec((seq, tc), lambda c, f: (0, c)),
        ],
        out_specs=[
            pl.BlockSpec((tf, tc), lambda c, f: (f, c)),
            pl.BlockSpec((tf, tc), lambda c, f: (f, c)),
            pl.BlockSpec((NYQ_ROWS, tc), lambda c, f: (0, c)),
        ],
        out_shape=[
            jax.ShapeDtypeStruct((seq, C_WIDTH), F32),
            jax.ShapeDtypeStruct((seq, C_WIDTH), F32),
            jax.ShapeDtypeStruct((NYQ_ROWS, C_WIDTH), F32),
        ],
        compiler_params=_cparams("parallel", "arbitrary"),
        name="hyena_filter_dft",
    )(fmat, a, bm)


def _dft_fwd_kernel(f_ref, v_ref, p1_ref, p2_ref, hn_ref, y_ref, *, tf, seq):
    acc = _dot(f_ref[...].reshape(2 * tf, seq), v_ref[0])
    vr, vi = acc[:tf], acc[tf:]
    p1, p2 = p1_ref[...], p2_ref[...]
    row = pl.program_id(2) * tf + lax.broadcasted_iota(jnp.int32, p1.shape, 0)
    p3 = jnp.where(row == 0, hn_ref[0:1, :], p1)
    y_ref[0, 0] = (vr * p1 - vi * p2).astype(y_ref.dtype)
    y_ref[0, 1] = (vr * p2 + vi * p3).astype(y_ref.dtype)


def hyena_dft_fwd(fmat, vv, p1, p2, hn, tf=256, tc=1024):
    b, seq, _ = vv.shape
    return pl.pallas_call(
        functools.partial(_dft_fwd_kernel, tf=tf, seq=seq),
        grid=(b, C_WIDTH // tc, seq // tf),
        in_specs=[
            pl.BlockSpec((2, tf, seq), lambda bi, c, f: (0, f, 0)),
            pl.BlockSpec((1, seq, tc), lambda bi, c, f: (bi, 0, c)),
            pl.BlockSpec((tf, tc), lambda bi, c, f: (f, c)),
            pl.BlockSpec((tf, tc), lambda bi, c, f: (f, c)),
            pl.BlockSpec((NYQ_ROWS, tc), lambda bi, c, f: (0, c)),
        ],
        out_specs=pl.BlockSpec((1, 2, tf, tc), lambda bi, c, f: (bi, 0, f, c)),
        out_shape=jax.ShapeDtypeStruct((b, 2, seq, C_WIDTH), BF16),
        compiler_params=_cparams("parallel", "parallel", "parallel"),
        name="hyena_dft_fwd",
    )(fmat, vv, p1, p2, hn)


def _dft_inv_kernel(g_ref, y_ref, vv_ref, x0_ref, skip_ref, u_ref):
    y = _dot(g_ref[...], y_ref[0])
    vv = vv_ref[0].astype(F32)
    x0 = x0_ref[0].astype(F32)
    u_ref[0] = ((y + vv * skip_ref[...]) * x0).astype(u_ref.dtype)


def hyena_dft_inv(gmat, ys, vv, x0, skip, tt=512, tc=512):
    b, seq, _ = vv.shape
    return pl.pallas_call(
        _dft_inv_kernel,
        grid=(b, C_WIDTH // tc, seq // tt),
        in_specs=[
            pl.BlockSpec((tt, 2 * seq), lambda bi, c, t: (t, 0)),
            pl.BlockSpec((1, 2 * seq, tc), lambda bi, c, t: (bi, 0, c)),
            pl.BlockSpec((1, tt, tc), lambda bi, c, t: (bi, t, c)),
            pl.BlockSpec((1, tt, tc), lambda bi, c, t: (bi, t, c)),
            pl.BlockSpec((1, tc), lambda bi, c, t: (0, c)),
        ],
        out_specs=pl.BlockSpec((1, tt, tc), lambda bi, c, t: (bi, t, c)),
        out_shape=jax.ShapeDtypeStruct((b, seq, C_WIDTH), BF16),
        compiler_params=_cparams("parallel", "parallel", "parallel"),
        name="hyena_dft_inv",
    )(gmat, ys, vv, x0, skip.reshape(1, C_WIDTH))


def _dft_matrices(seq):
    idx = jnp.arange(seq, dtype=jnp.int32)
    ang = ((idx[:, None] * idx[None, :]) % (2 * seq)).astype(F32) * (math.pi / seq)
    sin_part = (-jnp.sin(ang)).at[0].set((1 - 2 * (idx % 2)).astype(F32))
    fmat = jnp.stack([jnp.cos(ang), sin_part]).astype(BF16)
    return fmat, fmat.reshape(2 * seq, seq).T


def _filter_hidden(seq, w1, b1, freq, w2, b2):
    t = jnp.linspace(0.0, 1.0, seq, dtype=F32)[:, None]
    bands = (FILTER_EMB - 1) // 2
    w = 2.0 * math.pi * jnp.arange(seq, dtype=F32) / seq
    f = jnp.linspace(1e-4, bands - 1, bands, dtype=F32)
    ang = w[:, None] * f[None, :]
    feats = jnp.concatenate([t, jnp.cos(ang), -jnp.sin(ang)], axis=-1)
    hdn = jnp.sin(freq[0] * (feats @ w1 + b1))
    return jnp.sin(freq[1] * (hdn @ w2 + b2))


def hyena_mixer(h, batch, seq, g, w_in, conv_w, conv_b, w1, b1, freq, w2, b2, w3, skip, w_out):
    z = norm_matmul(h, g, w_in, jnp.ones((3 * C_WIDTH,), F32), F32)
    z = z.reshape(batch, seq, 3 * C_WIDTH)
    x0, vv = hyena_pre(z, conv_w, conv_b)
    hdn = _filter_hidden(seq, w1, b1, freq, w2, b2)
    min_decay = math.log(FILTER_TARGET) / FAST_DECAY_PCT
    max_decay = math.log(FILTER_TARGET) / SLOW_DECAY_PCT
    deltas = jnp.abs(jnp.linspace(min_decay, max_decay, C_WIDTH, dtype=F32))
    a, bm = hyena_filter_taps(hdn, w3, deltas)
    fmat, gmat = _dft_matrices(seq)
    p1, p2, hn = hyena_filter_dft(fmat, a, bm)
    ys = hyena_dft_fwd(fmat, vv, p1, p2, hn).reshape(batch, 2 * seq, C_WIDTH)
    u = hyena_dft_inv(gmat, ys, vv, x0, skip)
    return proj_residual([u.reshape(batch * seq, C_WIDTH)], [w_out], h)


def even_mixer(h, batch, seq, g, w_in, w_out, lam_vec, subln_g, na_table, rel_table, layer_idx):
    col_scale = np.ones((6, A_WIDTH), np.float32)
    col_scale[0] = A_QK_DIM ** -0.5 * LOG2E
    col_scale[3] = B_HEAD_DIM ** -0.5 * LOG2E
    z = norm_matmul(h, g, w_in, jnp.asarray(col_scale.reshape(-1)), BF16)
    z = z.reshape(batch, seq, w_in.shape[1])
    lam_init = 0.8 - 0.6 * math.exp(-0.3 * layer_idx)
    lv = lam_vec.astype(F32)
    lam = jnp.exp(jnp.sum(lv[0] * lv[1])) - jnp.exp(jnp.sum(lv[2] * lv[3])) + lam_init
    oa = diff_attention(z, lam, rel_table, subln_g, lam_init)
    ob = neighbourhood_attention(z, na_table)
    n = batch * seq
    return proj_residual([oa.reshape(n, A_WIDTH), ob.reshape(n, B_WIDTH)],
                         [w_out[:A_WIDTH], w_out[A_WIDTH:]], h)


def _trunk(x, p, W):
    batch, seq, d = x.shape
    depth = p.shape[0]
    h = x.reshape(batch * seq, d)
    for i in range(depth):
        j = i // 2
        if i % 2 == 0:
            h = even_mixer(h, batch, seq, W['norm_mix'][i], W['ab_w_in'][j], W['ab_w_out'][j],
                           W['diff_lambda'][j], W['diff_subln'][j], W['na_bias'][j],
                           W['rel_bias_table'], i)
        else:
            h = hyena_mixer(h, batch, seq, W['norm_mix'][i], W['c_w_in'][j], W['c_conv_w'][j],
                            W['c_conv_b'][j], W['c_filt_w1'][j], W['c_filt_b1'][j],
                            W['c_filt_freq'][j], W['c_filt_w2'][j], W['c_filt_b2'][j],
                            W['c_filt_w3'][j], W['c_skip'][j], W['c_w_out'][j])
        h = ffn(h, W['norm_ffn'][i], W['ffn_w_in'][i], W['ffn_w_out'][i])
        h = ple(h, p[i].reshape(batch * seq, -1), W['norm_ple'][i], W['ple_w_gate'][i],
                W['ple_w_proj'][i], W['final_norm'], final_norm=(i == depth - 1))
    return h.reshape(batch, seq, d)


def kernel(x_prompt, x_sample, p_prompt, p_sample, rel_bias_table, norm_mix, norm_ffn, norm_ple, final_norm, ab_w_in, ab_w_out, diff_lambda, diff_subln, na_bias, c_w_in, c_conv_w, c_conv_b, c_filt_w1, c_filt_b1, c_filt_freq, c_filt_w2, c_filt_b2, c_filt_w3, c_skip, c_w_out, ffn_w_in, ffn_w_out, ple_w_proj, ple_w_gate):
    bf = lambda w: w.astype(BF16)
    W = dict(rel_bias_table=rel_bias_table, norm_mix=norm_mix, norm_ffn=norm_ffn,
             norm_ple=norm_ple, final_norm=final_norm, ab_w_in=bf(ab_w_in), ab_w_out=bf(ab_w_out),
             diff_lambda=diff_lambda, diff_subln=diff_subln, na_bias=na_bias,
             c_w_in=bf(c_w_in), c_conv_w=c_conv_w, c_conv_b=c_conv_b, c_filt_w1=c_filt_w1,
             c_filt_b1=c_filt_b1, c_filt_freq=c_filt_freq, c_filt_w2=c_filt_w2,
             c_filt_b2=c_filt_b2, c_filt_w3=bf(c_filt_w3), c_skip=c_skip, c_w_out=bf(c_w_out),
             ffn_w_in=bf(ffn_w_in), ffn_w_out=bf(ffn_w_out), ple_w_proj=bf(ple_w_proj),
             ple_w_gate=bf(ple_w_gate))
    return (_trunk(x_prompt, p_prompt, W), _trunk(x_sample, p_sample, W))
```

```python
import functools
import math

import numpy as np
import jax
import jax.numpy as jnp
from jax import lax
from jax.experimental import pallas as pl
from jax.experimental.pallas import tpu as pltpu

F32 = jnp.float32
BF16 = jnp.bfloat16

D_MODEL = 2048
GRID_W = 64
A_HEADS = 8
A_QK_DIM = 64
A_V_DIM = 2 * A_QK_DIM
A_WIDTH = A_HEADS * A_V_DIM
T5_BUCKETS = 32
T5_MAX_DIST = 128
B_HEADS = 8
B_HEAD_DIM = 128
B_WIDTH = B_HEADS * B_HEAD_DIM
NA_ROWS = 8
NA_COLS = 16
C_WIDTH = D_MODEL
FILTER_EMB = 33
FILTER_HIDDEN = 64
FILTER_TARGET = 1e-2
FAST_DECAY_PCT = 0.3
SLOW_DECAY_PCT = 1.5
EPS = 1e-6

LANES = 128
NA_KEYS = NA_ROWS * GRID_W
MASK_VALUE = -1e30
LOG2E = math.log2(math.e)
NA_UNROLL = 8
VMEM_LIMIT = 56 * 1024 * 1024


def _cparams(*sem):
    return pltpu.CompilerParams(dimension_semantics=sem, vmem_limit_bytes=VMEM_LIMIT)


def _rms(x, g):
    ms = jnp.mean(x * x, axis=-1, keepdims=True)
    return x * lax.rsqrt(ms + EPS) * g


def _dot(a, b):
    return jnp.dot(a, b, preferred_element_type=F32)


def _dot_nt(a, b):
    return lax.dot_general(a, b, (((1,), (1,)), ((), ())), preferred_element_type=F32)


def _norm_matmul_kernel(x_ref, g_ref, w_ref, cs_ref, o_ref, xn_ref):
    @pl.when(pl.program_id(2) == 0)
    def _():
        xn_ref[...] = _rms(x_ref[...], g_ref[...]).astype(BF16)

    o_ref[...] = (_dot(xn_ref[...], w_ref[...]) * cs_ref[...]).astype(o_ref.dtype)


def _parity_rows(tokens, d, tm, split_seq):
    if split_seq is None:
        return (tokens, d), 1, (lambda r, i: (i, 0)), (lambda r, i: i)
    nsb = split_seq // 2 // tm
    return ((tokens // 2, 2 * d), 2, (lambda r, i: (i, r)),
            (lambda r, i: (2 * (i // nsb) + r) * nsb + i % nsb))


def norm_matmul(x, g, w, col_scale, out_dtype, split_seq=None, tm=512, tn=2048):
    n, d = x.shape
    d_out = w.shape[1]
    view, parities, nat_map, split_row = _parity_rows(n, d, tm, split_seq)
    return pl.pallas_call(
        _norm_matmul_kernel,
        grid=(parities, n // parities // tm, d_out // tn),
        in_specs=[
            pl.BlockSpec((tm, d), lambda r, i, j: nat_map(r, i)),
            pl.BlockSpec((1, d), lambda r, i, j: (0, 0)),
            pl.BlockSpec((d, tn), lambda r, i, j: (0, j)),
            pl.BlockSpec((1, tn), lambda r, i, j: (0, j)),
        ],
        out_specs=pl.BlockSpec((tm, tn), lambda r, i, j: (split_row(r, i), j)),
        out_shape=jax.ShapeDtypeStruct((n, d_out), out_dtype),
        scratch_shapes=[pltpu.VMEM((tm, d), BF16)],
        compiler_params=_cparams("parallel", "parallel", "arbitrary"),
        name="norm_matmul",
    )(x.reshape(view), g.reshape(1, d), w, col_scale.reshape(1, d_out))


def _proj_res_kernel(*refs, n_in):
    a_refs = refs[:n_in]
    w_refs = refs[n_in:2 * n_in]
    res_ref, o_ref = refs[2 * n_in], refs[2 * n_in + 1]
    acc = res_ref[...]
    for a_ref, w_ref in zip(a_refs, w_refs):
        acc = acc + _dot(a_ref[...], w_ref[...])
    o_ref[...] = acc


def proj_residual(a_list, w_list, res, split_seq=None, tm=512):
    n, d = res.shape
    n_in = len(a_list)
    view, parities, nat_map, split_row = _parity_rows(n, d, tm, split_seq)
    in_specs = [pl.BlockSpec((tm, a.shape[1]), lambda r, i: (split_row(r, i), 0)) for a in a_list]
    in_specs += [pl.BlockSpec(w.shape, lambda r, i: (0, 0)) for w in w_list]
    in_specs += [pl.BlockSpec((tm, d), nat_map)]
    out = pl.pallas_call(
        functools.partial(_proj_res_kernel, n_in=n_in),
        grid=(parities, n // parities // tm),
        in_specs=in_specs,
        out_specs=pl.BlockSpec((tm, d), nat_map),
        out_shape=jax.ShapeDtypeStruct(view, F32),
        compiler_params=_cparams("parallel", "parallel"),
        name="proj_residual",
    )(*a_list, *w_list, res.reshape(view))
    return out.reshape(n, d)


def _ffn_kernel(x_ref, g_ref, wg_ref, wu_ref, wo_ref, o_ref, xn_ref):
    @pl.when(pl.program_id(1) == 0)
    def _():
        x = x_ref[...]
        xn_ref[...] = _rms(x, g_ref[...]).astype(BF16)
        o_ref[...] = x

    xn = xn_ref[...]
    gate = _dot(xn, wg_ref[...])
    up = _dot(xn, wu_ref[...])
    act = (gate * jax.nn.sigmoid(gate) * up).astype(BF16)
    o_ref[...] += _dot(act, wo_ref[...])


def ffn(x, g, w_in, w_out, tm=1024, tf=512):
    n, d = x.shape
    d_ff = w_out.shape[0]
    nf = d_ff // tf
    return pl.pallas_call(
        _ffn_kernel,
        grid=(n // tm, nf),
        in_specs=[
            pl.BlockSpec((tm, d), lambda i, f: (i, 0)),
            pl.BlockSpec((1, d), lambda i, f: (0, 0)),
            pl.BlockSpec((d, tf), lambda i, f: (0, f)),
            pl.BlockSpec((d, tf), lambda i, f: (0, nf + f)),
            pl.BlockSpec((tf, d), lambda i, f: (f, 0)),
        ],
        out_specs=pl.BlockSpec((tm, d), lambda i, f: (i, 0)),
        out_shape=jax.ShapeDtypeStruct((n, d), F32),
        scratch_shapes=[pltpu.VMEM((tm, d), BF16)],
        compiler_params=_cparams("parallel", "arbitrary"),
        name="ffn",
    )(x, g.reshape(1, d), w_in, w_in, w_out)


def _ple_kernel(x_ref, p_ref, g_ref, wg_ref, wp_ref, gf_ref, o_ref, *, final_norm):
    x = x_ref[...]
    gate = jax.nn.sigmoid(_dot(_rms(x, g_ref[...]).astype(BF16), wg_ref[...]))
    proj = _dot(p_ref[...].astype(BF16), wp_ref[...])
    y = x + proj * gate
    if final_norm:
        y = _rms(y, gf_ref[...])
    o_ref[...] = y


def ple(x, p, g, w_gate, w_proj, g_final, final_norm, tm=512):
    n, d = x.shape
    dp = p.shape[1]
    return pl.pallas_call(
        functools.partial(_ple_kernel, final_norm=final_norm),
        grid=(n // tm,),
        in_specs=[
            pl.BlockSpec((tm, d), lambda i: (i, 0)),
            pl.BlockSpec((tm, dp), lambda i: (i, 0)),
            pl.BlockSpec((1, d), lambda i: (0, 0)),
            pl.BlockSpec((d, d), lambda i: (0, 0)),
            pl.BlockSpec((dp, d), lambda i: (0, 0)),
            pl.BlockSpec((1, d), lambda i: (0, 0)),
        ],
        out_specs=pl.BlockSpec((tm, d), lambda i: (i, 0)),
        out_shape=jax.ShapeDtypeStruct((n, d), F32),
        compiler_params=_cparams("parallel"),
        name="ple",
    )(x, p, g.reshape(1, d), w_gate, w_proj, g_final.reshape(1, d))


def t5_bucket(rel):
    nb = T5_BUCKETS // 2
    max_exact = nb // 2
    ret = jnp.where(rel > 0, nb, 0)
    n = jnp.abs(rel)
    nf = jnp.maximum(n, 1).astype(jnp.float32)
    large = max_exact + (jnp.log(nf / max_exact) / math.log(T5_MAX_DIST / max_exact)
                         * (nb - max_exact)).astype(jnp.int32)
    large = jnp.minimum(large, nb - 1)
    return ret + jnp.where(n < max_exact, n, large)


def _lane_tiles(x):
    return [x[:, j * LANES:(j + 1) * LANES] for j in range(x.shape[1] // LANES)]


def _diff_attn_kernel(lam_ref, q_ref, k_ref, v_ref, e_ref, g_ref, o_ref,
                      s1_ref, s2_ref, p1_ref, p2_ref, va_ref, *, tq, tk, seq, pad, width,
                      out_scale):
    i = pl.program_id(2)

    @pl.when(i == 0)
    def _():
        va_ref[:, :A_V_DIM] = v_ref[0]
        va_ref[:, A_V_DIM:] = jnp.ones((seq, A_V_DIM), BF16)

    q = q_ref[0]
    lane = lax.broadcasted_iota(jnp.int32, q.shape, 1)
    zero = jnp.zeros_like(q)
    q1 = jnp.where(lane < A_QK_DIM, q, zero)
    q2 = jnp.where(lane < A_QK_DIM, zero, q)
    def bias_tile(j):
        c = j * LANES // tk
        start = jnp.clip(c * tk - i * tq + pad, 0, width - tk) + (j * LANES - c * tk)
        return e_ref[0, :, pl.ds(pl.multiple_of(start, LANES), LANES)]

    m1 = m2 = None
    for c in range(seq // tk):
        cols = slice(c * tk, (c + 1) * tk)
        kc = k_ref[0, cols, :]
        x1 = _dot_nt(q1, kc)
        x2 = _dot_nt(q2, kc)
        s1_ref[:, cols] = x1
        s2_ref[:, cols] = x2
        for jj, (t1, t2) in enumerate(zip(_lane_tiles(x1), _lane_tiles(x2))):
            bias = bias_tile(c * (tk // LANES) + jj)
            m1 = t1 + bias if m1 is None else jnp.maximum(m1, t1 + bias)
            m2 = t2 + bias if m2 is None else jnp.maximum(m2, t2 + bias)
    m1 = jnp.broadcast_to(jnp.max(m1, axis=-1, keepdims=True), (tq, LANES))
    m2 = jnp.broadcast_to(jnp.max(m2, axis=-1, keepdims=True), (tq, LANES))
    for j in range(seq // LANES):
        cols = slice(j * LANES, (j + 1) * LANES)
        shift = bias_tile(j)
        p1_ref[:, cols] = jnp.exp2(s1_ref[:, cols] + (shift - m1)).astype(BF16)
        p2_ref[:, cols] = jnp.exp2(s2_ref[:, cols] + (shift - m2)).astype(BF16)
    o1 = _dot(p1_ref[...], va_ref[...])
    o2 = _dot(p2_ref[...], va_ref[...])
    o = (o1[:, :A_V_DIM] * (1.0 / o1[:, A_V_DIM:])
         - o2[:, :A_V_DIM] * (lam_ref[0] / o2[:, A_V_DIM:]))
    o = _rms(o, g_ref[...]) * out_scale
    o_ref[0] = o.astype(o_ref.dtype)


def _band_kernel(w_ref, o_ref, *, rows, cols):
    n = rows + cols
    x = jnp.broadcast_to(w_ref[0], (rows, n))
    y = pltpu.roll(x, n - (rows - 1), 1, stride=1, stride_axis=0)
    o_ref[0] = y[:, :cols]


def _toeplitz(w, rows, cols):
    heads, n = w.shape
    assert n == rows + cols and n % LANES == 0
    return pl.pallas_call(
        functools.partial(_band_kernel, rows=rows, cols=cols),
        grid=(heads,),
        in_specs=[pl.BlockSpec((1, 1, n), lambda h: (h, 0, 0))],
        out_specs=pl.BlockSpec((1, rows, cols), lambda h: (h, 0, 0)),
        out_shape=jax.ShapeDtypeStruct((heads, rows, cols), F32),
        compiler_params=_cparams("parallel"),
        name="toeplitz_band",
    )(w.reshape(heads, 1, n))


def diff_attention(z, lam, rel_table, subln_g, lam_init, tq=512, tk=512):
    b, seq, _ = z.shape
    pad = tk + T5_MAX_DIST
    width = 2 * tk + tq + 2 * T5_MAX_DIST
    rel = jnp.arange(tq + width, dtype=jnp.int32) - (tq - 1) - pad
    onehot = (t5_bucket(rel)[:, None] == jnp.arange(T5_BUCKETS)[None, :]).astype(F32)
    diag = jnp.dot(onehot, rel_table.astype(F32), precision=lax.Precision.HIGHEST)
    band = _toeplitz(diag.T * LOG2E, tq, width)
    hq = A_WIDTH // LANES
    return pl.pallas_call(
        functools.partial(_diff_attn_kernel, tq=tq, tk=tk, seq=seq, pad=pad, width=width,
                          out_scale=1.0 - lam_init),
        grid=(b, A_HEADS, seq // tq),
        in_specs=[
            pl.BlockSpec(memory_space=pltpu.SMEM),
            pl.BlockSpec((1, tq, LANES), lambda bi, h, i: (bi, i, h)),
            pl.BlockSpec((1, seq, LANES), lambda bi, h, i: (bi, 0, hq + h)),
            pl.BlockSpec((1, seq, LANES), lambda bi, h, i: (bi, 0, 2 * hq + h)),
            pl.BlockSpec((1, tq, width), lambda bi, h, i: (h, 0, 0)),
            pl.BlockSpec((1, A_V_DIM), lambda bi, h, i: (0, 0)),
        ],
        out_specs=pl.BlockSpec((1, tq, LANES), lambda bi, h, i: (bi, i, h)),
        out_shape=jax.ShapeDtypeStruct((b, seq, A_WIDTH), BF16),
        scratch_shapes=[pltpu.VMEM((tq, seq), F32), pltpu.VMEM((tq, seq), F32),
                        pltpu.VMEM((tq, seq), BF16), pltpu.VMEM((tq, seq), BF16),
                        pltpu.VMEM((seq, 2 * A_V_DIM), BF16)],
        compiler_params=_cparams("parallel", "parallel", "arbitrary"),
        name="diff_attention",
    )(lam.reshape(1).astype(F32), z, z, z, band, subln_g.reshape(1, A_V_DIM))


def _na_kernel(q_ref, k_ref, v_ref, nb_ref, o_ref, *, rows):
    def row_group(gi, carry):
        geom = []
        for u in range(NA_UNROLL):
            r = gi * NA_UNROLL + u
            r_start = jnp.clip(r - NA_ROWS // 2, 0, rows - NA_ROWS)
            geom.append((pl.multiple_of(r * GRID_W, GRID_W),
                         pl.multiple_of(r_start * GRID_W, GRID_W), r - r_start))
        scores = [_dot_nt(q_ref[0, pl.ds(q0, GRID_W), :], k_ref[0, pl.ds(k0, NA_KEYS), :])
                  + nb_ref[0, var] for q0, k0, var in geom]
        probs = []
        for s in scores:
            p = jnp.exp2(s - jnp.max(s, axis=-1, keepdims=True))
            probs.append((p * (1.0 / jnp.sum(p, axis=-1, keepdims=True))).astype(BF16))
        for p, (q0, k0, _) in zip(probs, geom):
            o = _dot(p, v_ref[0, pl.ds(k0, NA_KEYS), :])
            o_ref[0, pl.ds(q0, GRID_W), :] = o.astype(o_ref.dtype)
        return carry

    lax.fori_loop(0, rows // NA_UNROLL, row_group, 0)


def _na_bias(na_table):
    var = np.arange(NA_ROWS)[:, None, None]
    kr = np.arange(NA_ROWS)[None, :, None]
    row_sel = (kr - var + (NA_ROWS - 1) == np.arange(2 * NA_ROWS - 1)).astype(np.float32)
    c = np.arange(GRID_W)[:, None, None]
    kc = np.arange(GRID_W)[None, :, None]
    col_sel = (kc - c + (NA_COLS - 1) == np.arange(2 * NA_COLS - 1)).astype(np.float32)
    c_start = np.clip(c - NA_COLS // 2, 0, GRID_W - NA_COLS)
    valid = ((kc >= c_start) & (kc < c_start + NA_COLS))[None, None, :, None, :, 0]
    bias = jnp.einsum('vka,hab,cqb->hvckq', row_sel, na_table.astype(F32), col_sel,
                      precision=lax.Precision.HIGHEST)
    bias = jnp.where(valid, bias * LOG2E, MASK_VALUE)
    return bias.reshape(B_HEADS, NA_ROWS, GRID_W, NA_KEYS)


def neighbourhood_attention(z, na_table):
    b, seq, _ = z.shape
    rows = seq // GRID_W
    assert rows >= NA_ROWS and rows % NA_UNROLL == 0
    hq = B_WIDTH // LANES
    base = 3 * A_WIDTH // LANES
    return pl.pallas_call(
        functools.partial(_na_kernel, rows=rows),
        grid=(b, B_HEADS),
        in_specs=[
            pl.BlockSpec((1, seq, LANES), lambda bi, h: (bi, 0, base + h)),
            pl.BlockSpec((1, seq, LANES), lambda bi, h: (bi, 0, base + hq + h)),
            pl.BlockSpec((1, seq, LANES), lambda bi, h: (bi, 0, base + 2 * hq + h)),
            pl.BlockSpec((1, NA_ROWS, GRID_W, NA_KEYS), lambda bi, h: (h, 0, 0, 0)),
        ],
        out_specs=pl.BlockSpec((1, seq, LANES), lambda bi, h: (bi, 0, h)),
        out_shape=jax.ShapeDtypeStruct((b, seq, B_WIDTH), BF16),
        compiler_params=_cparams("parallel", "parallel"),
        name="neighbourhood_attention",
    )(z, z, z, _na_bias(na_table))


def _hyena_pre_kernel(x0_ref, x1_ref, v_ref, w0_ref, w1_ref, wv_ref, b0_ref, b1_ref, bv_ref,
                      x0_out, vv_out, *, half):
    row = lax.broadcasted_iota(jnp.int32, x0_ref.shape[2:], 0)

    def conv(z_ref, w_ref, b_ref):
        ze, zo = z_ref[0, 0], z_ref[0, 1]
        zo_prev = jnp.where(row == 0, 0.0, pltpu.roll(zo, 1, 0))
        ze_next = jnp.where(row == half - 1, 0.0, pltpu.roll(ze, half - 1, 0))
        w0, w1, w2, b = w_ref[0:1, :], w_ref[1:2, :], w_ref[2:3, :], b_ref[...]
        return zo_prev * w0 + ze * w1 + zo * w2 + b, ze * w0 + zo * w1 + ze_next * w2 + b

    x0e, x0o = conv(x0_ref, w0_ref, b0_ref)
    x1e, x1o = conv(x1_ref, w1_ref, b1_ref)
    ve, vo = conv(v_ref, wv_ref, bv_ref)
    x0_out[0, 0] = x0e.astype(x0_out.dtype)
    x0_out[0, 1] = x0o.astype(x0_out.dtype)
    vv_out[0, 0] = (ve * x1e).astype(vv_out.dtype)
    vv_out[0, 1] = (vo * x1o).astype(vv_out.dtype)


def hyena_pre(z, conv_w, conv_b, tc=128):
    b, _, half, _ = z.shape
    nc = C_WIDTH // tc
    zspec = lambda o: pl.BlockSpec((1, 2, half, tc), lambda bi, c: (bi, 0, 0, o * nc + c))
    wspec = lambda o: pl.BlockSpec((3, tc), lambda bi, c: (0, o * nc + c))
    bspec = lambda o: pl.BlockSpec((1, tc), lambda bi, c: (0, o * nc + c))
    cb = conv_b.reshape(1, -1)
    out = jax.ShapeDtypeStruct((b, 2, half, C_WIDTH), BF16)
    return pl.pallas_call(
        functools.partial(_hyena_pre_kernel, half=half),
        grid=(b, nc),
        in_specs=[zspec(0), zspec(1), zspec(2), wspec(0), wspec(1), wspec(2),
                  bspec(0), bspec(1), bspec(2)],
        out_specs=[pl.BlockSpec((1, 2, half, tc), lambda bi, c: (bi, 0, 0, c))] * 2,
        out_shape=[out, out],
        compiler_params=_cparams("parallel", "parallel"),
        name="hyena_pre",
    )(z, z, z, conv_w, conv_w, conv_w, cb, cb, cb)


def _filter_kernel(hdn_ref, wf_ref, wb_ref, dl_ref, ae_ref, be_ref, fo_ref, bo_ref, *, tl, seq):
    hd = hdn_ref[...].astype(BF16)
    lane = lax.broadcasted_iota(jnp.int32, hd.shape, 1)
    zero = jnp.zeros_like(hd)
    hd_e = jnp.where(lane < FILTER_HIDDEN, hd, zero)
    hd_o = jnp.where(lane < FILTER_HIDDEN, zero, hd)
    j = pl.program_id(0) * tl + lax.broadcasted_iota(jnp.int32, ae_ref.shape, 0)
    dl = dl_ref[...]
    decay_e = jnp.exp((2 * j).astype(F32) * (-1.0 / (seq - 1)) * dl)
    decay_o = jnp.exp((2 * j + 1).astype(F32) * (-1.0 / (seq - 1)) * dl)
    wf, wb = wf_ref[...], wb_ref[...]
    hf_e = _dot(hd_e, wf) * decay_e
    hb_e = jnp.where(j == 0, 0.0, _dot(hd_e, wb) * decay_e)
    ae_ref[...] = (hf_e + hb_e).astype(ae_ref.dtype)
    be_ref[...] = (hf_e - hb_e).astype(be_ref.dtype)
    fo_ref[...] = (_dot(hd_o, wf) * decay_o).astype(fo_ref.dtype)
    bo_ref[...] = (_dot(hd_o, wb) * decay_o).astype(bo_ref.dtype)


def hyena_filter_taps(hdn, w3, deltas, tl=512, tc=512):
    seq, hidden = hdn.shape
    half = seq // 2
    nc = C_WIDTH // tc
    w3s = jnp.concatenate([w3, w3], axis=0)
    out = jax.ShapeDtypeStruct((half, C_WIDTH), BF16)
    return pl.pallas_call(
        functools.partial(_filter_kernel, tl=tl, seq=seq),
        grid=(half // tl, nc),
        in_specs=[
            pl.BlockSpec((tl, 2 * hidden), lambda l, c: (l, 0)),
            pl.BlockSpec((2 * hidden, tc), lambda l, c: (0, c)),
            pl.BlockSpec((2 * hidden, tc), lambda l, c: (0, nc + c)),
            pl.BlockSpec((1, tc), lambda l, c: (0, c)),
        ],
        out_specs=[pl.BlockSpec((tl, tc), lambda l, c: (l, c))] * 4,
        out_shape=[out] * 4,
        compiler_params=_cparams("parallel", "parallel"),
        name="hyena_filter_taps",
    )(hdn.reshape(half, 2 * hidden), w3s, w3s, deltas.reshape(1, C_WIDTH))


NYQ_ROWS = 16


def _filter_dft_kernel(m_ref, tw_ref, ae_ref, be_ref, fo_ref, bo_ref,
                       sr_ref, si_ref, tr_ref, ti_ref, ur_ref, ui_ref, nyq_ref, *, tf, seq):
    fi = pl.program_id(1)
    cos_m, sin_m = m_ref[0], m_ref[1]
    fo, bo = fo_ref[...], bo_ref[...]
    row0 = (fi * tf + lax.broadcasted_iota(jnp.int32, sr_ref.shape, 0)) == 0
    e_r = _dot(cos_m, ae_ref[...])
    e_i = jnp.where(row0, 0.0, _dot(sin_m, be_ref[...]))
    nyq_f, nyq_b = _dot(sin_m, fo), _dot(sin_m, bo)
    f_r, f_i = _dot(cos_m, fo), jnp.where(row0, 0.0, nyq_f)
    b_r, b_i = _dot(cos_m, bo), jnp.where(row0, 0.0, -nyq_b)
    cw, sw = tw_ref[:, 0:1], tw_ref[:, 1:2]
    wt = jnp.where(row0, 1.0 / seq, 2.0 / seq)
    sr_ref[...] = e_r * wt
    si_ref[...] = e_i * wt
    tr_ref[...] = (cw * f_r + sw * f_i + b_r) * wt
    ti_ref[...] = (cw * f_i - sw * f_r + b_i) * wt
    ur_ref[...] = (f_r + cw * b_r - sw * b_i) * wt
    ui_ref[...] = (f_i + cw * b_i + sw * b_r) * wt

    @pl.when(fi == 0)
    def _():
        e_n = _dot(sin_m[0:NYQ_ROWS, :], ae_ref[...])
        f_n, b_n = nyq_f[0:NYQ_ROWS, :], nyq_b[0:NYQ_ROWS, :]
        nyq_ref[0] = e_n * (1.0 / seq)
        nyq_ref[1] = (b_n - f_n) * (1.0 / seq)
        nyq_ref[2] = (f_n - b_n) * (1.0 / seq)


def hyena_filter_dft(mmat, twiddle, taps, tf=256, tc=512):
    half = mmat.shape[1]
    coef = jax.ShapeDtypeStruct((half, C_WIDTH), F32)
    tap_spec = pl.BlockSpec((half, tc), lambda c, f: (0, c))
    coef_spec = pl.BlockSpec((tf, tc), lambda c, f: (f, c))
    return pl.pallas_call(
        functools.partial(_filter_dft_kernel, tf=tf, seq=2 * half),
        grid=(C_WIDTH // tc, half // tf),
        in_specs=[pl.BlockSpec((2, tf, half), lambda c, f: (0, f, 0)),
                  pl.BlockSpec((tf, 2), lambda c, f: (f, 0))] + [tap_spec] * 4,
        out_specs=[coef_spec] * 6 + [pl.BlockSpec((3, NYQ_ROWS, tc), lambda c, f: (0, 0, c))],
        out_shape=[coef] * 6 + [jax.ShapeDtypeStruct((3, NYQ_ROWS, C_WIDTH), F32)],
        compiler_params=_cparams("parallel", "arbitrary"),
        name="hyena_filter_dft",
    )(mmat, twiddle, *taps)


def _dft_fwd_kernel(m_ref, ve_ref, vo_ref, sr_ref, si_ref, tr_ref, ti_ref, ur_ref, ui_ref,
                    nyq_ref, z_ref, *, tf):
    m = m_ref[...].reshape(2 * tf, m_ref.shape[2])
    a = _dot(m, ve_ref[0, 0])
    b = _dot(m, vo_ref[0, 0])
    ar, ai, br, bi = a[:tf], a[tf:], b[:tf], b[tf:]
    sr, si, tr, ti, ur, ui = (r[...] for r in (sr_ref, si_ref, tr_ref, ti_ref, ur_ref, ui_ref))
    row0 = (pl.program_id(2) * tf + lax.broadcasted_iota(jnp.int32, sr.shape, 0)) == 0
    sr_n = jnp.where(row0, nyq_ref[0, 0:1, :], sr)
    tr_n = jnp.where(row0, nyq_ref[1, 0:1, :], tr)
    ur_n = jnp.where(row0, nyq_ref[2, 0:1, :], ur)
    z_ref[0, 0, 0] = (sr * ar - si * ai + tr * br - ti * bi).astype(z_ref.dtype)
    z_ref[0, 0, 1] = (sr_n * ai + si * ar + tr_n * bi + ti * br).astype(z_ref.dtype)
    z_ref[0, 1, 0] = (ur * ar - ui * ai + sr * br - si * bi).astype(z_ref.dtype)
    z_ref[0, 1, 1] = (ur_n * ai + ui * ar + sr_n * bi + si * br).astype(z_ref.dtype)


def hyena_dft_fwd(mmat, vv, coefs, nyq, tf=256, tc=1024):
    b, _, half, _ = vv.shape
    coef_spec = pl.BlockSpec((tf, tc), lambda bi, c, f: (f, c))
    return pl.pallas_call(
        functools.partial(_dft_fwd_kernel, tf=tf),
        grid=(b, C_WIDTH // tc, half // tf),
        in_specs=[pl.BlockSpec((2, tf, half), lambda bi, c, f: (0, f, 0)),
                  pl.BlockSpec((1, 1, half, tc), lambda bi, c, f: (bi, 0, 0, c)),
                  pl.BlockSpec((1, 1, half, tc), lambda bi, c, f: (bi, 1, 0, c))]
                 + [coef_spec] * 6
                 + [pl.BlockSpec((3, NYQ_ROWS, tc), lambda bi, c, f: (0, 0, c))],
        out_specs=pl.BlockSpec((1, 2, 2, tf, tc), lambda bi, c, f: (bi, 0, 0, f, c)),
        out_shape=jax.ShapeDtypeStruct((b, 2, 2, half, C_WIDTH), BF16),
        compiler_params=_cparams("parallel", "parallel", "parallel"),
        name="hyena_dft_fwd",
    )(mmat, vv, vv, *coefs, nyq)


def _dft_inv_kernel(g_ref, z_ref, vv_ref, x0_ref, skip_ref, u_ref):
    g = g_ref[...]
    for parity in range(2):
        y = _dot(g, z_ref[0, parity])
        vv = vv_ref[0, parity].astype(F32)
        x0 = x0_ref[0, parity].astype(F32)
        u_ref[0, parity] = ((y + vv * skip_ref[...]) * x0).astype(u_ref.dtype)


def hyena_dft_inv(gmat, zs, vv, x0, skip, tt=512, tc=512):
    b, _, half, _ = vv.shape
    io_spec = pl.BlockSpec((1, 2, tt, tc), lambda bi, c, t: (bi, 0, t, c))
    return pl.pallas_call(
        _dft_inv_kernel,
        grid=(b, C_WIDTH // tc, half // tt),
        in_specs=[
            pl.BlockSpec((tt, 2 * half), lambda bi, c, t: (t, 0)),
            pl.BlockSpec((1, 2, 2 * half, tc), lambda bi, c, t: (bi, 0, 0, c)),
            io_spec,
            io_spec,
            pl.BlockSpec((1, tc), lambda bi, c, t: (0, c)),
        ],
        out_specs=io_spec,
        out_shape=jax.ShapeDtypeStruct((b, 2, half, C_WIDTH), BF16),
        compiler_params=_cparams("parallel", "parallel", "parallel"),
        name="hyena_dft_inv",
    )(gmat, zs, vv, x0, skip.reshape(1, C_WIDTH))


def _dft_matrices(half):
    idx = jnp.arange(half, dtype=jnp.int32)
    ang = ((idx[:, None] * idx[None, :]) % (2 * half)).astype(F32) * (math.pi / half)
    sin_part = (-jnp.sin(ang)).at[0].set((1 - 2 * (idx % 2)).astype(F32))
    mmat = jnp.stack([jnp.cos(ang), sin_part]).astype(BF16)
    tw = idx.astype(F32) * (math.pi / half)
    return mmat, mmat.reshape(2 * half, half).T, jnp.stack([jnp.cos(tw), jnp.sin(tw)], axis=1)


def _filter_hidden(seq, w1, b1, freq, w2, b2):
    t = jnp.linspace(0.0, 1.0, seq, dtype=F32)[:, None]
    bands = (FILTER_EMB - 1) // 2
    w = 2.0 * math.pi * jnp.arange(seq, dtype=F32) / seq
    f = jnp.linspace(1e-4, bands - 1, bands, dtype=F32)
    ang = w[:, None] * f[None, :]
    feats = jnp.concatenate([t, jnp.cos(ang), -jnp.sin(ang)], axis=-1)
    hdn = jnp.sin(freq[0] * (feats @ w1 + b1))
    return jnp.sin(freq[1] * (hdn @ w2 + b2))


def hyena_mixer(h, batch, seq, g, w_in, conv_w, conv_b, w1, b1, freq, w2, b2, w3, skip, w_out):
    half = seq // 2
    z = norm_matmul(h, g, w_in, jnp.ones((3 * C_WIDTH,), F32), F32, split_seq=seq)
    x0, vv = hyena_pre(z.reshape(batch, 2, half, 3 * C_WIDTH), conv_w, conv_b)
    hdn = _filter_hidden(seq, w1, b1, freq, w2, b2)
    min_decay = math.log(FILTER_TARGET) / FAST_DECAY_PCT
    max_decay = math.log(FILTER_TARGET) / SLOW_DECAY_PCT
    deltas = jnp.abs(jnp.linspace(min_decay, max_decay, C_WIDTH, dtype=F32))
    taps = hyena_filter_taps(hdn, w3, deltas)
    mmat, gmat, twiddle = _dft_matrices(half)
    *coefs, nyq = hyena_filter_dft(mmat, twiddle, taps)
    zs = hyena_dft_fwd(mmat, vv, coefs, nyq).reshape(batch, 2, seq, C_WIDTH)
    u = hyena_dft_inv(gmat, zs, vv, x0, skip)
    return proj_residual([u.reshape(batch * seq, C_WIDTH)], [w_out], h, split_seq=seq)


def even_mixer(h, batch, seq, g, w_in, w_out, lam_vec, subln_g, na_table, rel_table, layer_idx):
    col_scale = np.ones((6, A_WIDTH), np.float32)
    col_scale[0] = A_QK_DIM ** -0.5 * LOG2E
    col_scale[3] = B_HEAD_DIM ** -0.5 * LOG2E
    z = norm_matmul(h, g, w_in, jnp.asarray(col_scale.reshape(-1)), BF16)
    z = z.reshape(batch, seq, w_in.shape[1])
    lam_init = 0.8 - 0.6 * math.exp(-0.3 * layer_idx)
    lv = lam_vec.astype(F32)
    lam = jnp.exp(jnp.sum(lv[0] * lv[1])) - jnp.exp(jnp.sum(lv[2] * lv[3])) + lam_init
    oa = diff_attention(z, lam, rel_table, subln_g, lam_init)
    ob = neighbourhood_attention(z, na_table)
    n = batch * seq
    return proj_residual([oa.reshape(n, A_WIDTH), ob.reshape(n, B_WIDTH)],
                         [w_out[:A_WIDTH], w_out[A_WIDTH:]], h)


def _trunk(x, p, W):
    batch, seq, d = x.shape
    depth = p.shape[0]
    h = x.reshape(batch * seq, d)
    for i in range(depth):
        j = i // 2
        if i % 2 == 0:
            h = even_mixer(h, batch, seq, W['norm_mix'][i], W['ab_w_in'][j], W['ab_w_out'][j],
                           W['diff_lambda'][j], W['diff_subln'][j], W['na_bias'][j],
                           W['rel_bias_table'], i)
        else:
            h = hyena_mixer(h, batch, seq, W['norm_mix'][i], W['c_w_in'][j], W['c_conv_w'][j],
                            W['c_conv_b'][j], W['c_filt_w1'][j], W['c_filt_b1'][j],
                            W['c_filt_freq'][j], W['c_filt_w2'][j], W['c_filt_b2'][j],
                            W['c_filt_w3'][j], W['c_skip'][j], W['c_w_out'][j])
        h = ffn(h, W['norm_ffn'][i], W['ffn_w_in'][i], W['ffn_w_out'][i])
        h = ple(h, p[i].reshape(batch * seq, -1), W['norm_ple'][i], W['ple_w_gate'][i],
                W['ple_w_proj'][i], W['final_norm'], final_norm=(i == depth - 1))
    return h.reshape(batch, seq, d)


def kernel(x_prompt, x_sample, p_prompt, p_sample, rel_bias_table, norm_mix, norm_ffn, norm_ple, final_norm, ab_w_in, ab_w_out, diff_lambda, diff_subln, na_bias, c_w_in, c_conv_w, c_conv_b, c_filt_w1, c_filt_b1, c_filt_freq, c_filt_w2, c_filt_b2, c_filt_w3, c_skip, c_w_out, ffn_w_in, ffn_w_out, ple_w_proj, ple_w_gate):
    bf = lambda w: w.astype(BF16)
    W = dict(rel_bias_table=rel_bias_table, norm_mix=norm_mix, norm_ffn=norm_ffn,
             norm_ple=norm_ple, final_norm=final_norm, ab_w_in=bf(ab_w_in), ab_w_out=bf(ab_w_out),
             diff_lambda=diff_lambda, diff_subln=diff_subln, na_bias=na_bias,
             c_w_in=bf(c_w_in), c_conv_w=c_conv_w, c_conv_b=c_conv_b, c_filt_w1=c_filt_w1,
             c_filt_b1=c_filt_b1, c_filt_freq=c_filt_freq, c_filt_w2=c_filt_w2,
             c_filt_b2=c_filt_b2, c_filt_w3=bf(c_filt_w3), c_skip=c_skip, c_w_out=bf(c_w_out),
             ffn_w_in=bf(ffn_w_in), ffn_w_out=bf(ffn_w_out), ple_w_proj=bf(ple_w_proj),
             ple_w_gate=bf(ple_w_gate))
    return (_trunk(x_prompt, p_prompt, W), _trunk(x_sample, p_sample, W))
```

```python
import functools
import math

import numpy as np
import jax
import jax.numpy as jnp
from jax import lax
from jax.experimental import pallas as pl
from jax.experimental.pallas import tpu as pltpu

F32 = jnp.float32
BF16 = jnp.bfloat16

D_MODEL = 2048
GRID_W = 64
A_HEADS = 8
A_QK_DIM = 64
A_V_DIM = 2 * A_QK_DIM
A_WIDTH = A_HEADS * A_V_DIM
T5_BUCKETS = 32
T5_MAX_DIST = 128
B_HEADS = 8
B_HEAD_DIM = 128
B_WIDTH = B_HEADS * B_HEAD_DIM
NA_ROWS = 8
NA_COLS = 16
C_WIDTH = D_MODEL
FILTER_EMB = 33
FILTER_HIDDEN = 64
FILTER_TARGET = 1e-2
FAST_DECAY_PCT = 0.3
SLOW_DECAY_PCT = 1.5
EPS = 1e-6

LANES = 128
NA_KEYS = NA_ROWS * GRID_W
MASK_VALUE = -1e30
LOG2E = math.log2(math.e)
SPLIT_CHUNK = 512
NA_UNROLL = 8
VMEM_LIMIT = 56 * 1024 * 1024


def _cparams(*sem):
    return pltpu.CompilerParams(dimension_semantics=sem, vmem_limit_bytes=VMEM_LIMIT)


def _rms(x, g):
    ms = jnp.mean(x * x, axis=-1, keepdims=True)
    return x * lax.rsqrt(ms + EPS) * g


def _dot(a, b):
    return jnp.dot(a, b, preferred_element_type=F32)


def _dot_nt(a, b):
    return lax.dot_general(a, b, (((1,), (1,)), ((), ())), preferred_element_type=F32)


def _norm_matmul_kernel(x_ref, g_ref, w_ref, cs_ref, o_ref, xn_ref):
    @pl.when(pl.program_id(1) == 0)
    def _():
        xn_ref[...] = _rms(x_ref[...], g_ref[...]).astype(BF16)

    o_ref[...] = (_dot(xn_ref[...], w_ref[...]) * cs_ref[...]).astype(o_ref.dtype)


def norm_matmul(x, g, w, col_scale, out_dtype, tm=1024, tn=1024):
    n, d = x.shape
    d_out = w.shape[1]
    return pl.pallas_call(
        _norm_matmul_kernel,
        grid=(n // tm, d_out // tn),
        in_specs=[
            pl.BlockSpec((tm, d), lambda i, j: (i, 0)),
            pl.BlockSpec((1, d), lambda i, j: (0, 0)),
            pl.BlockSpec((d, tn), lambda i, j: (0, j)),
            pl.BlockSpec((1, tn), lambda i, j: (0, j)),
        ],
        out_specs=pl.BlockSpec((tm, tn), lambda i, j: (i, j)),
        out_shape=jax.ShapeDtypeStruct((n, d_out), out_dtype),
        scratch_shapes=[pltpu.VMEM((tm, d), BF16)],
        compiler_params=_cparams("parallel", "arbitrary"),
        name="norm_matmul",
    )(x, g.reshape(1, d), w, col_scale.reshape(1, d_out))


def _parity_selectors(n):
    t = lax.broadcasted_iota(jnp.int32, (n, n // 2), 0)
    s = lax.broadcasted_iota(jnp.int32, (n, n // 2), 1)
    return (jnp.where(t == 2 * s, 1.0, 0.0).astype(BF16),
            jnp.where(t == 2 * s + 1, 1.0, 0.0).astype(BF16))


def _proj_res_kernel(*refs, n_in, interleave):
    a_refs = refs[:n_in]
    w_refs = refs[n_in:2 * n_in]
    res_ref, o_ref = refs[2 * n_in], refs[2 * n_in + 1]
    acc = res_ref[...]
    for a_ref, w_ref in zip(a_refs, w_refs):
        if interleave:
            sel_e, sel_o = _parity_selectors(2 * a_ref.shape[2])
            a = (_dot(sel_e, a_ref[0, 0]) + _dot(sel_o, a_ref[0, 1])).astype(BF16)
        else:
            a = a_ref[...]
        acc = acc + _dot(a, w_ref[...])
    o_ref[...] = acc


def proj_residual(a_list, w_list, res, interleave=False, tm=512):
    n, d = res.shape
    n_in = len(a_list)
    if interleave:
        in_specs = [pl.BlockSpec((1, 2, tm // 2, a.shape[-1]), lambda i: (i, 0, 0, 0))
                    for a in a_list]
    else:
        in_specs = [pl.BlockSpec((tm, a.shape[1]), lambda i: (i, 0)) for a in a_list]
    in_specs += [pl.BlockSpec(w.shape, lambda i: (0, 0)) for w in w_list]
    in_specs += [pl.BlockSpec((tm, d), lambda i: (i, 0))]
    return pl.pallas_call(
        functools.partial(_proj_res_kernel, n_in=n_in, interleave=interleave),
        grid=(n // tm,),
        in_specs=in_specs,
        out_specs=pl.BlockSpec((tm, d), lambda i: (i, 0)),
        out_shape=jax.ShapeDtypeStruct((n, d), F32),
        compiler_params=_cparams("parallel"),
        name="proj_residual",
    )(*a_list, *w_list, res)


def _ffn_kernel(x_ref, g_ref, wg_ref, wu_ref, wo_ref, o_ref, xn_ref):
    @pl.when(pl.program_id(1) == 0)
    def _():
        x = x_ref[...]
        xn_ref[...] = _rms(x, g_ref[...]).astype(BF16)
        o_ref[...] = x

    xn = xn_ref[...]
    gate = _dot(xn, wg_ref[...])
    up = _dot(xn, wu_ref[...])
    act = (gate * jax.nn.sigmoid(gate) * up).astype(BF16)
    o_ref[...] += _dot(act, wo_ref[...])


def ffn(x, g, w_in, w_out, tm=1024, tf=512):
    n, d = x.shape
    d_ff = w_out.shape[0]
    nf = d_ff // tf
    return pl.pallas_call(
        _ffn_kernel,
        grid=(n // tm, nf),
        in_specs=[
            pl.BlockSpec((tm, d), lambda i, f: (i, 0)),
            pl.BlockSpec((1, d), lambda i, f: (0, 0)),
            pl.BlockSpec((d, tf), lambda i, f: (0, f)),
            pl.BlockSpec((d, tf), lambda i, f: (0, nf + f)),
            pl.BlockSpec((tf, d), lambda i, f: (f, 0)),
        ],
        out_specs=pl.BlockSpec((tm, d), lambda i, f: (i, 0)),
        out_shape=jax.ShapeDtypeStruct((n, d), F32),
        scratch_shapes=[pltpu.VMEM((tm, d), BF16)],
        compiler_params=_cparams("parallel", "arbitrary"),
        name="ffn",
    )(x, g.reshape(1, d), w_in, w_in, w_out)


def _ple_kernel(x_ref, p_ref, g_ref, wg_ref, wp_ref, gf_ref, o_ref, *, final_norm):
    x = x_ref[...]
    gate = jax.nn.sigmoid(_dot(_rms(x, g_ref[...]).astype(BF16), wg_ref[...]))
    proj = _dot(p_ref[...].astype(BF16), wp_ref[...])
    y = x + proj * gate
    if final_norm:
        y = _rms(y, gf_ref[...])
    o_ref[...] = y


def ple(x, p, g, w_gate, w_proj, g_final, final_norm, tm=512):
    n, d = x.shape
    dp = p.shape[1]
    return pl.pallas_call(
        functools.partial(_ple_kernel, final_norm=final_norm),
        grid=(n // tm,),
        in_specs=[
            pl.BlockSpec((tm, d), lambda i: (i, 0)),
            pl.BlockSpec((tm, dp), lambda i: (i, 0)),
            pl.BlockSpec((1, d), lambda i: (0, 0)),
            pl.BlockSpec((d, d), lambda i: (0, 0)),
            pl.BlockSpec((dp, d), lambda i: (0, 0)),
            pl.BlockSpec((1, d), lambda i: (0, 0)),
        ],
        out_specs=pl.BlockSpec((tm, d), lambda i: (i, 0)),
        out_shape=jax.ShapeDtypeStruct((n, d), F32),
        compiler_params=_cparams("parallel"),
        name="ple",
    )(x, p, g.reshape(1, d), w_gate, w_proj, g_final.reshape(1, d))


def t5_bucket(rel):
    nb = T5_BUCKETS // 2
    max_exact = nb // 2
    ret = jnp.where(rel > 0, nb, 0)
    n = jnp.abs(rel)
    nf = jnp.maximum(n, 1).astype(jnp.float32)
    large = max_exact + (jnp.log(nf / max_exact) / math.log(T5_MAX_DIST / max_exact)
                         * (nb - max_exact)).astype(jnp.int32)
    large = jnp.minimum(large, nb - 1)
    return ret + jnp.where(n < max_exact, n, large)


def _lane_tiles(x):
    return [x[:, j * LANES:(j + 1) * LANES] for j in range(x.shape[1] // LANES)]


def _diff_attn_kernel(lam_ref, q_ref, k_ref, v_ref, e_ref, g_ref, o_ref,
                      s1_ref, s2_ref, p1_ref, p2_ref, va_ref, *, tq, tk, seq, pad, width,
                      out_scale):
    i = pl.program_id(2)

    @pl.when(i == 0)
    def _():
        va_ref[:, :A_V_DIM] = v_ref[0]
        va_ref[:, A_V_DIM:] = jnp.ones((seq, A_V_DIM), BF16)

    q = q_ref[0]
    lane = lax.broadcasted_iota(jnp.int32, q.shape, 1)
    zero = jnp.zeros_like(q)
    q1 = jnp.where(lane < A_QK_DIM, q, zero)
    q2 = jnp.where(lane < A_QK_DIM, zero, q)

    def bias_tile(j):
        c = j * LANES // tk
        start = jnp.clip(c * tk - i * tq + pad, 0, width - tk) + (j * LANES - c * tk)
        return e_ref[0, :, pl.ds(pl.multiple_of(start, LANES), LANES)]

    m1 = m2 = None
    for c in range(seq // tk):
        cols = slice(c * tk, (c + 1) * tk)
        kc = k_ref[0, cols, :]
        x1 = _dot_nt(q1, kc)
        x2 = _dot_nt(q2, kc)
        s1_ref[:, cols] = x1
        s2_ref[:, cols] = x2
        for jj, (t1, t2) in enumerate(zip(_lane_tiles(x1), _lane_tiles(x2))):
            bias = bias_tile(c * (tk // LANES) + jj)
            m1 = t1 + bias if m1 is None else jnp.maximum(m1, t1 + bias)
            m2 = t2 + bias if m2 is None else jnp.maximum(m2, t2 + bias)
    m1 = jnp.broadcast_to(jnp.max(m1, axis=-1, keepdims=True), (tq, LANES))
    m2 = jnp.broadcast_to(jnp.max(m2, axis=-1, keepdims=True), (tq, LANES))
    for j in range(seq // LANES):
        cols = slice(j * LANES, (j + 1) * LANES)
        shift = bias_tile(j)
        p1_ref[:, cols] = jnp.exp2(s1_ref[:, cols] + (shift - m1)).astype(BF16)
        p2_ref[:, cols] = jnp.exp2(s2_ref[:, cols] + (shift - m2)).astype(BF16)
    o1 = _dot(p1_ref[...], va_ref[...])
    o2 = _dot(p2_ref[...], va_ref[...])
    o = (o1[:, :A_V_DIM] * (1.0 / o1[:, A_V_DIM:])
         - o2[:, :A_V_DIM] * (lam_ref[0] / o2[:, A_V_DIM:]))
    o = _rms(o, g_ref[...]) * out_scale
    o_ref[0] = o.astype(o_ref.dtype)


def _band_kernel(w_ref, o_ref, *, rows, cols):
    n = rows + cols
    x = jnp.broadcast_to(w_ref[0], (rows, n))
    y = pltpu.roll(x, n - (rows - 1), 1, stride=1, stride_axis=0)
    o_ref[0] = y[:, :cols]


def _toeplitz(w, rows, cols):
    heads, n = w.shape
    assert n == rows + cols and n % LANES == 0
    return pl.pallas_call(
        functools.partial(_band_kernel, rows=rows, cols=cols),
        grid=(heads,),
        in_specs=[pl.BlockSpec((1, 1, n), lambda h: (h, 0, 0))],
        out_specs=pl.BlockSpec((1, rows, cols), lambda h: (h, 0, 0)),
        out_shape=jax.ShapeDtypeStruct((heads, rows, cols), F32),
        compiler_params=_cparams("parallel"),
        name="toeplitz_band",
    )(w.reshape(heads, 1, n))


def diff_attention(z, lam, rel_table, subln_g, lam_init, tq=512, tk=512):
    b, seq, _ = z.shape
    pad = tk + T5_MAX_DIST
    width = 2 * tk + tq + 2 * T5_MAX_DIST
    rel = jnp.arange(tq + width, dtype=jnp.int32) - (tq - 1) - pad
    onehot = (t5_bucket(rel)[:, None] == jnp.arange(T5_BUCKETS)[None, :]).astype(F32)
    diag = jnp.dot(onehot, rel_table.astype(F32), precision=lax.Precision.HIGHEST)
    band = _toeplitz(diag.T * LOG2E, tq, width)
    hq = A_WIDTH // LANES
    return pl.pallas_call(
        functools.partial(_diff_attn_kernel, tq=tq, tk=tk, seq=seq, pad=pad, width=width,
                          out_scale=1.0 - lam_init),
        grid=(b, A_HEADS, seq // tq),
        in_specs=[
            pl.BlockSpec(memory_space=pltpu.SMEM),
            pl.BlockSpec((1, tq, LANES), lambda bi, h, i: (bi, i, h)),
            pl.BlockSpec((1, seq, LANES), lambda bi, h, i: (bi, 0, hq + h)),
            pl.BlockSpec((1, seq, LANES), lambda bi, h, i: (bi, 0, 2 * hq + h)),
            pl.BlockSpec((1, tq, width), lambda bi, h, i: (h, 0, 0)),
            pl.BlockSpec((1, A_V_DIM), lambda bi, h, i: (0, 0)),
        ],
        out_specs=pl.BlockSpec((1, tq, LANES), lambda bi, h, i: (bi, i, h)),
        out_shape=jax.ShapeDtypeStruct((b, seq, A_WIDTH), BF16),
        scratch_shapes=[pltpu.VMEM((tq, seq), F32), pltpu.VMEM((tq, seq), F32),
                        pltpu.VMEM((tq, seq), BF16), pltpu.VMEM((tq, seq), BF16),
                        pltpu.VMEM((seq, 2 * A_V_DIM), BF16)],
        compiler_params=_cparams("parallel", "parallel", "arbitrary"),
        name="diff_attention",
    )(lam.reshape(1).astype(F32), z, z, z, band, subln_g.reshape(1, A_V_DIM))


def _na_kernel(q_ref, k_ref, v_ref, nb_ref, o_ref, *, rows):
    def row_group(gi, carry):
        geom = []
        for u in range(NA_UNROLL):
            r = gi * NA_UNROLL + u
            r_start = jnp.clip(r - NA_ROWS // 2, 0, rows - NA_ROWS)
            geom.append((pl.multiple_of(r * GRID_W, GRID_W),
                         pl.multiple_of(r_start * GRID_W, GRID_W), r - r_start))
        scores = [_dot_nt(q_ref[0, pl.ds(q0, GRID_W), :], k_ref[0, pl.ds(k0, NA_KEYS), :])
                  + nb_ref[0, var] for q0, k0, var in geom]
        probs = []
        for s in scores:
            p = jnp.exp2(s - jnp.max(s, axis=-1, keepdims=True))
            probs.append((p * (1.0 / jnp.sum(p, axis=-1, keepdims=True))).astype(BF16))
        for p, (q0, k0, _) in zip(probs, geom):
            o = _dot(p, v_ref[0, pl.ds(k0, NA_KEYS), :])
            o_ref[0, pl.ds(q0, GRID_W), :] = o.astype(o_ref.dtype)
        return carry

    lax.fori_loop(0, rows // NA_UNROLL, row_group, 0)


def _na_bias(na_table):
    var = np.arange(NA_ROWS)[:, None, None]
    kr = np.arange(NA_ROWS)[None, :, None]
    row_sel = (kr - var + (NA_ROWS - 1) == np.arange(2 * NA_ROWS - 1)).astype(np.float32)
    c = np.arange(GRID_W)[:, None, None]
    kc = np.arange(GRID_W)[None, :, None]
    col_sel = (kc - c + (NA_COLS - 1) == np.arange(2 * NA_COLS - 1)).astype(np.float32)
    c_start = np.clip(c - NA_COLS // 2, 0, GRID_W - NA_COLS)
    valid = ((kc >= c_start) & (kc < c_start + NA_COLS))[None, None, :, None, :, 0]
    bias = jnp.einsum('vka,hab,cqb->hvckq', row_sel, na_table.astype(F32), col_sel,
                      precision=lax.Precision.HIGHEST)
    bias = jnp.where(valid, bias * LOG2E, MASK_VALUE)
    return bias.reshape(B_HEADS, NA_ROWS, GRID_W, NA_KEYS)


def neighbourhood_attention(z, na_table):
    b, seq, _ = z.shape
    rows = seq // GRID_W
    assert rows >= NA_ROWS and rows % NA_UNROLL == 0
    hq = B_WIDTH // LANES
    base = 3 * A_WIDTH // LANES
    return pl.pallas_call(
        functools.partial(_na_kernel, rows=rows),
        grid=(b, B_HEADS),
        in_specs=[
            pl.BlockSpec((1, seq, LANES), lambda bi, h: (bi, 0, base + h)),
            pl.BlockSpec((1, seq, LANES), lambda bi, h: (bi, 0, base + hq + h)),
            pl.BlockSpec((1, seq, LANES), lambda bi, h: (bi, 0, base + 2 * hq + h)),
            pl.BlockSpec((1, NA_ROWS, GRID_W, NA_KEYS), lambda bi, h: (h, 0, 0, 0)),
        ],
        out_specs=pl.BlockSpec((1, seq, LANES), lambda bi, h: (bi, 0, h)),
        out_shape=jax.ShapeDtypeStruct((b, seq, B_WIDTH), BF16),
        compiler_params=_cparams("parallel", "parallel"),
        name="neighbourhood_attention",
    )(z, z, z, _na_bias(na_table))


def _hyena_pre_kernel(x0_ref, x1_ref, v_ref, w0_ref, w1_ref, wv_ref, b0_ref, b1_ref, bv_ref,
                      x0_out, vv_out, *, seq):
    row = lax.broadcasted_iota(jnp.int32, x0_ref.shape[1:], 0)

    def conv(z_ref, w_ref, b_ref):
        z = z_ref[0].astype(F32)
        prev = jnp.where(row == 0, 0.0, pltpu.roll(z, 1, 0))
        nxt = jnp.where(row == seq - 1, 0.0, pltpu.roll(z, seq - 1, 0))
        return prev * w_ref[0:1, :] + z * w_ref[1:2, :] + nxt * w_ref[2:3, :] + b_ref[...]

    x0 = conv(x0_ref, w0_ref, b0_ref).astype(BF16)
    vv = (conv(v_ref, wv_ref, bv_ref) * conv(x1_ref, w1_ref, b1_ref)).astype(BF16)
    both = jnp.concatenate([x0, vv], axis=1)
    tc = x0.shape[1]
    sel_e, sel_o = _parity_selectors(SPLIT_CHUNK)
    for k in range(seq // SPLIT_CHUNK):
        chunk = both[k * SPLIT_CHUNK:(k + 1) * SPLIT_CHUNK]
        for p, sel in enumerate((sel_e, sel_o)):
            rows = lax.dot_general(sel, chunk, (((0,), (0,)), ((), ())),
                                   preferred_element_type=F32).astype(BF16)
            x0_out[0, k, p] = rows[:, :tc]
            vv_out[0, k, p] = rows[:, tc:]


def hyena_pre(z, conv_w, conv_b, tc=128):
    b, seq, _ = z.shape
    nc = C_WIDTH // tc
    nck = seq // SPLIT_CHUNK
    zspec = lambda o: pl.BlockSpec((1, seq, tc), lambda bi, c: (bi, 0, o * nc + c))
    wspec = lambda o: pl.BlockSpec((3, tc), lambda bi, c: (0, o * nc + c))
    bspec = lambda o: pl.BlockSpec((1, tc), lambda bi, c: (0, o * nc + c))
    cb = conv_b.reshape(1, -1)
    out = jax.ShapeDtypeStruct((b, nck, 2, SPLIT_CHUNK // 2, C_WIDTH), BF16)
    ospec = pl.BlockSpec((1, nck, 2, SPLIT_CHUNK // 2, tc), lambda bi, c: (bi, 0, 0, 0, c))
    return pl.pallas_call(
        functools.partial(_hyena_pre_kernel, seq=seq),
        grid=(b, nc),
        in_specs=[zspec(0), zspec(1), zspec(2), wspec(0), wspec(1), wspec(2),
                  bspec(0), bspec(1), bspec(2)],
        out_specs=[ospec, ospec],
        out_shape=[out, out],
        compiler_params=_cparams("parallel", "parallel"),
        name="hyena_pre",
    )(z, z, z, conv_w, conv_w, conv_w, cb, cb, cb)


def _filter_kernel(hdn_ref, wf_ref, wb_ref, dl_ref, ae_ref, be_ref, fo_ref, bo_ref, *, tl, seq):
    hd = hdn_ref[...].astype(BF16)
    lane = lax.broadcasted_iota(jnp.int32, hd.shape, 1)
    zero = jnp.zeros_like(hd)
    hd_e = jnp.where(lane < FILTER_HIDDEN, hd, zero)
    hd_o = jnp.where(lane < FILTER_HIDDEN, zero, hd)
    j = pl.program_id(0) * tl + lax.broadcasted_iota(jnp.int32, ae_ref.shape, 0)
    dl = dl_ref[...]
    decay_e = jnp.exp((2 * j).astype(F32) * (-1.0 / (seq - 1)) * dl)
    decay_o = jnp.exp((2 * j + 1).astype(F32) * (-1.0 / (seq - 1)) * dl)
    wf, wb = wf_ref[...], wb_ref[...]
    hf_e = _dot(hd_e, wf) * decay_e
    hb_e = jnp.where(j == 0, 0.0, _dot(hd_e, wb) * decay_e)
    ae_ref[...] = (hf_e + hb_e).astype(ae_ref.dtype)
    be_ref[...] = (hf_e - hb_e).astype(be_ref.dtype)
    fo_ref[...] = (_dot(hd_o, wf) * decay_o).astype(fo_ref.dtype)
    bo_ref[...] = (_dot(hd_o, wb) * decay_o).astype(bo_ref.dtype)


def hyena_filter_taps(hdn, w3, deltas, tl=512, tc=512):
    seq, hidden = hdn.shape
    half = seq // 2
    nc = C_WIDTH // tc
    w3s = jnp.concatenate([w3, w3], axis=0)
    out = jax.ShapeDtypeStruct((half, C_WIDTH), BF16)
    return pl.pallas_call(
        functools.partial(_filter_kernel, tl=tl, seq=seq),
        grid=(half // tl, nc),
        in_specs=[
            pl.BlockSpec((tl, 2 * hidden), lambda l, c: (l, 0)),
            pl.BlockSpec((2 * hidden, tc), lambda l, c: (0, c)),
            pl.BlockSpec((2 * hidden, tc), lambda l, c: (0, nc + c)),
            pl.BlockSpec((1, tc), lambda l, c: (0, c)),
        ],
        out_specs=[pl.BlockSpec((tl, tc), lambda l, c: (l, c))] * 4,
        out_shape=[out] * 4,
        compiler_params=_cparams("parallel", "parallel"),
        name="hyena_filter_taps",
    )(hdn.reshape(half, 2 * hidden), w3s, w3s, deltas.reshape(1, C_WIDTH))


NYQ_ROWS = 16


def _filter_dft_kernel(m_ref, tw_ref, ae_ref, be_ref, fo_ref, bo_ref,
                       sr_ref, si_ref, tr_ref, ti_ref, ur_ref, ui_ref, nyq_ref, *, tf, seq):
    fi = pl.program_id(1)
    cos_m, sin_m = m_ref[0], m_ref[1]
    fo, bo = fo_ref[...], bo_ref[...]
    row0 = (fi * tf + lax.broadcasted_iota(jnp.int32, sr_ref.shape, 0)) == 0
    e_r = _dot(cos_m, ae_ref[...])
    e_i = jnp.where(row0, 0.0, _dot(sin_m, be_ref[...]))
    nyq_f, nyq_b = _dot(sin_m, fo), _dot(sin_m, bo)
    f_r, f_i = _dot(cos_m, fo), jnp.where(row0, 0.0, nyq_f)
    b_r, b_i = _dot(cos_m, bo), jnp.where(row0, 0.0, -nyq_b)
    cw, sw = tw_ref[:, 0:1], tw_ref[:, 1:2]
    wt = jnp.where(row0, 1.0 / seq, 2.0 / seq)
    sr_ref[...] = e_r * wt
    si_ref[...] = e_i * wt
    tr_ref[...] = (cw * f_r + sw * f_i + b_r) * wt
    ti_ref[...] = (cw * f_i - sw * f_r + b_i) * wt
    ur_ref[...] = (f_r + cw * b_r - sw * b_i) * wt
    ui_ref[...] = (f_i + cw * b_i + sw * b_r) * wt

    @pl.when(fi == 0)
    def _():
        e_n = _dot(sin_m[0:NYQ_ROWS, :], ae_ref[...])
        f_n, b_n = nyq_f[0:NYQ_ROWS, :], nyq_b[0:NYQ_ROWS, :]
        nyq_ref[0] = e_n * (1.0 / seq)
        nyq_ref[1] = (b_n - f_n) * (1.0 / seq)
        nyq_ref[2] = (f_n - b_n) * (1.0 / seq)


def hyena_filter_dft(mmat, twiddle, taps, tf=256, tc=512):
    half = mmat.shape[1]
    coef = jax.ShapeDtypeStruct((half, C_WIDTH), F32)
    tap_spec = pl.BlockSpec((half, tc), lambda c, f: (0, c))
    coef_spec = pl.BlockSpec((tf, tc), lambda c, f: (f, c))
    return pl.pallas_call(
        functools.partial(_filter_dft_kernel, tf=tf, seq=2 * half),
        grid=(C_WIDTH // tc, half // tf),
        in_specs=[pl.BlockSpec((2, tf, half), lambda c, f: (0, f, 0)),
                  pl.BlockSpec((tf, 2), lambda c, f: (f, 0))] + [tap_spec] * 4,
        out_specs=[coef_spec] * 6 + [pl.BlockSpec((3, NYQ_ROWS, tc), lambda c, f: (0, 0, c))],
        out_shape=[coef] * 6 + [jax.ShapeDtypeStruct((3, NYQ_ROWS, C_WIDTH), F32)],
        compiler_params=_cparams("parallel", "arbitrary"),
        name="hyena_filter_dft",
    )(mmat, twiddle, *taps)


def _dft_fwd_kernel(m_ref, ve_ref, vo_ref, sr_ref, si_ref, tr_ref, ti_ref, ur_ref, ui_ref,
                    nyq_ref, z_ref, *, tf):
    m = m_ref[...].reshape(2 * tf, m_ref.shape[2])
    half, tc = m_ref.shape[2], ve_ref.shape[-1]
    a = _dot(m, ve_ref[0, :, 0].reshape(half, tc))
    b = _dot(m, vo_ref[0, :, 0].reshape(half, tc))
    ar, ai, br, bi = a[:tf], a[tf:], b[:tf], b[tf:]
    sr, si, tr, ti, ur, ui = (r[...] for r in (sr_ref, si_ref, tr_ref, ti_ref, ur_ref, ui_ref))
    row0 = (pl.program_id(2) * tf + lax.broadcasted_iota(jnp.int32, sr.shape, 0)) == 0
    sr_n = jnp.where(row0, nyq_ref[0, 0:1, :], sr)
    tr_n = jnp.where(row0, nyq_ref[1, 0:1, :], tr)
    ur_n = jnp.where(row0, nyq_ref[2, 0:1, :], ur)
    z_ref[0, 0, 0] = (sr * ar - si * ai + tr * br - ti * bi).astype(z_ref.dtype)
    z_ref[0, 0, 1] = (sr_n * ai + si * ar + tr_n * bi + ti * br).astype(z_ref.dtype)
    z_ref[0, 1, 0] = (ur * ar - ui * ai + sr * br - si * bi).astype(z_ref.dtype)
    z_ref[0, 1, 1] = (ur_n * ai + ui * ar + sr_n * bi + si * br).astype(z_ref.dtype)


def hyena_dft_fwd(mmat, vv, coefs, nyq, tf=256, tc=1024):
    b, nck, _, hck, _ = vv.shape
    half = nck * hck
    coef_spec = pl.BlockSpec((tf, tc), lambda bi, c, f: (f, c))
    return pl.pallas_call(
        functools.partial(_dft_fwd_kernel, tf=tf),
        grid=(b, C_WIDTH // tc, half // tf),
        in_specs=[pl.BlockSpec((2, tf, half), lambda bi, c, f: (0, f, 0)),
                  pl.BlockSpec((1, nck, 1, hck, tc), lambda bi, c, f: (bi, 0, 0, 0, c)),
                  pl.BlockSpec((1, nck, 1, hck, tc), lambda bi, c, f: (bi, 0, 1, 0, c))]
                 + [coef_spec] * 6
                 + [pl.BlockSpec((3, NYQ_ROWS, tc), lambda bi, c, f: (0, 0, c))],
        out_specs=pl.BlockSpec((1, 2, 2, tf, tc), lambda bi, c, f: (bi, 0, 0, f, c)),
        out_shape=jax.ShapeDtypeStruct((b, 2, 2, half, C_WIDTH), BF16),
        compiler_params=_cparams("parallel", "parallel", "parallel"),
        name="hyena_dft_fwd",
    )(mmat, vv, vv, *coefs, nyq)


def _dft_inv_kernel(g_ref, z_ref, vv_ref, x0_ref, skip_ref, u_ref):
    g = g_ref[...]
    nk, _, hck, tc = u_ref.shape[1:]
    for parity in range(2):
        y = _dot(g, z_ref[0, parity])
        vv = vv_ref[0, :, parity].reshape(nk * hck, tc).astype(F32)
        x0 = x0_ref[0, :, parity].reshape(nk * hck, tc).astype(F32)
        u = ((y + vv * skip_ref[...]) * x0).astype(u_ref.dtype)
        u_ref[0, :, parity] = u.reshape(nk, hck, tc)


def hyena_dft_inv(gmat, zs, vv, x0, skip, tt=512, tc=512):
    b, nck, _, hck, _ = vv.shape
    half = nck * hck
    io_spec = pl.BlockSpec((1, tt // hck, 2, hck, tc), lambda bi, c, t: (bi, t, 0, 0, c))
    return pl.pallas_call(
        _dft_inv_kernel,
        grid=(b, C_WIDTH // tc, half // tt),
        in_specs=[
            pl.BlockSpec((tt, 2 * half), lambda bi, c, t: (t, 0)),
            pl.BlockSpec((1, 2, 2 * half, tc), lambda bi, c, t: (bi, 0, 0, c)),
            io_spec,
            io_spec,
            pl.BlockSpec((1, tc), lambda bi, c, t: (0, c)),
        ],
        out_specs=io_spec,
        out_shape=jax.ShapeDtypeStruct(vv.shape, BF16),
        compiler_params=_cparams("parallel", "parallel", "parallel"),
        name="hyena_dft_inv",
    )(gmat, zs, vv, x0, skip.reshape(1, C_WIDTH))


def _dft_matrices(half):
    idx = jnp.arange(half, dtype=jnp.int32)
    ang = ((idx[:, None] * idx[None, :]) % (2 * half)).astype(F32) * (math.pi / half)
    sin_part = (-jnp.sin(ang)).at[0].set((1 - 2 * (idx % 2)).astype(F32))
    mmat = jnp.stack([jnp.cos(ang), sin_part]).astype(BF16)
    tw = idx.astype(F32) * (math.pi / half)
    return mmat, mmat.reshape(2 * half, half).T, jnp.stack([jnp.cos(tw), jnp.sin(tw)], axis=1)


def _filter_hidden(seq, w1, b1, freq, w2, b2):
    t = jnp.linspace(0.0, 1.0, seq, dtype=F32)[:, None]
    bands = (FILTER_EMB - 1) // 2
    w = 2.0 * math.pi * jnp.arange(seq, dtype=F32) / seq
    f = jnp.linspace(1e-4, bands - 1, bands, dtype=F32)
    ang = w[:, None] * f[None, :]
    feats = jnp.concatenate([t, jnp.cos(ang), -jnp.sin(ang)], axis=-1)
    hdn = jnp.sin(freq[0] * (feats @ w1 + b1))
    return jnp.sin(freq[1] * (hdn @ w2 + b2))


def hyena_mixer(h, batch, seq, g, w_in, conv_w, conv_b, w1, b1, freq, w2, b2, w3, skip, w_out):
    half = seq // 2
    z = norm_matmul(h, g, w_in, jnp.ones((3 * C_WIDTH,), F32), BF16)
    x0, vv = hyena_pre(z.reshape(batch, seq, 3 * C_WIDTH), conv_w, conv_b)
    hdn = _filter_hidden(seq, w1, b1, freq, w2, b2)
    min_decay = math.log(FILTER_TARGET) / FAST_DECAY_PCT
    max_decay = math.log(FILTER_TARGET) / SLOW_DECAY_PCT
    deltas = jnp.abs(jnp.linspace(min_decay, max_decay, C_WIDTH, dtype=F32))
    taps = hyena_filter_taps(hdn, w3, deltas)
    mmat, gmat, twiddle = _dft_matrices(half)
    *coefs, nyq = hyena_filter_dft(mmat, twiddle, taps)
    zs = hyena_dft_fwd(mmat, vv, coefs, nyq).reshape(batch, 2, seq, C_WIDTH)
    u = hyena_dft_inv(gmat, zs, vv, x0, skip)
    u = u.reshape(batch * seq // SPLIT_CHUNK, 2, SPLIT_CHUNK // 2, C_WIDTH)
    return proj_residual([u], [w_out], h, interleave=True, tm=SPLIT_CHUNK)


def even_mixer(h, batch, seq, g, w_in, w_out, lam_vec, subln_g, na_table, rel_table, layer_idx):
    col_scale = np.ones((6, A_WIDTH), np.float32)
    col_scale[0] = A_QK_DIM ** -0.5 * LOG2E
    col_scale[3] = B_HEAD_DIM ** -0.5 * LOG2E
    z = norm_matmul(h, g, w_in, jnp.asarray(col_scale.reshape(-1)), BF16)
    z = z.reshape(batch, seq, w_in.shape[1])
    lam_init = 0.8 - 0.6 * math.exp(-0.3 * layer_idx)
    lv = lam_vec.astype(F32)
    lam = jnp.exp(jnp.sum(lv[0] * lv[1])) - jnp.exp(jnp.sum(lv[2] * lv[3])) + lam_init
    oa = diff_attention(z, lam, rel_table, subln_g, lam_init)
    ob = neighbourhood_attention(z, na_table)
    n = batch * seq
    return proj_residual([oa.reshape(n, A_WIDTH), ob.reshape(n, B_WIDTH)],
                         [w_out[:A_WIDTH], w_out[A_WIDTH:]], h)


def _trunk(x, p, W):
    batch, seq, d = x.shape
    depth = p.shape[0]
    h = x.reshape(batch * seq, d)
    for i in range(depth):
        j = i // 2
        if i % 2 == 0:
            h = even_mixer(h, batch, seq, W['norm_mix'][i], W['ab_w_in'][j], W['ab_w_out'][j],
                           W['diff_lambda'][j], W['diff_subln'][j], W['na_bias'][j],
                           W['rel_bias_table'], i)
        else:
            h = hyena_mixer(h, batch, seq, W['norm_mix'][i], W['c_w_in'][j], W['c_conv_w'][j],
                            W['c_conv_b'][j], W['c_filt_w1'][j], W['c_filt_b1'][j],
                            W['c_filt_freq'][j], W['c_filt_w2'][j], W['c_filt_b2'][j],
                            W['c_filt_w3'][j], W['c_skip'][j], W['c_w_out'][j])
        h = ffn(h, W['norm_ffn'][i], W['ffn_w_in'][i], W['ffn_w_out'][i])
        h = ple(h, p[i].reshape(batch * seq, -1), W['norm_ple'][i], W['ple_w_gate'][i],
                W['ple_w_proj'][i], W['final_norm'], final_norm=(i == depth - 1))
    return h.reshape(batch, seq, d)


def kernel(x_prompt, x_sample, p_prompt, p_sample, rel_bias_table, norm_mix, norm_ffn, norm_ple, final_norm, ab_w_in, ab_w_out, diff_lambda, diff_subln, na_bias, c_w_in, c_conv_w, c_conv_b, c_filt_w1, c_filt_b1, c_filt_freq, c_filt_w2, c_filt_b2, c_filt_w3, c_skip, c_w_out, ffn_w_in, ffn_w_out, ple_w_proj, ple_w_gate):
    bf = lambda w: w.astype(BF16)
    W = dict(rel_bias_table=rel_bias_table, norm_mix=norm_mix, norm_ffn=norm_ffn,
             norm_ple=norm_ple, final_norm=final_norm, ab_w_in=bf(ab_w_in), ab_w_out=bf(ab_w_out),
             diff_lambda=diff_lambda, diff_subln=diff_subln, na_bias=na_bias,
             c_w_in=bf(c_w_in), c_conv_w=c_conv_w, c_conv_b=c_conv_b, c_filt_w1=c_filt_w1,
             c_filt_b1=c_filt_b1, c_filt_freq=c_filt_freq, c_filt_w2=c_filt_w2,
             c_filt_b2=c_filt_b2, c_filt_w3=bf(c_filt_w3), c_skip=c_skip, c_w_out=bf(c_w_out),
             ffn_w_in=bf(ffn_w_in), ffn_w_out=bf(ffn_w_out), ple_w_proj=bf(ple_w_proj),
             ple_w_gate=bf(ple_w_gate))
    return (_trunk(x_prompt, p_prompt, W), _trunk(x_sample, p_sample, W))
```

```python
import functools
import math

import numpy as np
import jax
import jax.numpy as jnp
from jax import lax
from jax.experimental import pallas as pl
from jax.experimental.pallas import tpu as pltpu

F32 = jnp.float32
BF16 = jnp.bfloat16

D_MODEL = 2048
GRID_W = 64
A_HEADS = 8
A_QK_DIM = 64
A_V_DIM = 2 * A_QK_DIM
A_WIDTH = A_HEADS * A_V_DIM
T5_BUCKETS = 32
T5_MAX_DIST = 128
B_HEADS = 8
B_HEAD_DIM = 128
B_WIDTH = B_HEADS * B_HEAD_DIM
NA_ROWS = 8
NA_COLS = 16
C_WIDTH = D_MODEL
FILTER_EMB = 33
FILTER_HIDDEN = 64
FILTER_TARGET = 1e-2
FAST_DECAY_PCT = 0.3
SLOW_DECAY_PCT = 1.5
EPS = 1e-6

LANES = 128
NA_KEYS = NA_ROWS * GRID_W
MASK_VALUE = -1e30
LOG2E = math.log2(math.e)
SPLIT_CHUNK = 512
NA_UNROLL = 8
VMEM_LIMIT = 56 * 1024 * 1024


def _cparams(*sem):
    return pltpu.CompilerParams(dimension_semantics=sem, vmem_limit_bytes=VMEM_LIMIT)


def _rms(x, g):
    ms = jnp.mean(x * x, axis=-1, keepdims=True)
    return x * lax.rsqrt(ms + EPS) * g


def _dot(a, b):
    return jnp.dot(a, b, preferred_element_type=F32)


def _dot_nt(a, b):
    return lax.dot_general(a, b, (((1,), (1,)), ((), ())), preferred_element_type=F32)


def _norm_matmul_kernel(x_ref, g_ref, w_ref, cs_ref, o_ref, xn_ref):
    @pl.when(pl.program_id(1) == 0)
    def _():
        xn_ref[...] = _rms(x_ref[...], g_ref[...]).astype(BF16)

    o_ref[...] = (_dot(xn_ref[...], w_ref[...]) * cs_ref[...]).astype(o_ref.dtype)


def norm_matmul(x, g, w, col_scale, out_dtype, tm=1024, tn=1024):
    n, d = x.shape
    d_out = w.shape[1]
    return pl.pallas_call(
        _norm_matmul_kernel,
        grid=(n // tm, d_out // tn),
        in_specs=[
            pl.BlockSpec((tm, d), lambda i, j: (i, 0)),
            pl.BlockSpec((1, d), lambda i, j: (0, 0)),
            pl.BlockSpec((d, tn), lambda i, j: (0, j)),
            pl.BlockSpec((1, tn), lambda i, j: (0, j)),
        ],
        out_specs=pl.BlockSpec((tm, tn), lambda i, j: (i, j)),
        out_shape=jax.ShapeDtypeStruct((n, d_out), out_dtype),
        scratch_shapes=[pltpu.VMEM((tm, d), BF16)],
        compiler_params=_cparams("parallel", "arbitrary"),
        name="norm_matmul",
    )(x, g.reshape(1, d), w, col_scale.reshape(1, d_out))


def _parity_selectors(n):
    t = lax.broadcasted_iota(jnp.int32, (n, n // 2), 0)
    s = lax.broadcasted_iota(jnp.int32, (n, n // 2), 1)
    return (jnp.where(t == 2 * s, 1.0, 0.0).astype(BF16),
            jnp.where(t == 2 * s + 1, 1.0, 0.0).astype(BF16))


def _proj_res_kernel(*refs, n_in, interleave):
    a_refs = refs[:n_in]
    w_refs = refs[n_in:2 * n_in]
    res_ref, o_ref = refs[2 * n_in], refs[2 * n_in + 1]
    acc = res_ref[...]
    for a_ref, w_ref in zip(a_refs, w_refs):
        if interleave:
            sel_e, sel_o = _parity_selectors(2 * a_ref.shape[2])
            a = (_dot(sel_e, a_ref[0, 0]) + _dot(sel_o, a_ref[0, 1])).astype(BF16)
        else:
            a = a_ref[...]
        acc = acc + _dot(a, w_ref[...])
    o_ref[...] = acc


def proj_residual(a_list, w_list, res, interleave=False, tm=512):
    n, d = res.shape
    n_in = len(a_list)
    if interleave:
        in_specs = [pl.BlockSpec((1, 2, tm // 2, a.shape[-1]), lambda i: (i, 0, 0, 0))
                    for a in a_list]
    else:
        in_specs = [pl.BlockSpec((tm, a.shape[1]), lambda i: (i, 0)) for a in a_list]
    in_specs += [pl.BlockSpec(w.shape, lambda i: (0, 0)) for w in w_list]
    in_specs += [pl.BlockSpec((tm, d), lambda i: (i, 0))]
    return pl.pallas_call(
        functools.partial(_proj_res_kernel, n_in=n_in, interleave=interleave),
        grid=(n // tm,),
        in_specs=in_specs,
        out_specs=pl.BlockSpec((tm, d), lambda i: (i, 0)),
        out_shape=jax.ShapeDtypeStruct((n, d), F32),
        compiler_params=_cparams("parallel"),
        name="proj_residual",
    )(*a_list, *w_list, res)


def _ffn_kernel(x_ref, g_ref, wg_ref, wu_ref, wo_ref, o_ref, xn_ref):
    @pl.when(pl.program_id(1) == 0)
    def _():
        x = x_ref[...]
        xn_ref[...] = _rms(x, g_ref[...]).astype(BF16)
        o_ref[...] = x

    xn = xn_ref[...]
    gate = _dot(xn, wg_ref[...])
    up = _dot(xn, wu_ref[...])
    act = (gate * jax.nn.sigmoid(gate) * up).astype(BF16)
    o_ref[...] += _dot(act, wo_ref[...])


def ffn(x, g, w_in, w_out, tm=1024, tf=512):
    n, d = x.shape
    d_ff = w_out.shape[0]
    nf = d_ff // tf
    return pl.pallas_call(
        _ffn_kernel,
        grid=(n // tm, nf),
        in_specs=[
            pl.BlockSpec((tm, d), lambda i, f: (i, 0)),
            pl.BlockSpec((1, d), lambda i, f: (0, 0)),
            pl.BlockSpec((d, tf), lambda i, f: (0, f)),
            pl.BlockSpec((d, tf), lambda i, f: (0, nf + f)),
            pl.BlockSpec((tf, d), lambda i, f: (f, 0)),
        ],
        out_specs=pl.BlockSpec((tm, d), lambda i, f: (i, 0)),
        out_shape=jax.ShapeDtypeStruct((n, d), F32),
        scratch_shapes=[pltpu.VMEM((tm, d), BF16)],
        compiler_params=_cparams("parallel", "arbitrary"),
        name="ffn",
    )(x, g.reshape(1, d), w_in, w_in, w_out)


def _ple_kernel(x_ref, p_ref, g_ref, wg_ref, wp_ref, gf_ref, o_ref, *, final_norm):
    x = x_ref[...]
    gate = jax.nn.sigmoid(_dot(_rms(x, g_ref[...]).astype(BF16), wg_ref[...]))
    proj = _dot(p_ref[...].astype(BF16), wp_ref[...])
    y = x + proj * gate
    if final_norm:
        y = _rms(y, gf_ref[...])
    o_ref[...] = y


def ple(x, p, g, w_gate, w_proj, g_final, final_norm, tm=512):
    n, d = x.shape
    dp = p.shape[1]
    return pl.pallas_call(
        functools.partial(_ple_kernel, final_norm=final_norm),
        grid=(n // tm,),
        in_specs=[
            pl.BlockSpec((tm, d), lambda i: (i, 0)),
            pl.BlockSpec((tm, dp), lambda i: (i, 0)),
            pl.BlockSpec((1, d), lambda i: (0, 0)),
            pl.BlockSpec((d, d), lambda i: (0, 0)),
            pl.BlockSpec((dp, d), lambda i: (0, 0)),
            pl.BlockSpec((1, d), lambda i: (0, 0)),
        ],
        out_specs=pl.BlockSpec((tm, d), lambda i: (i, 0)),
        out_shape=jax.ShapeDtypeStruct((n, d), F32),
        compiler_params=_cparams("parallel"),
        name="ple",
    )(x, p, g.reshape(1, d), w_gate, w_proj, g_final.reshape(1, d))


def t5_bucket(rel):
    nb = T5_BUCKETS // 2
    max_exact = nb // 2
    ret = jnp.where(rel > 0, nb, 0)
    n = jnp.abs(rel)
    nf = jnp.maximum(n, 1).astype(jnp.float32)
    large = max_exact + (jnp.log(nf / max_exact) / math.log(T5_MAX_DIST / max_exact)
                         * (nb - max_exact)).astype(jnp.int32)
    large = jnp.minimum(large, nb - 1)
    return ret + jnp.where(n < max_exact, n, large)


def _lane_tiles(x):
    return [x[:, j * LANES:(j + 1) * LANES] for j in range(x.shape[1] // LANES)]


def _diff_attn_kernel(lam_ref, q_ref, k_ref, v_ref, e_ref, g_ref, o_ref,
                      s1_ref, s2_ref, p1_ref, p2_ref, va_ref, *, tq, tk, seq, pad, width,
                      out_scale):
    i = pl.program_id(2)

    @pl.when(i == 0)
    def _():
        va_ref[:, :A_V_DIM] = v_ref[0]
        va_ref[:, A_V_DIM:] = jnp.ones((seq, A_V_DIM), BF16)

    q = q_ref[0]
    lane = lax.broadcasted_iota(jnp.int32, q.shape, 1)
    zero = jnp.zeros_like(q)
    q1 = jnp.where(lane < A_QK_DIM, q, zero)
    q2 = jnp.where(lane < A_QK_DIM, zero, q)

    def bias_tile(j):
        c = j * LANES // tk
        start = jnp.clip(c * tk - i * tq + pad, 0, width - tk) + (j * LANES - c * tk)
        return e_ref[0, :, pl.ds(pl.multiple_of(start, LANES), LANES)]

    m1 = m2 = None
    for c in range(seq // tk):
        cols = slice(c * tk, (c + 1) * tk)
        kc = k_ref[0, cols, :]
        x1 = _dot_nt(q1, kc)
        x2 = _dot_nt(q2, kc)
        s1_ref[:, cols] = x1
        s2_ref[:, cols] = x2
        for jj, (t1, t2) in enumerate(zip(_lane_tiles(x1), _lane_tiles(x2))):
            bias = bias_tile(c * (tk // LANES) + jj)
            m1 = t1 + bias if m1 is None else jnp.maximum(m1, t1 + bias)
            m2 = t2 + bias if m2 is None else jnp.maximum(m2, t2 + bias)
    m1 = jnp.broadcast_to(jnp.max(m1, axis=-1, keepdims=True), (tq, LANES))
    m2 = jnp.broadcast_to(jnp.max(m2, axis=-1, keepdims=True), (tq, LANES))
    for j in range(seq // LANES):
        cols = slice(j * LANES, (j + 1) * LANES)
        shift = bias_tile(j)
        p1_ref[:, cols] = jnp.exp2(s1_ref[:, cols] + (shift - m1)).astype(BF16)
        p2_ref[:, cols] = jnp.exp2(s2_ref[:, cols] + (shift - m2)).astype(BF16)
    o1 = _dot(p1_ref[...], va_ref[...])
    o2 = _dot(p2_ref[...], va_ref[...])
    o = (o1[:, :A_V_DIM] * (1.0 / o1[:, A_V_DIM:])
         - o2[:, :A_V_DIM] * (lam_ref[0] / o2[:, A_V_DIM:]))
    o = _rms(o, g_ref[...]) * out_scale
    o_ref[0] = o.astype(o_ref.dtype)


def _band_kernel(w_ref, o_ref, *, rows, cols):
    n = rows + cols
    x = jnp.broadcast_to(w_ref[0], (rows, n))
    y = pltpu.roll(x, n - (rows - 1), 1, stride=1, stride_axis=0)
    o_ref[0] = y[:, :cols]


def _toeplitz(w, rows, cols):
    heads, n = w.shape
    assert n == rows + cols and n % LANES == 0
    return pl.pallas_call(
        functools.partial(_band_kernel, rows=rows, cols=cols),
        grid=(heads,),
        in_specs=[pl.BlockSpec((1, 1, n), lambda h: (h, 0, 0))],
        out_specs=pl.BlockSpec((1, rows, cols), lambda h: (h, 0, 0)),
        out_shape=jax.ShapeDtypeStruct((heads, rows, cols), F32),
        compiler_params=_cparams("parallel"),
        name="toeplitz_band",
    )(w.reshape(heads, 1, n))


def diff_attention(z, lam, rel_table, subln_g, lam_init, tq=512, tk=512):
    b, seq, _ = z.shape
    pad = tk + T5_MAX_DIST
    width = 2 * tk + tq + 2 * T5_MAX_DIST
    rel = jnp.arange(tq + width, dtype=jnp.int32) - (tq - 1) - pad
    onehot = (t5_bucket(rel)[:, None] == jnp.arange(T5_BUCKETS)[None, :]).astype(F32)
    diag = jnp.dot(onehot, rel_table.astype(F32), precision=lax.Precision.HIGHEST)
    band = _toeplitz(diag.T * LOG2E, tq, width)
    hq = A_WIDTH // LANES
    return pl.pallas_call(
        functools.partial(_diff_attn_kernel, tq=tq, tk=tk, seq=seq, pad=pad, width=width,
                          out_scale=1.0 - lam_init),
        grid=(b, A_HEADS, seq // tq),
        in_specs=[
            pl.BlockSpec(memory_space=pltpu.SMEM),
            pl.BlockSpec((1, tq, LANES), lambda bi, h, i: (bi, i, h)),
            pl.BlockSpec((1, seq, LANES), lambda bi, h, i: (bi, 0, hq + h)),
            pl.BlockSpec((1, seq, LANES), lambda bi, h, i: (bi, 0, 2 * hq + h)),
            pl.BlockSpec((1, tq, width), lambda bi, h, i: (h, 0, 0)),
            pl.BlockSpec((1, A_V_DIM), lambda bi, h, i: (0, 0)),
        ],
        out_specs=pl.BlockSpec((1, tq, LANES), lambda bi, h, i: (bi, i, h)),
        out_shape=jax.ShapeDtypeStruct((b, seq, A_WIDTH), BF16),
        scratch_shapes=[pltpu.VMEM((tq, seq), F32), pltpu.VMEM((tq, seq), F32),
                        pltpu.VMEM((tq, seq), BF16), pltpu.VMEM((tq, seq), BF16),
                        pltpu.VMEM((seq, 2 * A_V_DIM), BF16)],
        compiler_params=_cparams("parallel", "parallel", "arbitrary"),
        name="diff_attention",
    )(lam.reshape(1).astype(F32), z, z, z, band, subln_g.reshape(1, A_V_DIM))


def _na_kernel(q_ref, k_ref, v_ref, nb_ref, o_ref, *, rows):
    def row_group(gi, carry):
        geom = []
        for u in range(NA_UNROLL):
            r = gi * NA_UNROLL + u
            r_start = jnp.clip(r - NA_ROWS // 2, 0, rows - NA_ROWS)
            geom.append((pl.multiple_of(r * GRID_W, GRID_W),
                         pl.multiple_of(r_start * GRID_W, GRID_W), r - r_start))
        scores = [_dot_nt(q_ref[0, pl.ds(q0, GRID_W), :], k_ref[0, pl.ds(k0, NA_KEYS), :])
                  + nb_ref[0, var] for q0, k0, var in geom]
        probs = []
        for s in scores:
            p = jnp.exp2(s - jnp.max(s, axis=-1, keepdims=True))
            probs.append((p * (1.0 / jnp.sum(p, axis=-1, keepdims=True))).astype(BF16))
        for p, (q0, k0, _) in zip(probs, geom):
            o = _dot(p, v_ref[0, pl.ds(k0, NA_KEYS), :])
            o_ref[0, pl.ds(q0, GRID_W), :] = o.astype(o_ref.dtype)
        return carry

    lax.fori_loop(0, rows // NA_UNROLL, row_group, 0)


def _na_bias(na_table):
    var = np.arange(NA_ROWS)[:, None, None]
    kr = np.arange(NA_ROWS)[None, :, None]
    row_sel = (kr - var + (NA_ROWS - 1) == np.arange(2 * NA_ROWS - 1)).astype(np.float32)
    c = np.arange(GRID_W)[:, None, None]
    kc = np.arange(GRID_W)[None, :, None]
    col_sel = (kc - c + (NA_COLS - 1) == np.arange(2 * NA_COLS - 1)).astype(np.float32)
    c_start = np.clip(c - NA_COLS // 2, 0, GRID_W - NA_COLS)
    valid = ((kc >= c_start) & (kc < c_start + NA_COLS))[None, None, :, None, :, 0]
    bias = jnp.einsum('vka,hab,cqb->hvckq', row_sel, na_table.astype(F32), col_sel,
                      precision=lax.Precision.HIGHEST)
    bias = jnp.where(valid, bias * LOG2E, MASK_VALUE)
    return bias.reshape(B_HEADS, NA_ROWS, GRID_W, NA_KEYS)


def neighbourhood_attention(z, na_table):
    b, seq, _ = z.shape
    rows = seq // GRID_W
    assert rows >= NA_ROWS and rows % NA_UNROLL == 0
    hq = B_WIDTH // LANES
    base = 3 * A_WIDTH // LANES
    return pl.pallas_call(
        functools.partial(_na_kernel, rows=rows),
        grid=(b, B_HEADS),
        in_specs=[
            pl.BlockSpec((1, seq, LANES), lambda bi, h: (bi, 0, base + h)),
            pl.BlockSpec((1, seq, LANES), lambda bi, h: (bi, 0, base + hq + h)),
            pl.BlockSpec((1, seq, LANES), lambda bi, h: (bi, 0, base + 2 * hq + h)),
            pl.BlockSpec((1, NA_ROWS, GRID_W, NA_KEYS), lambda bi, h: (h, 0, 0, 0)),
        ],
        out_specs=pl.BlockSpec((1, seq, LANES), lambda bi, h: (bi, 0, h)),
        out_shape=jax.ShapeDtypeStruct((b, seq, B_WIDTH), BF16),
        compiler_params=_cparams("parallel", "parallel"),
        name="neighbourhood_attention",
    )(z, z, z, _na_bias(na_table))


HALO = 16


def _hyena_in_kernel(x_ref, xp_ref, xq_ref, g_ref, w0_ref, w1_ref, wv_ref,
                     c0_ref, c1_ref, cv_ref, b0_ref, b1_ref, bv_ref,
                     x0_out, vv_out, xn_ref, *, tiles_per_seq):
    i = pl.program_id(0)
    tm = x_ref.shape[0]
    hk = tm // 2

    @pl.when(pl.program_id(1) == 0)
    def _():
        g = g_ref[...]
        xn = _rms(x_ref[...], g).astype(BF16)
        sel_e, sel_o = _parity_selectors(tm)
        tn_dims = (((0,), (0,)), ((), ()))
        xn_ref[HALO:HALO + hk, :] = lax.dot_general(
            sel_e, xn, tn_dims, preferred_element_type=F32).astype(BF16)
        xn_ref[HALO + hk:HALO + tm, :] = lax.dot_general(
            sel_o, xn, tn_dims, preferred_element_type=F32).astype(BF16)
        first = (i % tiles_per_seq) == 0
        last = (i % tiles_per_seq) == tiles_per_seq - 1
        xp = _rms(xp_ref[...], g)
        xq = _rms(xq_ref[...], g)
        xn_ref[0:HALO, :] = jnp.where(first, 0.0, xp).astype(BF16)
        xn_ref[HALO + tm:2 * HALO + tm, :] = jnp.where(last, 0.0, xq).astype(BF16)

    xn = xn_ref[...]
    row = lax.broadcasted_iota(jnp.int32, (hk, w0_ref.shape[1]), 0)

    def conv(w_ref, c_ref, b_ref):
        z = _dot(xn, w_ref[...])
        ze, zo = z[HALO:HALO + hk], z[HALO + hk:HALO + tm]
        zo_prev = jnp.where(row == 0, z[HALO - 1:HALO], pltpu.roll(zo, 1, 0))
        ze_next = jnp.where(row == hk - 1, z[HALO + tm:HALO + tm + 1],
                            pltpu.roll(ze, hk - 1, 0))
        c0, c1, c2, bias = c_ref[0:1, :], c_ref[1:2, :], c_ref[2:3, :], b_ref[...]
        return (zo_prev * c0 + ze * c1 + zo * c2 + bias,
                ze * c0 + zo * c1 + ze_next * c2 + bias)

    x0e, x0o = conv(w0_ref, c0_ref, b0_ref)
    x1e, x1o = conv(w1_ref, c1_ref, b1_ref)
    ve, vo = conv(wv_ref, cv_ref, bv_ref)
    x0_out[0, 0] = x0e.astype(x0_out.dtype)
    x0_out[0, 1] = x0o.astype(x0_out.dtype)
    vv_out[0, 0] = (ve * x1e).astype(vv_out.dtype)
    vv_out[0, 1] = (vo * x1o).astype(vv_out.dtype)


def hyena_in(h, seq, g, w_in, conv_w, conv_b, tc=512):
    n, d = h.shape
    tm = SPLIT_CHUNK
    nc = C_WIDTH // tc
    blocks = tm // HALO
    last_block = n // HALO - 1
    wspec = lambda o: pl.BlockSpec((d, tc), lambda i, c: (0, o * nc + c))
    cspec = lambda o: pl.BlockSpec((3, tc), lambda i, c: (0, o * nc + c))
    bspec = lambda o: pl.BlockSpec((1, tc), lambda i, c: (0, o * nc + c))
    cb = conv_b.reshape(1, -1)
    out = jax.ShapeDtypeStruct((n // tm, 2, tm // 2, C_WIDTH), BF16)
    ospec = pl.BlockSpec((1, 2, tm // 2, tc), lambda i, c: (i, 0, 0, c))
    return pl.pallas_call(
        functools.partial(_hyena_in_kernel, tiles_per_seq=seq // tm),
        grid=(n // tm, nc),
        in_specs=[
            pl.BlockSpec((tm, d), lambda i, c: (i, 0)),
            pl.BlockSpec((HALO, d), lambda i, c: (jnp.maximum(i * blocks - 1, 0), 0)),
            pl.BlockSpec((HALO, d), lambda i, c: (jnp.minimum((i + 1) * blocks, last_block), 0)),
            pl.BlockSpec((1, d), lambda i, c: (0, 0)),
            wspec(0), wspec(1), wspec(2), cspec(0), cspec(1), cspec(2),
            bspec(0), bspec(1), bspec(2),
        ],
        out_specs=[ospec, ospec],
        out_shape=[out, out],
        scratch_shapes=[pltpu.VMEM((tm + 2 * HALO, d), BF16)],
        compiler_params=_cparams("parallel", "arbitrary"),
        name="hyena_in",
    )(h, h, h, g.reshape(1, d), w_in, w_in, w_in, conv_w, conv_w, conv_w, cb, cb, cb)


def _filter_kernel(hdn_ref, wf_ref, wb_ref, dl_ref, ae_ref, be_ref, fo_ref, bo_ref, *, tl, seq):
    hd = hdn_ref[...].astype(BF16)
    lane = lax.broadcasted_iota(jnp.int32, hd.shape, 1)
    zero = jnp.zeros_like(hd)
    hd_e = jnp.where(lane < FILTER_HIDDEN, hd, zero)
    hd_o = jnp.where(lane < FILTER_HIDDEN, zero, hd)
    j = pl.program_id(0) * tl + lax.broadcasted_iota(jnp.int32, ae_ref.shape, 0)
    dl = dl_ref[...]
    decay_e = jnp.exp((2 * j).astype(F32) * (-1.0 / (seq - 1)) * dl)
    decay_o = jnp.exp((2 * j + 1).astype(F32) * (-1.0 / (seq - 1)) * dl)
    wf, wb = wf_ref[...], wb_ref[...]
    hf_e = _dot(hd_e, wf) * decay_e
    hb_e = jnp.where(j == 0, 0.0, _dot(hd_e, wb) * decay_e)
    ae_ref[...] = (hf_e + hb_e).astype(ae_ref.dtype)
    be_ref[...] = (hf_e - hb_e).astype(be_ref.dtype)
    fo_ref[...] = (_dot(hd_o, wf) * decay_o).astype(fo_ref.dtype)
    bo_ref[...] = (_dot(hd_o, wb) * decay_o).astype(bo_ref.dtype)


def hyena_filter_taps(hdn, w3, deltas, tl=512, tc=512):
    seq, hidden = hdn.shape
    half = seq // 2
    nc = C_WIDTH // tc
    w3s = jnp.concatenate([w3, w3], axis=0)
    out = jax.ShapeDtypeStruct((half, C_WIDTH), BF16)
    return pl.pallas_call(
        functools.partial(_filter_kernel, tl=tl, seq=seq),
        grid=(half // tl, nc),
        in_specs=[
            pl.BlockSpec((tl, 2 * hidden), lambda l, c: (l, 0)),
            pl.BlockSpec((2 * hidden, tc), lambda l, c: (0, c)),
            pl.BlockSpec((2 * hidden, tc), lambda l, c: (0, nc + c)),
            pl.BlockSpec((1, tc), lambda l, c: (0, c)),
        ],
        out_specs=[pl.BlockSpec((tl, tc), lambda l, c: (l, c))] * 4,
        out_shape=[out] * 4,
        compiler_params=_cparams("parallel", "parallel"),
        name="hyena_filter_taps",
    )(hdn.reshape(half, 2 * hidden), w3s, w3s, deltas.reshape(1, C_WIDTH))


NYQ_ROWS = 16


def _filter_dft_kernel(m_ref, tw_ref, ae_ref, be_ref, fo_ref, bo_ref,
                       sr_ref, si_ref, tr_ref, ti_ref, ur_ref, ui_ref, nyq_ref, *, tf, seq):
    fi = pl.program_id(1)
    cos_m, sin_m = m_ref[0], m_ref[1]
    fo, bo = fo_ref[...], bo_ref[...]
    row0 = (fi * tf + lax.broadcasted_iota(jnp.int32, sr_ref.shape, 0)) == 0
    e_r = _dot(cos_m, ae_ref[...])
    e_i = jnp.where(row0, 0.0, _dot(sin_m, be_ref[...]))
    nyq_f, nyq_b = _dot(sin_m, fo), _dot(sin_m, bo)
    f_r, f_i = _dot(cos_m, fo), jnp.where(row0, 0.0, nyq_f)
    b_r, b_i = _dot(cos_m, bo), jnp.where(row0, 0.0, -nyq_b)
    cw, sw = tw_ref[:, 0:1], tw_ref[:, 1:2]
    wt = jnp.where(row0, 1.0 / seq, 2.0 / seq)
    sr_ref[...] = e_r * wt
    si_ref[...] = e_i * wt
    tr_ref[...] = (cw * f_r + sw * f_i + b_r) * wt
    ti_ref[...] = (cw * f_i - sw * f_r + b_i) * wt
    ur_ref[...] = (f_r + cw * b_r - sw * b_i) * wt
    ui_ref[...] = (f_i + cw * b_i + sw * b_r) * wt

    @pl.when(fi == 0)
    def _():
        e_n = _dot(sin_m[0:NYQ_ROWS, :], ae_ref[...])
        f_n, b_n = nyq_f[0:NYQ_ROWS, :], nyq_b[0:NYQ_ROWS, :]
        nyq_ref[0] = e_n * (1.0 / seq)
        nyq_ref[1] = (b_n - f_n) * (1.0 / seq)
        nyq_ref[2] = (f_n - b_n) * (1.0 / seq)


def hyena_filter_dft(mmat, twiddle, taps, tf=256, tc=512):
    half = mmat.shape[1]
    coef = jax.ShapeDtypeStruct((half, C_WIDTH), F32)
    tap_spec = pl.BlockSpec((half, tc), lambda c, f: (0, c))
    coef_spec = pl.BlockSpec((tf, tc), lambda c, f: (f, c))
    return pl.pallas_call(
        functools.partial(_filter_dft_kernel, tf=tf, seq=2 * half),
        grid=(C_WIDTH // tc, half // tf),
        in_specs=[pl.BlockSpec((2, tf, half), lambda c, f: (0, f, 0)),
                  pl.BlockSpec((tf, 2), lambda c, f: (f, 0))] + [tap_spec] * 4,
        out_specs=[coef_spec] * 6 + [pl.BlockSpec((3, NYQ_ROWS, tc), lambda c, f: (0, 0, c))],
        out_shape=[coef] * 6 + [jax.ShapeDtypeStruct((3, NYQ_ROWS, C_WIDTH), F32)],
        compiler_params=_cparams("parallel", "arbitrary"),
        name="hyena_filter_dft",
    )(mmat, twiddle, *taps)


def _dft_fwd_kernel(m_ref, ve_ref, vo_ref, sr_ref, si_ref, tr_ref, ti_ref, ur_ref, ui_ref,
                    nyq_ref, z_ref, *, tf):
    m = m_ref[...].reshape(2 * tf, m_ref.shape[2])
    half, tc = m_ref.shape[2], ve_ref.shape[-1]
    a = _dot(m, ve_ref[0, :, 0].reshape(half, tc))
    b = _dot(m, vo_ref[0, :, 0].reshape(half, tc))
    ar, ai, br, bi = a[:tf], a[tf:], b[:tf], b[tf:]
    sr, si, tr, ti, ur, ui = (r[...] for r in (sr_ref, si_ref, tr_ref, ti_ref, ur_ref, ui_ref))
    row0 = (pl.program_id(2) * tf + lax.broadcasted_iota(jnp.int32, sr.shape, 0)) == 0
    sr_n = jnp.where(row0, nyq_ref[0, 0:1, :], sr)
    tr_n = jnp.where(row0, nyq_ref[1, 0:1, :], tr)
    ur_n = jnp.where(row0, nyq_ref[2, 0:1, :], ur)
    z_ref[0, 0, 0] = (sr * ar - si * ai + tr * br - ti * bi).astype(z_ref.dtype)
    z_ref[0, 0, 1] = (sr_n * ai + si * ar + tr_n * bi + ti * br).astype(z_ref.dtype)
    z_ref[0, 1, 0] = (ur * ar - ui * ai + sr * br - si * bi).astype(z_ref.dtype)
    z_ref[0, 1, 1] = (ur_n * ai + ui * ar + sr_n * bi + si * br).astype(z_ref.dtype)


def hyena_dft_fwd(mmat, vv, coefs, nyq, tf=256, tc=1024):
    b, nck, _, hck, _ = vv.shape
    half = nck * hck
    coef_spec = pl.BlockSpec((tf, tc), lambda bi, c, f: (f, c))
    return pl.pallas_call(
        functools.partial(_dft_fwd_kernel, tf=tf),
        grid=(b, C_WIDTH // tc, half // tf),
        in_specs=[pl.BlockSpec((2, tf, half), lambda bi, c, f: (0, f, 0)),
                  pl.BlockSpec((1, nck, 1, hck, tc), lambda bi, c, f: (bi, 0, 0, 0, c)),
                  pl.BlockSpec((1, nck, 1, hck, tc), lambda bi, c, f: (bi, 0, 1, 0, c))]
                 + [coef_spec] * 6
                 + [pl.BlockSpec((3, NYQ_ROWS, tc), lambda bi, c, f: (0, 0, c))],
        out_specs=pl.BlockSpec((1, 2, 2, tf, tc), lambda bi, c, f: (bi, 0, 0, f, c)),
        out_shape=jax.ShapeDtypeStruct((b, 2, 2, half, C_WIDTH), BF16),
        compiler_params=_cparams("parallel", "parallel", "parallel"),
        name="hyena_dft_fwd",
    )(mmat, vv, vv, *coefs, nyq)


def _dft_inv_kernel(g_ref, z_ref, vv_ref, x0_ref, skip_ref, u_ref):
    g = g_ref[...]
    nk, _, hck, tc = u_ref.shape[1:]
    for parity in range(2):
        y = _dot(g, z_ref[0, parity])
        vv = vv_ref[0, :, parity].reshape(nk * hck, tc).astype(F32)
        x0 = x0_ref[0, :, parity].reshape(nk * hck, tc).astype(F32)
        u = ((y + vv * skip_ref[...]) * x0).astype(u_ref.dtype)
        u_ref[0, :, parity] = u.reshape(nk, hck, tc)


def hyena_dft_inv(gmat, zs, vv, x0, skip, tt=1024, tc=512):
    b, nck, _, hck, _ = vv.shape
    half = nck * hck
    tt = min(tt, half)
    io_spec = pl.BlockSpec((1, tt // hck, 2, hck, tc), lambda bi, c, t: (bi, t, 0, 0, c))
    return pl.pallas_call(
        _dft_inv_kernel,
        grid=(b, C_WIDTH // tc, half // tt),
        in_specs=[
            pl.BlockSpec((tt, 2 * half), lambda bi, c, t: (t, 0)),
            pl.BlockSpec((1, 2, 2 * half, tc), lambda bi, c, t: (bi, 0, 0, c)),
            io_spec,
            io_spec,
            pl.BlockSpec((1, tc), lambda bi, c, t: (0, c)),
        ],
        out_specs=io_spec,
        out_shape=jax.ShapeDtypeStruct(vv.shape, BF16),
        compiler_params=_cparams("parallel", "parallel", "parallel"),
        name="hyena_dft_inv",
    )(gmat, zs, vv, x0, skip.reshape(1, C_WIDTH))


def _dft_matrices(half):
    idx = jnp.arange(half, dtype=jnp.int32)
    ang = ((idx[:, None] * idx[None, :]) % (2 * half)).astype(F32) * (math.pi / half)
    sin_part = (-jnp.sin(ang)).at[0].set((1 - 2 * (idx % 2)).astype(F32))
    mmat = jnp.stack([jnp.cos(ang), sin_part]).astype(BF16)
    tw = idx.astype(F32) * (math.pi / half)
    return mmat, mmat.reshape(2 * half, half).T, jnp.stack([jnp.cos(tw), jnp.sin(tw)], axis=1)


def _filter_hidden(seq, w1, b1, freq, w2, b2):
    t = jnp.linspace(0.0, 1.0, seq, dtype=F32)[:, None]
    bands = (FILTER_EMB - 1) // 2
    w = 2.0 * math.pi * jnp.arange(seq, dtype=F32) / seq
    f = jnp.linspace(1e-4, bands - 1, bands, dtype=F32)
    ang = w[:, None] * f[None, :]
    feats = jnp.concatenate([t, jnp.cos(ang), -jnp.sin(ang)], axis=-1)
    hdn = jnp.sin(freq[0] * (feats @ w1 + b1))
    return jnp.sin(freq[1] * (hdn @ w2 + b2))


def hyena_mixer(h, batch, seq, g, w_in, conv_w, conv_b, w1, b1, freq, w2, b2, w3, skip, w_out):
    half = seq // 2
    split_shape = (batch, seq // SPLIT_CHUNK, 2, SPLIT_CHUNK // 2, C_WIDTH)
    x0, vv = (a.reshape(split_shape) for a in hyena_in(h, seq, g, w_in, conv_w, conv_b))
    hdn = _filter_hidden(seq, w1, b1, freq, w2, b2)
    min_decay = math.log(FILTER_TARGET) / FAST_DECAY_PCT
    max_decay = math.log(FILTER_TARGET) / SLOW_DECAY_PCT
    deltas = jnp.abs(jnp.linspace(min_decay, max_decay, C_WIDTH, dtype=F32))
    taps = hyena_filter_taps(hdn, w3, deltas)
    mmat, gmat, twiddle = _dft_matrices(half)
    *coefs, nyq = hyena_filter_dft(mmat, twiddle, taps)
    zs = hyena_dft_fwd(mmat, vv, coefs, nyq).reshape(batch, 2, seq, C_WIDTH)
    u = hyena_dft_inv(gmat, zs, vv, x0, skip)
    u = u.reshape(batch * seq // SPLIT_CHUNK, 2, SPLIT_CHUNK // 2, C_WIDTH)
    return proj_residual([u], [w_out], h, interleave=True, tm=SPLIT_CHUNK)


def even_mixer(h, batch, seq, g, w_in, w_out, lam_vec, subln_g, na_table, rel_table, layer_idx):
    col_scale = np.ones((6, A_WIDTH), np.float32)
    col_scale[0] = A_QK_DIM ** -0.5 * LOG2E
    col_scale[3] = B_HEAD_DIM ** -0.5 * LOG2E
    z = norm_matmul(h, g, w_in, jnp.asarray(col_scale.reshape(-1)), BF16)
    z = z.reshape(batch, seq, w_in.shape[1])
    lam_init = 0.8 - 0.6 * math.exp(-0.3 * layer_idx)
    lv = lam_vec.astype(F32)
    lam = jnp.exp(jnp.sum(lv[0] * lv[1])) - jnp.exp(jnp.sum(lv[2] * lv[3])) + lam_init
    oa = diff_attention(z, lam, rel_table, subln_g, lam_init)
    ob = neighbourhood_attention(z, na_table)
    n = batch * seq
    return proj_residual([oa.reshape(n, A_WIDTH), ob.reshape(n, B_WIDTH)],
                         [w_out[:A_WIDTH], w_out[A_WIDTH:]], h)


def _trunk(x, p, W):
    batch, seq, d = x.shape
    depth = p.shape[0]
    h = x.reshape(batch * seq, d)
    for i in range(depth):
        j = i // 2
        if i % 2 == 0:
            h = even_mixer(h, batch, seq, W['norm_mix'][i], W['ab_w_in'][j], W['ab_w_out'][j],
                           W['diff_lambda'][j], W['diff_subln'][j], W['na_bias'][j],
                           W['rel_bias_table'], i)
        else:
            h = hyena_mixer(h, batch, seq, W['norm_mix'][i], W['c_w_in'][j], W['c_conv_w'][j],
                            W['c_conv_b'][j], W['c_filt_w1'][j], W['c_filt_b1'][j],
                            W['c_filt_freq'][j], W['c_filt_w2'][j], W['c_filt_b2'][j],
                            W['c_filt_w3'][j], W['c_skip'][j], W['c_w_out'][j])
        h = ffn(h, W['norm_ffn'][i], W['ffn_w_in'][i], W['ffn_w_out'][i])
        h = ple(h, p[i].reshape(batch * seq, -1), W['norm_ple'][i], W['ple_w_gate'][i],
                W['ple_w_proj'][i], W['final_norm'], final_norm=(i == depth - 1))
    return h.reshape(batch, seq, d)


def kernel(x_prompt, x_sample, p_prompt, p_sample, rel_bias_table, norm_mix, norm_ffn, norm_ple, final_norm, ab_w_in, ab_w_out, diff_lambda, diff_subln, na_bias, c_w_in, c_conv_w, c_conv_b, c_filt_w1, c_filt_b1, c_filt_freq, c_filt_w2, c_filt_b2, c_filt_w3, c_skip, c_w_out, ffn_w_in, ffn_w_out, ple_w_proj, ple_w_gate):
    bf = lambda w: w.astype(BF16)
    W = dict(rel_bias_table=rel_bias_table, norm_mix=norm_mix, norm_ffn=norm_ffn,
             norm_ple=norm_ple, final_norm=final_norm, ab_w_in=bf(ab_w_in), ab_w_out=bf(ab_w_out),
             diff_lambda=diff_lambda, diff_subln=diff_subln, na_bias=na_bias,
             c_w_in=bf(c_w_in), c_conv_w=c_conv_w, c_conv_b=c_conv_b, c_filt_w1=c_filt_w1,
             c_filt_b1=c_filt_b1, c_filt_freq=c_filt_freq, c_filt_w2=c_filt_w2,
             c_filt_b2=c_filt_b2, c_filt_w3=bf(c_filt_w3), c_skip=c_skip, c_w_out=bf(c_w_out),
             ffn_w_in=bf(ffn_w_in), ffn_w_out=bf(ffn_w_out), ple_w_proj=bf(ple_w_proj),
             ple_w_gate=bf(ple_w_gate))
    return (_trunk(x_prompt, p_prompt, W), _trunk(x_sample, p_sample, W))
```

```python
import functools
import math

import numpy as np
import jax
import jax.numpy as jnp
from jax import lax
from jax.experimental import pallas as pl
from jax.experimental.pallas import tpu as pltpu

F32 = jnp.float32
BF16 = jnp.bfloat16

D_MODEL = 2048
GRID_W = 64
A_HEADS = 8
A_QK_DIM = 64
A_V_DIM = 2 * A_QK_DIM
A_WIDTH = A_HEADS * A_V_DIM
T5_BUCKETS = 32
T5_MAX_DIST = 128
B_HEADS = 8
B_HEAD_DIM = 128
B_WIDTH = B_HEADS * B_HEAD_DIM
NA_ROWS = 8
NA_COLS = 16
C_WIDTH = D_MODEL
FILTER_EMB = 33
FILTER_HIDDEN = 64
FILTER_TARGET = 1e-2
FAST_DECAY_PCT = 0.3
SLOW_DECAY_PCT = 1.5
EPS = 1e-6

LANES = 128
NA_KEYS = NA_ROWS * GRID_W
MASK_VALUE = -1e30
LOG2E = math.log2(math.e)
POLY = 4
SPLIT_CHUNK = 512
NA_UNROLL = 8
VMEM_LIMIT = 56 * 1024 * 1024


def _cparams(*sem):
    return pltpu.CompilerParams(dimension_semantics=sem, vmem_limit_bytes=VMEM_LIMIT)


def _rms(x, g):
    ms = jnp.mean(x * x, axis=-1, keepdims=True)
    return x * lax.rsqrt(ms + EPS) * g


def _dot(a, b):
    return jnp.dot(a, b, preferred_element_type=F32)


def _dot_nt(a, b):
    return lax.dot_general(a, b, (((1,), (1,)), ((), ())), preferred_element_type=F32)


def _norm_matmul_kernel(x_ref, g_ref, w_ref, cs_ref, o_ref, xn_ref):
    @pl.when(pl.program_id(1) == 0)
    def _():
        xn_ref[...] = _rms(x_ref[...], g_ref[...]).astype(BF16)

    o_ref[...] = (_dot(xn_ref[...], w_ref[...]) * cs_ref[...]).astype(o_ref.dtype)


def norm_matmul(x, g, w, col_scale, out_dtype, tm=1024, tn=1024):
    n, d = x.shape
    d_out = w.shape[1]
    return pl.pallas_call(
        _norm_matmul_kernel,
        grid=(n // tm, d_out // tn),
        in_specs=[
            pl.BlockSpec((tm, d), lambda i, j: (i, 0)),
            pl.BlockSpec((1, d), lambda i, j: (0, 0)),
            pl.BlockSpec((d, tn), lambda i, j: (0, j)),
            pl.BlockSpec((1, tn), lambda i, j: (0, j)),
        ],
        out_specs=pl.BlockSpec((tm, tn), lambda i, j: (i, j)),
        out_shape=jax.ShapeDtypeStruct((n, d_out), out_dtype),
        scratch_shapes=[pltpu.VMEM((tm, d), BF16)],
        compiler_params=_cparams("parallel", "arbitrary"),
        name="norm_matmul",
    )(x, g.reshape(1, d), w, col_scale.reshape(1, d_out))


def _phase_permutation(n):
    t = lax.broadcasted_iota(jnp.int32, (n, n), 0)
    j = lax.broadcasted_iota(jnp.int32, (n, n), 1)
    return jnp.where(j == (t % POLY) * (n // POLY) + t // POLY, 1.0, 0.0).astype(BF16)


def _proj_res_kernel(*refs, n_in, interleave):
    a_refs = refs[:n_in]
    w_refs = refs[n_in:2 * n_in]
    res_ref, o_ref = refs[2 * n_in], refs[2 * n_in + 1]
    acc = res_ref[...]
    for a_ref, w_ref in zip(a_refs, w_refs):
        if interleave:
            n = POLY * a_ref.shape[2]
            a = _dot(_phase_permutation(n), a_ref[0].reshape(n, a_ref.shape[3])).astype(BF16)
        else:
            a = a_ref[...]
        acc = acc + _dot(a, w_ref[...])
    o_ref[...] = acc


def proj_residual(a_list, w_list, res, interleave=False, tm=512):
    n, d = res.shape
    n_in = len(a_list)
    if interleave:
        in_specs = [pl.BlockSpec((1, POLY, tm // POLY, a.shape[-1]), lambda i: (i, 0, 0, 0))
                    for a in a_list]
    else:
        in_specs = [pl.BlockSpec((tm, a.shape[1]), lambda i: (i, 0)) for a in a_list]
    in_specs += [pl.BlockSpec(w.shape, lambda i: (0, 0)) for w in w_list]
    in_specs += [pl.BlockSpec((tm, d), lambda i: (i, 0))]
    return pl.pallas_call(
        functools.partial(_proj_res_kernel, n_in=n_in, interleave=interleave),
        grid=(n // tm,),
        in_specs=in_specs,
        out_specs=pl.BlockSpec((tm, d), lambda i: (i, 0)),
        out_shape=jax.ShapeDtypeStruct((n, d), F32),
        compiler_params=_cparams("parallel"),
        name="proj_residual",
    )(*a_list, *w_list, res)


def _ffn_kernel(x_ref, g_ref, wg_ref, wu_ref, wo_ref, o_ref, xn_ref):
    @pl.when(pl.program_id(1) == 0)
    def _():
        x = x_ref[...]
        xn_ref[...] = _rms(x, g_ref[...]).astype(BF16)
        o_ref[...] = x

    xn = xn_ref[...]
    gate = _dot(xn, wg_ref[...])
    up = _dot(xn, wu_ref[...])
    act = (gate * jax.nn.sigmoid(gate) * up).astype(BF16)
    o_ref[...] += _dot(act, wo_ref[...])


def ffn(x, g, w_in, w_out, tm=1024, tf=512):
    n, d = x.shape
    d_ff = w_out.shape[0]
    nf = d_ff // tf
    return pl.pallas_call(
        _ffn_kernel,
        grid=(n // tm, nf),
        in_specs=[
            pl.BlockSpec((tm, d), lambda i, f: (i, 0)),
            pl.BlockSpec((1, d), lambda i, f: (0, 0)),
            pl.BlockSpec((d, tf), lambda i, f: (0, f)),
            pl.BlockSpec((d, tf), lambda i, f: (0, nf + f)),
            pl.BlockSpec((tf, d), lambda i, f: (f, 0)),
        ],
        out_specs=pl.BlockSpec((tm, d), lambda i, f: (i, 0)),
        out_shape=jax.ShapeDtypeStruct((n, d), F32),
        scratch_shapes=[pltpu.VMEM((tm, d), BF16)],
        compiler_params=_cparams("parallel", "arbitrary"),
        name="ffn",
    )(x, g.reshape(1, d), w_in, w_in, w_out)


def _ple_kernel(x_ref, p_ref, g_ref, wg_ref, wp_ref, gf_ref, o_ref, *, final_norm):
    x = x_ref[...]
    gate = jax.nn.sigmoid(_dot(_rms(x, g_ref[...]).astype(BF16), wg_ref[...]))
    proj = _dot(p_ref[...].astype(BF16), wp_ref[...])
    y = x + proj * gate
    if final_norm:
        y = _rms(y, gf_ref[...])
    o_ref[...] = y


def ple(x, p, g, w_gate, w_proj, g_final, final_norm, tm=512):
    n, d = x.shape
    dp = p.shape[1]
    return pl.pallas_call(
        functools.partial(_ple_kernel, final_norm=final_norm),
        grid=(n // tm,),
        in_specs=[
            pl.BlockSpec((tm, d), lambda i: (i, 0)),
            pl.BlockSpec((tm, dp), lambda i: (i, 0)),
            pl.BlockSpec((1, d), lambda i: (0, 0)),
            pl.BlockSpec((d, d), lambda i: (0, 0)),
            pl.BlockSpec((dp, d), lambda i: (0, 0)),
            pl.BlockSpec((1, d), lambda i: (0, 0)),
        ],
        out_specs=pl.BlockSpec((tm, d), lambda i: (i, 0)),
        out_shape=jax.ShapeDtypeStruct((n, d), F32),
        compiler_params=_cparams("parallel"),
        name="ple",
    )(x, p, g.reshape(1, d), w_gate, w_proj, g_final.reshape(1, d))


def t5_bucket(rel):
    nb = T5_BUCKETS // 2
    max_exact = nb // 2
    ret = jnp.where(rel > 0, nb, 0)
    n = jnp.abs(rel)
    nf = jnp.maximum(n, 1).astype(jnp.float32)
    large = max_exact + (jnp.log(nf / max_exact) / math.log(T5_MAX_DIST / max_exact)
                         * (nb - max_exact)).astype(jnp.int32)
    large = jnp.minimum(large, nb - 1)
    return ret + jnp.where(n < max_exact, n, large)


def _lane_tiles(x):
    return [x[:, j * LANES:(j + 1) * LANES] for j in range(x.shape[1] // LANES)]


def _diff_attn_kernel(lam_ref, q_ref, k_ref, v_ref, e_ref, g_ref, o_ref,
                      s1_ref, s2_ref, p1_ref, p2_ref, va_ref, *, tq, tk, seq, pad, width,
                      out_scale):
    i = pl.program_id(2)

    @pl.when(i == 0)
    def _():
        va_ref[:, :A_V_DIM] = v_ref[0]
        va_ref[:, A_V_DIM:] = jnp.ones((seq, A_V_DIM), BF16)

    q = q_ref[0]
    lane = lax.broadcasted_iota(jnp.int32, q.shape, 1)
    zero = jnp.zeros_like(q)
    q1 = jnp.where(lane < A_QK_DIM, q, zero)
    q2 = jnp.where(lane < A_QK_DIM, zero, q)

    def bias_tile(j):
        c = j * LANES // tk
        start = jnp.clip(c * tk - i * tq + pad, 0, width - tk) + (j * LANES - c * tk)
        return e_ref[0, :, pl.ds(pl.multiple_of(start, LANES), LANES)]

    m1 = m2 = None
    for c in range(seq // tk):
        cols = slice(c * tk, (c + 1) * tk)
        kc = k_ref[0, cols, :]
        x1 = _dot_nt(q1, kc)
        x2 = _dot_nt(q2, kc)
        s1_ref[:, cols] = x1
        s2_ref[:, cols] = x2
        for jj, (t1, t2) in enumerate(zip(_lane_tiles(x1), _lane_tiles(x2))):
            bias = bias_tile(c * (tk // LANES) + jj)
            m1 = t1 + bias if m1 is None else jnp.maximum(m1, t1 + bias)
            m2 = t2 + bias if m2 is None else jnp.maximum(m2, t2 + bias)
    m1 = jnp.broadcast_to(jnp.max(m1, axis=-1, keepdims=True), (tq, LANES))
    m2 = jnp.broadcast_to(jnp.max(m2, axis=-1, keepdims=True), (tq, LANES))
    for j in range(seq // LANES):
        cols = slice(j * LANES, (j + 1) * LANES)
        shift = bias_tile(j)
        p1_ref[:, cols] = jnp.exp2(s1_ref[:, cols] + (shift - m1)).astype(BF16)
        p2_ref[:, cols] = jnp.exp2(s2_ref[:, cols] + (shift - m2)).astype(BF16)
    o1 = _dot(p1_ref[...], va_ref[...])
    o2 = _dot(p2_ref[...], va_ref[...])
    o = (o1[:, :A_V_DIM] * (1.0 / o1[:, A_V_DIM:])
         - o2[:, :A_V_DIM] * (lam_ref[0] / o2[:, A_V_DIM:]))
    o = _rms(o, g_ref[...]) * out_scale
    o_ref[0] = o.astype(o_ref.dtype)


def _band_kernel(w_ref, o_ref, *, rows, cols):
    n = rows + cols
    x = jnp.broadcast_to(w_ref[0], (rows, n))
    y = pltpu.roll(x, n - (rows - 1), 1, stride=1, stride_axis=0)
    o_ref[0] = y[:, :cols]


def _toeplitz(w, rows, cols):
    heads, n = w.shape
    assert n == rows + cols and n % LANES == 0
    return pl.pallas_call(
        functools.partial(_band_kernel, rows=rows, cols=cols),
        grid=(heads,),
        in_specs=[pl.BlockSpec((1, 1, n), lambda h: (h, 0, 0))],
        out_specs=pl.BlockSpec((1, rows, cols), lambda h: (h, 0, 0)),
        out_shape=jax.ShapeDtypeStruct((heads, rows, cols), F32),
        compiler_params=_cparams("parallel"),
        name="toeplitz_band",
    )(w.reshape(heads, 1, n))


def diff_attention(z, lam, rel_table, subln_g, lam_init, tq=512, tk=512):
    b, seq, _ = z.shape
    pad = tk + T5_MAX_DIST
    width = 2 * tk + tq + 2 * T5_MAX_DIST
    rel = jnp.arange(tq + width, dtype=jnp.int32) - (tq - 1) - pad
    onehot = (t5_bucket(rel)[:, None] == jnp.arange(T5_BUCKETS)[None, :]).astype(F32)
    diag = jnp.dot(onehot, rel_table.astype(F32), precision=lax.Precision.HIGHEST)
    band = _toeplitz(diag.T * LOG2E, tq, width)
    hq = A_WIDTH // LANES
    return pl.pallas_call(
        functools.partial(_diff_attn_kernel, tq=tq, tk=tk, seq=seq, pad=pad, width=width,
                          out_scale=1.0 - lam_init),
        grid=(b, A_HEADS, seq // tq),
        in_specs=[
            pl.BlockSpec(memory_space=pltpu.SMEM),
            pl.BlockSpec((1, tq, LANES), lambda bi, h, i: (bi, i, h)),
            pl.BlockSpec((1, seq, LANES), lambda bi, h, i: (bi, 0, hq + h)),
            pl.BlockSpec((1, seq, LANES), lambda bi, h, i: (bi, 0, 2 * hq + h)),
            pl.BlockSpec((1, tq, width), lambda bi, h, i: (h, 0, 0)),
            pl.BlockSpec((1, A_V_DIM), lambda bi, h, i: (0, 0)),
        ],
        out_specs=pl.BlockSpec((1, tq, LANES), lambda bi, h, i: (bi, i, h)),
        out_shape=jax.ShapeDtypeStruct((b, seq, A_WIDTH), BF16),
        scratch_shapes=[pltpu.VMEM((tq, seq), F32), pltpu.VMEM((tq, seq), F32),
                        pltpu.VMEM((tq, seq), BF16), pltpu.VMEM((tq, seq), BF16),
                        pltpu.VMEM((seq, 2 * A_V_DIM), BF16)],
        compiler_params=_cparams("parallel", "parallel", "arbitrary"),
        name="diff_attention",
    )(lam.reshape(1).astype(F32), z, z, z, band, subln_g.reshape(1, A_V_DIM))


def _na_kernel(q_ref, k_ref, v_ref, nb_ref, o_ref, *, rows):
    def row_group(gi, carry):
        geom = []
        for u in range(NA_UNROLL):
            r = gi * NA_UNROLL + u
            r_start = jnp.clip(r - NA_ROWS // 2, 0, rows - NA_ROWS)
            geom.append((pl.multiple_of(r * GRID_W, GRID_W),
                         pl.multiple_of(r_start * GRID_W, GRID_W), r - r_start))
        scores = [_dot_nt(q_ref[0, pl.ds(q0, GRID_W), :], k_ref[0, pl.ds(k0, NA_KEYS), :])
                  + nb_ref[0, var] for q0, k0, var in geom]
        probs = []
        for s in scores:
            p = jnp.exp2(s - jnp.max(s, axis=-1, keepdims=True))
            probs.append((p * (1.0 / jnp.sum(p, axis=-1, keepdims=True))).astype(BF16))
        for p, (q0, k0, _) in zip(probs, geom):
            o = _dot(p, v_ref[0, pl.ds(k0, NA_KEYS), :])
            o_ref[0, pl.ds(q0, GRID_W), :] = o.astype(o_ref.dtype)
        return carry

    lax.fori_loop(0, rows // NA_UNROLL, row_group, 0)


def _na_bias(na_table):
    var = np.arange(NA_ROWS)[:, None, None]
    kr = np.arange(NA_ROWS)[None, :, None]
    row_sel = (kr - var + (NA_ROWS - 1) == np.arange(2 * NA_ROWS - 1)).astype(np.float32)
    c = np.arange(GRID_W)[:, None, None]
    kc = np.arange(GRID_W)[None, :, None]
    col_sel = (kc - c + (NA_COLS - 1) == np.arange(2 * NA_COLS - 1)).astype(np.float32)
    c_start = np.clip(c - NA_COLS // 2, 0, GRID_W - NA_COLS)
    valid = ((kc >= c_start) & (kc < c_start + NA_COLS))[None, None, :, None, :, 0]
    bias = jnp.einsum('vka,hab,cqb->hvckq', row_sel, na_table.astype(F32), col_sel,
                      precision=lax.Precision.HIGHEST)
    bias = jnp.where(valid, bias * LOG2E, MASK_VALUE)
    return bias.reshape(B_HEADS, NA_ROWS, GRID_W, NA_KEYS)


def neighbourhood_attention(z, na_table):
    b, seq, _ = z.shape
    rows = seq // GRID_W
    assert rows >= NA_ROWS and rows % NA_UNROLL == 0
    hq = B_WIDTH // LANES
    base = 3 * A_WIDTH // LANES
    return pl.pallas_call(
        functools.partial(_na_kernel, rows=rows),
        grid=(b, B_HEADS),
        in_specs=[
            pl.BlockSpec((1, seq, LANES), lambda bi, h: (bi, 0, base + h)),
            pl.BlockSpec((1, seq, LANES), lambda bi, h: (bi, 0, base + hq + h)),
            pl.BlockSpec((1, seq, LANES), lambda bi, h: (bi, 0, base + 2 * hq + h)),
            pl.BlockSpec((1, NA_ROWS, GRID_W, NA_KEYS), lambda bi, h: (h, 0, 0, 0)),
        ],
        out_specs=pl.BlockSpec((1, seq, LANES), lambda bi, h: (bi, 0, h)),
        out_shape=jax.ShapeDtypeStruct((b, seq, B_WIDTH), BF16),
        compiler_params=_cparams("parallel", "parallel"),
        name="neighbourhood_attention",
    )(z, z, z, _na_bias(na_table))


HALO = 16
N_LAGS = 2 * POLY - 1


def _hyena_in_kernel(x_ref, xp_ref, xq_ref, g_ref, w0_ref, w1_ref, wv_ref,
                     c0_ref, c1_ref, cv_ref, b0_ref, b1_ref, bv_ref,
                     x0_out, vv_out, xn_ref, *, tiles_per_seq):
    i = pl.program_id(0)
    tm = x_ref.shape[0]
    hk = tm // POLY

    @pl.when(pl.program_id(1) == 0)
    def _():
        g = g_ref[...]
        xn = _rms(x_ref[...], g).astype(BF16)
        xn_ref[HALO:HALO + tm, :] = lax.dot_general(
            _phase_permutation(tm), xn, (((0,), (0,)), ((), ())),
            preferred_element_type=F32).astype(BF16)
        first = (i % tiles_per_seq) == 0
        last = (i % tiles_per_seq) == tiles_per_seq - 1
        xp = _rms(xp_ref[...], g)
        xq = _rms(xq_ref[...], g)
        xn_ref[0:HALO, :] = jnp.where(first, 0.0, xp).astype(BF16)
        xn_ref[HALO + tm:2 * HALO + tm, :] = jnp.where(last, 0.0, xq).astype(BF16)

    xn = xn_ref[...]
    row = lax.broadcasted_iota(jnp.int32, (hk, w0_ref.shape[1]), 0)

    def conv(w_ref, c_ref, b_ref):
        z = _dot(xn, w_ref[...])
        zp = [z[HALO + p * hk:HALO + (p + 1) * hk] for p in range(POLY)]
        before = jnp.where(row == 0, z[HALO - 1:HALO], pltpu.roll(zp[POLY - 1], 1, 0))
        after = jnp.where(row == hk - 1, z[HALO + tm:HALO + tm + 1], pltpu.roll(zp[0], hk - 1, 0))
        ext = [before] + zp + [after]
        c0, c1, c2, bias = c_ref[0:1, :], c_ref[1:2, :], c_ref[2:3, :], b_ref[...]
        return [ext[p] * c0 + ext[p + 1] * c1 + ext[p + 2] * c2 + bias for p in range(POLY)]

    x0 = conv(w0_ref, c0_ref, b0_ref)
    x1 = conv(w1_ref, c1_ref, b1_ref)
    v = conv(wv_ref, cv_ref, bv_ref)
    for p in range(POLY):
        x0_out[0, p] = x0[p].astype(x0_out.dtype)
        vv_out[0, p] = (v[p] * x1[p]).astype(vv_out.dtype)


def hyena_in(h, seq, g, w_in, conv_w, conv_b, tc=1024):
    n, d = h.shape
    tm = SPLIT_CHUNK
    nc = C_WIDTH // tc
    blocks = tm // HALO
    last_block = n // HALO - 1
    wspec = lambda o: pl.BlockSpec((d, tc), lambda i, c: (0, o * nc + c))
    cspec = lambda o: pl.BlockSpec((3, tc), lambda i, c: (0, o * nc + c))
    bspec = lambda o: pl.BlockSpec((1, tc), lambda i, c: (0, o * nc + c))
    cb = conv_b.reshape(1, -1)
    out = jax.ShapeDtypeStruct((n // tm, POLY, tm // POLY, C_WIDTH), BF16)
    ospec = pl.BlockSpec((1, POLY, tm // POLY, tc), lambda i, c: (i, 0, 0, c))
    return pl.pallas_call(
        functools.partial(_hyena_in_kernel, tiles_per_seq=seq // tm),
        grid=(n // tm, nc),
        in_specs=[
            pl.BlockSpec((tm, d), lambda i, c: (i, 0)),
            pl.BlockSpec((HALO, d), lambda i, c: (jnp.maximum(i * blocks - 1, 0), 0)),
            pl.BlockSpec((HALO, d), lambda i, c: (jnp.minimum((i + 1) * blocks, last_block), 0)),
            pl.BlockSpec((1, d), lambda i, c: (0, 0)),
            wspec(0), wspec(1), wspec(2), cspec(0), cspec(1), cspec(2),
            bspec(0), bspec(1), bspec(2),
        ],
        out_specs=[ospec, ospec],
        out_shape=[out, out],
        scratch_shapes=[pltpu.VMEM((tm + 2 * HALO, d), BF16)],
        compiler_params=_cparams("parallel", "arbitrary"),
        name="hyena_in",
    )(h, h, h, g.reshape(1, d), w_in, w_in, w_in, conv_w, conv_w, conv_w, cb, cb, cb)


def _filter_kernel(hd_ref, wf_ref, wb_ref, dl_ref, taps_ref, *, tl, seq):
    hd = hd_ref[...].astype(BF16)
    lane_group = lax.broadcasted_iota(jnp.int32, hd.shape, 1) // FILTER_HIDDEN
    zero = jnp.zeros_like(hd)
    m = pl.program_id(0) * tl + lax.broadcasted_iota(jnp.int32, taps_ref.shape[1:], 0)
    dl = dl_ref[...]

    def tap(w_ref, shift, r):
        token = (POLY * (m - 1 + shift) + r).astype(F32)
        x = jnp.where(lane_group == shift * POLY + r, hd, zero)
        return _dot(x, w_ref[...]) * jnp.exp(token * (-1.0 / (seq - 1)) * dl)

    for d in range(-POLY + 1, POLY):
        if d == 0:
            c = tap(wf_ref, 1, 0)
            a = jnp.where(m == 0, 0.0, tap(wb_ref, 1, 0))
            total, diff = c + a, c - a
        elif d > 0:
            c = tap(wf_ref, 1, d)
            a = tap(wb_ref, 0, POLY - d)
            total, diff = c + a, c - a
        else:
            c = tap(wf_ref, 0, POLY + d)
            a = tap(wb_ref, 1, -d)
            total, diff = c + a, c - jnp.where(m == 0, -a, a)
        taps_ref[2 * (d + POLY - 1)] = total.astype(taps_ref.dtype)
        taps_ref[2 * (d + POLY - 1) + 1] = diff.astype(taps_ref.dtype)


def hyena_filter_taps(hdn, w3, deltas, tl=512, tc=512):
    seq, hidden = hdn.shape
    sub = seq // POLY
    tl = min(tl, sub)
    nc = C_WIDTH // tc
    cur = hdn.reshape(sub, POLY * hidden)
    prev = jnp.concatenate([jnp.zeros_like(cur[:1]), cur[:-1]], axis=0)
    hd = jnp.concatenate([prev, cur], axis=1)
    w3s = jnp.tile(w3, (2 * POLY, 1))
    return pl.pallas_call(
        functools.partial(_filter_kernel, tl=tl, seq=seq),
        grid=(sub // tl, nc),
        in_specs=[
            pl.BlockSpec((tl, 2 * POLY * hidden), lambda l, c: (l, 0)),
            pl.BlockSpec((2 * POLY * hidden, tc), lambda l, c: (0, c)),
            pl.BlockSpec((2 * POLY * hidden, tc), lambda l, c: (0, nc + c)),
            pl.BlockSpec((1, tc), lambda l, c: (0, c)),
        ],
        out_specs=pl.BlockSpec((2 * N_LAGS, tl, tc), lambda l, c: (0, l, c)),
        out_shape=jax.ShapeDtypeStruct((2 * N_LAGS, sub, C_WIDTH), BF16),
        compiler_params=_cparams("parallel", "parallel"),
        name="hyena_filter_taps",
    )(hd, w3s, w3s, deltas.reshape(1, C_WIDTH))


NYQ_ROWS = 16


def _filter_dft_kernel(m_ref, taps_ref, coef_ref, nyq_ref, *, tf, sub):
    fi = pl.program_id(1)
    cos_m, sin_m = m_ref[0], m_ref[1]
    row0 = (fi * tf + lax.broadcasted_iota(jnp.int32, coef_ref.shape[1:], 0)) == 0
    wt = jnp.where(row0, 0.5 / sub, 1.0 / sub)
    for d in range(N_LAGS):
        coef_ref[2 * d] = _dot(cos_m, taps_ref[2 * d]) * wt
        coef_ref[2 * d + 1] = jnp.where(row0, 0.0, _dot(sin_m, taps_ref[2 * d + 1]) * wt)

    @pl.when(fi == 0)
    def _():
        for d in range(N_LAGS):
            nyq_ref[d] = _dot(sin_m[0:NYQ_ROWS, :], taps_ref[2 * d]) * (0.5 / sub)


def hyena_filter_dft(mmat, taps, tf=256, tc=512):
    sub = mmat.shape[1]
    tf = min(tf, sub)
    return pl.pallas_call(
        functools.partial(_filter_dft_kernel, tf=tf, sub=sub),
        grid=(C_WIDTH // tc, sub // tf),
        in_specs=[pl.BlockSpec((2, tf, sub), lambda c, f: (0, f, 0)),
                  pl.BlockSpec((2 * N_LAGS, sub, tc), lambda c, f: (0, 0, c))],
        out_specs=[pl.BlockSpec((2 * N_LAGS, tf, tc), lambda c, f: (0, f, c)),
                   pl.BlockSpec((N_LAGS, NYQ_ROWS, tc), lambda c, f: (0, 0, c))],
        out_shape=[jax.ShapeDtypeStruct((2 * N_LAGS, sub, C_WIDTH), F32),
                   jax.ShapeDtypeStruct((N_LAGS, NYQ_ROWS, C_WIDTH), F32)],
        compiler_params=_cparams("parallel", "arbitrary"),
        name="hyena_filter_dft",
    )(mmat, taps)


def _dft_fwd_kernel(m_ref, *refs, tf):
    v_refs, (coef_ref, nyq_ref, z_ref) = refs[:POLY], refs[POLY:]
    sub, tc = m_ref.shape[2], z_ref.shape[-1]
    m = m_ref[...].reshape(2 * tf, sub)
    spectra = [_dot(m, v_ref[0, :, 0].reshape(sub, tc)) for v_ref in v_refs]
    a_re = [a[:tf] for a in spectra]
    a_im = [a[tf:] for a in spectra]
    row0 = (pl.program_id(2) * tf + lax.broadcasted_iota(jnp.int32, (tf, tc), 0)) == 0
    h_re = [coef_ref[2 * d] for d in range(N_LAGS)]
    h_im = [coef_ref[2 * d + 1] for d in range(N_LAGS)]
    h_ny = [jnp.where(row0, nyq_ref[d, 0:1, :], h_re[d]) for d in range(N_LAGS)]
    for p in range(POLY):
        lag = [p - r + POLY - 1 for r in range(POLY)]
        z_re = sum(h_re[lag[r]] * a_re[r] - h_im[lag[r]] * a_im[r] for r in range(POLY))
        z_im = sum(h_ny[lag[r]] * a_im[r] + h_im[lag[r]] * a_re[r] for r in range(POLY))
        z_ref[0, p, 0] = z_re.astype(z_ref.dtype)
        z_ref[0, p, 1] = z_im.astype(z_ref.dtype)


def hyena_dft_fwd(mmat, vv, coefs, nyq, tf=256, tc=512):
    b, nck, _, hk, _ = vv.shape
    sub = nck * hk
    tf = min(tf, sub)
    v_spec = lambda p: pl.BlockSpec((1, nck, 1, hk, tc), lambda bi, c, f: (bi, 0, p, 0, c))
    return pl.pallas_call(
        functools.partial(_dft_fwd_kernel, tf=tf),
        grid=(b, C_WIDTH // tc, sub // tf),
        in_specs=[pl.BlockSpec((2, tf, sub), lambda bi, c, f: (0, f, 0))]
                 + [v_spec(p) for p in range(POLY)]
                 + [pl.BlockSpec((2 * N_LAGS, tf, tc), lambda bi, c, f: (0, f, c)),
                    pl.BlockSpec((N_LAGS, NYQ_ROWS, tc), lambda bi, c, f: (0, 0, c))],
        out_specs=pl.BlockSpec((1, POLY, 2, tf, tc), lambda bi, c, f: (bi, 0, 0, f, c)),
        out_shape=jax.ShapeDtypeStruct((b, POLY, 2, sub, C_WIDTH), BF16),
        compiler_params=_cparams("parallel", "parallel", "parallel"),
        name="hyena_dft_fwd",
    )(mmat, *([vv] * POLY), coefs, nyq)


def _dft_inv_kernel(g_ref, z_ref, vv_ref, x0_ref, skip_ref, u_ref):
    g = g_ref[...]
    nk, _, hk, tc = u_ref.shape[1:]
    for p in range(POLY):
        y = _dot(g, z_ref[0, p])
        vv = vv_ref[0, :, p].reshape(nk * hk, tc).astype(F32)
        x0 = x0_ref[0, :, p].reshape(nk * hk, tc).astype(F32)
        u = ((y + vv * skip_ref[...]) * x0).astype(u_ref.dtype)
        u_ref[0, :, p] = u.reshape(nk, hk, tc)


def hyena_dft_inv(gmat, zs, vv, x0, skip, tt=1024, tc=512):
    b, nck, _, hk, _ = vv.shape
    sub = nck * hk
    tt = min(tt, sub)
    io_spec = pl.BlockSpec((1, tt // hk, POLY, hk, tc), lambda bi, c, t: (bi, t, 0, 0, c))
    return pl.pallas_call(
        _dft_inv_kernel,
        grid=(b, C_WIDTH // tc, sub // tt),
        in_specs=[
            pl.BlockSpec((tt, 2 * sub), lambda bi, c, t: (t, 0)),
            pl.BlockSpec((1, POLY, 2 * sub, tc), lambda bi, c, t: (bi, 0, 0, c)),
            io_spec,
            io_spec,
            pl.BlockSpec((1, tc), lambda bi, c, t: (0, c)),
        ],
        out_specs=io_spec,
        out_shape=jax.ShapeDtypeStruct(vv.shape, BF16),
        compiler_params=_cparams("parallel", "parallel", "parallel"),
        name="hyena_dft_inv",
    )(gmat, zs, vv, x0, skip.reshape(1, C_WIDTH))


def _dft_matrices(sub):
    idx = jnp.arange(sub, dtype=jnp.int32)
    ang = ((idx[:, None] * idx[None, :]) % (2 * sub)).astype(F32) * (math.pi / sub)
    sin_part = (-jnp.sin(ang)).at[0].set((1 - 2 * (idx % 2)).astype(F32))
    mmat = jnp.stack([jnp.cos(ang), sin_part]).astype(BF16)
    return mmat, mmat.reshape(2 * sub, sub).T


def _filter_hidden(seq, w1, b1, freq, w2, b2):
    t = jnp.linspace(0.0, 1.0, seq, dtype=F32)[:, None]
    bands = (FILTER_EMB - 1) // 2
    w = 2.0 * math.pi * jnp.arange(seq, dtype=F32) / seq
    f = jnp.linspace(1e-4, bands - 1, bands, dtype=F32)
    ang = w[:, None] * f[None, :]
    feats = jnp.concatenate([t, jnp.cos(ang), -jnp.sin(ang)], axis=-1)
    hdn = jnp.sin(freq[0] * (feats @ w1 + b1))
    return jnp.sin(freq[1] * (hdn @ w2 + b2))


def hyena_mixer(h, batch, seq, g, w_in, conv_w, conv_b, w1, b1, freq, w2, b2, w3, skip, w_out):
    sub = seq // POLY
    split_shape = (batch, seq // SPLIT_CHUNK, POLY, SPLIT_CHUNK // POLY, C_WIDTH)
    x0, vv = (a.reshape(split_shape) for a in hyena_in(h, seq, g, w_in, conv_w, conv_b))
    hdn = _filter_hidden(seq, w1, b1, freq, w2, b2)
    min_decay = math.log(FILTER_TARGET) / FAST_DECAY_PCT
    max_decay = math.log(FILTER_TARGET) / SLOW_DECAY_PCT
    deltas = jnp.abs(jnp.linspace(min_decay, max_decay, C_WIDTH, dtype=F32))
    taps = hyena_filter_taps(hdn, w3, deltas)
    mmat, gmat = _dft_matrices(sub)
    coefs, nyq = hyena_filter_dft(mmat, taps)
    zs = hyena_dft_fwd(mmat, vv, coefs, nyq).reshape(batch, POLY, 2 * sub, C_WIDTH)
    u = hyena_dft_inv(gmat, zs, vv, x0, skip)
    u = u.reshape(batch * seq // SPLIT_CHUNK, POLY, SPLIT_CHUNK // POLY, C_WIDTH)
    return proj_residual([u], [w_out], h, interleave=True, tm=SPLIT_CHUNK)


def even_mixer(h, batch, seq, g, w_in, w_out, lam_vec, subln_g, na_table, rel_table, layer_idx):
    col_scale = np.ones((6, A_WIDTH), np.float32)
    col_scale[0] = A_QK_DIM ** -0.5 * LOG2E
    col_scale[3] = B_HEAD_DIM ** -0.5 * LOG2E
    z = norm_matmul(h, g, w_in, jnp.asarray(col_scale.reshape(-1)), BF16)
    z = z.reshape(batch, seq, w_in.shape[1])
    lam_init = 0.8 - 0.6 * math.exp(-0.3 * layer_idx)
    lv = lam_vec.astype(F32)
    lam = jnp.exp(jnp.sum(lv[0] * lv[1])) - jnp.exp(jnp.sum(lv[2] * lv[3])) + lam_init
    oa = diff_attention(z, lam, rel_table, subln_g, lam_init)
    ob = neighbourhood_attention(z, na_table)
    n = batch * seq
    return proj_residual([oa.reshape(n, A_WIDTH), ob.reshape(n, B_WIDTH)],
                         [w_out[:A_WIDTH], w_out[A_WIDTH:]], h)


def _trunk(x, p, W):
    batch, seq, d = x.shape
    depth = p.shape[0]
    h = x.reshape(batch * seq, d)
    for i in range(depth):
        j = i // 2
        if i % 2 == 0:
            h = even_mixer(h, batch, seq, W['norm_mix'][i], W['ab_w_in'][j], W['ab_w_out'][j],
                           W['diff_lambda'][j], W['diff_subln'][j], W['na_bias'][j],
                           W['rel_bias_table'], i)
        else:
            h = hyena_mixer(h, batch, seq, W['norm_mix'][i], W['c_w_in'][j], W['c_conv_w'][j],
                            W['c_conv_b'][j], W['c_filt_w1'][j], W['c_filt_b1'][j],
                            W['c_filt_freq'][j], W['c_filt_w2'][j], W['c_filt_b2'][j],
                            W['c_filt_w3'][j], W['c_skip'][j], W['c_w_out'][j])
        h = ffn(h, W['norm_ffn'][i], W['ffn_w_in'][i], W['ffn_w_out'][i])
        h = ple(h, p[i].reshape(batch * seq, -1), W['norm_ple'][i], W['ple_w_gate'][i],
                W['ple_w_proj'][i], W['final_norm'], final_norm=(i == depth - 1))
    return h.reshape(batch, seq, d)


def kernel(x_prompt, x_sample, p_prompt, p_sample, rel_bias_table, norm_mix, norm_ffn, norm_ple, final_norm, ab_w_in, ab_w_out, diff_lambda, diff_subln, na_bias, c_w_in, c_conv_w, c_conv_b, c_filt_w1, c_filt_b1, c_filt_freq, c_filt_w2, c_filt_b2, c_filt_w3, c_skip, c_w_out, ffn_w_in, ffn_w_out, ple_w_proj, ple_w_gate):
    bf = lambda w: w.astype(BF16)
    W = dict(rel_bias_table=rel_bias_table, norm_mix=norm_mix, norm_ffn=norm_ffn,
             norm_ple=norm_ple, final_norm=final_norm, ab_w_in=bf(ab_w_in), ab_w_out=bf(ab_w_out),
             diff_lambda=diff_lambda, diff_subln=diff_subln, na_bias=na_bias,
             c_w_in=bf(c_w_in), c_conv_w=c_conv_w, c_conv_b=c_conv_b, c_filt_w1=c_filt_w1,
             c_filt_b1=c_filt_b1, c_filt_freq=c_filt_freq, c_filt_w2=c_filt_w2,
             c_filt_b2=c_filt_b2, c_filt_w3=bf(c_filt_w3), c_skip=c_skip, c_w_out=bf(c_w_out),
             ffn_w_in=bf(ffn_w_in), ffn_w_out=bf(ffn_w_out), ple_w_proj=bf(ple_w_proj),
             ple_w_gate=bf(ple_w_gate))
    return (_trunk(x_prompt, p_prompt, W), _trunk(x_sample, p_sample, W))
```

```python
import functools
import math

import numpy as np
import jax
import jax.numpy as jnp
from jax import lax
from jax.experimental import pallas as pl
from jax.experimental.pallas import tpu as pltpu

F32 = jnp.float32
BF16 = jnp.bfloat16

D_MODEL = 2048
GRID_W = 64
A_HEADS = 8
A_QK_DIM = 64
A_V_DIM = 2 * A_QK_DIM
A_WIDTH = A_HEADS * A_V_DIM
T5_BUCKETS = 32
T5_MAX_DIST = 128
B_HEADS = 8
B_HEAD_DIM = 128
B_WIDTH = B_HEADS * B_HEAD_DIM
NA_ROWS = 8
NA_COLS = 16
C_WIDTH = D_MODEL
FILTER_EMB = 33
FILTER_HIDDEN = 64
FILTER_TARGET = 1e-2
FAST_DECAY_PCT = 0.3
SLOW_DECAY_PCT = 1.5
EPS = 1e-6

LANES = 128
NA_KEYS = NA_ROWS * GRID_W
MASK_VALUE = -1e30
LOG2E = math.log2(math.e)
POLY = 4
SPLIT_CHUNK = 512
NA_UNROLL = 32
VMEM_LIMIT = 56 * 1024 * 1024


def _cparams(*sem):
    return pltpu.CompilerParams(dimension_semantics=sem, vmem_limit_bytes=VMEM_LIMIT)


def _rms(x, g):
    ms = jnp.mean(x * x, axis=-1, keepdims=True)
    return x * lax.rsqrt(ms + EPS) * g


def _dot(a, b):
    return jnp.dot(a, b, preferred_element_type=F32)


def _dot_nt(a, b):
    return lax.dot_general(a, b, (((1,), (1,)), ((), ())), preferred_element_type=F32)


def _norm_matmul_kernel(x_ref, g_ref, w_ref, cs_ref, o_ref, xn_ref):
    @pl.when(pl.program_id(1) == 0)
    def _():
        xn_ref[...] = _rms(x_ref[...], g_ref[...]).astype(BF16)

    o_ref[...] = (_dot(xn_ref[...], w_ref[...]) * cs_ref[...]).astype(o_ref.dtype)


def norm_matmul(x, g, w, col_scale, out_dtype, tm=1024, tn=1024):
    n, d = x.shape
    d_out = w.shape[1]
    return pl.pallas_call(
        _norm_matmul_kernel,
        grid=(n // tm, d_out // tn),
        in_specs=[
            pl.BlockSpec((tm, d), lambda i, j: (i, 0)),
            pl.BlockSpec((1, d), lambda i, j: (0, 0)),
            pl.BlockSpec((d, tn), lambda i, j: (0, j)),
            pl.BlockSpec((1, tn), lambda i, j: (0, j)),
        ],
        out_specs=pl.BlockSpec((tm, tn), lambda i, j: (i, j)),
        out_shape=jax.ShapeDtypeStruct((n, d_out), out_dtype),
        scratch_shapes=[pltpu.VMEM((tm, d), BF16)],
        compiler_params=_cparams("parallel", "arbitrary"),
        name="norm_matmul",
    )(x, g.reshape(1, d), w, col_scale.reshape(1, d_out))


def _phase_permutation(n):
    t = lax.broadcasted_iota(jnp.int32, (n, n), 0)
    j = lax.broadcasted_iota(jnp.int32, (n, n), 1)
    return jnp.where(j == (t % POLY) * (n // POLY) + t // POLY, 1.0, 0.0).astype(BF16)


def _proj_res_kernel(*refs, n_in, interleave):
    a_refs = refs[:n_in]
    w_refs = refs[n_in:2 * n_in]
    res_ref, o_ref = refs[2 * n_in], refs[2 * n_in + 1]
    acc = res_ref[...]
    for a_ref, w_ref in zip(a_refs, w_refs):
        if interleave:
            n = POLY * a_ref.shape[2]
            a = _dot(_phase_permutation(n), a_ref[0].reshape(n, a_ref.shape[3])).astype(BF16)
        else:
            a = a_ref[...]
        acc = acc + _dot(a, w_ref[...])
    o_ref[...] = acc


def proj_residual(a_list, w_list, res, interleave=False, tm=512):
    n, d = res.shape
    n_in = len(a_list)
    if interleave:
        in_specs = [pl.BlockSpec((1, POLY, tm // POLY, a.shape[-1]), lambda i: (i, 0, 0, 0))
                    for a in a_list]
    else:
        in_specs = [pl.BlockSpec((tm, a.shape[1]), lambda i: (i, 0)) for a in a_list]
    in_specs += [pl.BlockSpec(w.shape, lambda i: (0, 0)) for w in w_list]
    in_specs += [pl.BlockSpec((tm, d), lambda i: (i, 0))]
    return pl.pallas_call(
        functools.partial(_proj_res_kernel, n_in=n_in, interleave=interleave),
        grid=(n // tm,),
        in_specs=in_specs,
        out_specs=pl.BlockSpec((tm, d), lambda i: (i, 0)),
        out_shape=jax.ShapeDtypeStruct((n, d), F32),
        compiler_params=_cparams("parallel"),
        name="proj_residual",
    )(*a_list, *w_list, res)


def _ffn_kernel(x_ref, g_ref, wg_ref, wu_ref, wo_ref, o_ref, xn_ref):
    @pl.when(pl.program_id(1) == 0)
    def _():
        x = x_ref[...]
        xn_ref[...] = _rms(x, g_ref[...]).astype(BF16)
        o_ref[...] = x

    xn = xn_ref[...]
    gate = _dot(xn, wg_ref[...])
    up = _dot(xn, wu_ref[...])
    act = (gate * jax.nn.sigmoid(gate) * up).astype(BF16)
    o_ref[...] += _dot(act, wo_ref[...])


def ffn(x, g, w_in, w_out, tm=1024, tf=512):
    n, d = x.shape
    d_ff = w_out.shape[0]
    nf = d_ff // tf
    return pl.pallas_call(
        _ffn_kernel,
        grid=(n // tm, nf),
        in_specs=[
            pl.BlockSpec((tm, d), lambda i, f: (i, 0)),
            pl.BlockSpec((1, d), lambda i, f: (0, 0)),
            pl.BlockSpec((d, tf), lambda i, f: (0, f)),
            pl.BlockSpec((d, tf), lambda i, f: (0, nf + f)),
            pl.BlockSpec((tf, d), lambda i, f: (f, 0)),
        ],
        out_specs=pl.BlockSpec((tm, d), lambda i, f: (i, 0)),
        out_shape=jax.ShapeDtypeStruct((n, d), F32),
        scratch_shapes=[pltpu.VMEM((tm, d), BF16)],
        compiler_params=_cparams("parallel", "arbitrary"),
        name="ffn",
    )(x, g.reshape(1, d), w_in, w_in, w_out)


def _ple_kernel(x_ref, p_ref, g_ref, wg_ref, wp_ref, gf_ref, o_ref, *, final_norm):
    x = x_ref[...]
    gate = jax.nn.sigmoid(_dot(_rms(x, g_ref[...]).astype(BF16), wg_ref[...]))
    proj = _dot(p_ref[...].astype(BF16), wp_ref[...])
    y = x + proj * gate
    if final_norm:
        y = _rms(y, gf_ref[...])
    o_ref[...] = y


def ple(x, p, g, w_gate, w_proj, g_final, final_norm, tm=512):
    n, d = x.shape
    dp = p.shape[1]
    return pl.pallas_call(
        functools.partial(_ple_kernel, final_norm=final_norm),
        grid=(n // tm,),
        in_specs=[
            pl.BlockSpec((tm, d), lambda i: (i, 0)),
            pl.BlockSpec((tm, dp), lambda i: (i, 0)),
            pl.BlockSpec((1, d), lambda i: (0, 0)),
            pl.BlockSpec((d, d), lambda i: (0, 0)),
            pl.BlockSpec((dp, d), lambda i: (0, 0)),
            pl.BlockSpec((1, d), lambda i: (0, 0)),
        ],
        out_specs=pl.BlockSpec((tm, d), lambda i: (i, 0)),
        out_shape=jax.ShapeDtypeStruct((n, d), F32),
        compiler_params=_cparams("parallel"),
        name="ple",
    )(x, p, g.reshape(1, d), w_gate, w_proj, g_final.reshape(1, d))


def t5_bucket(rel):
    nb = T5_BUCKETS // 2
    max_exact = nb // 2
    ret = jnp.where(rel > 0, nb, 0)
    n = jnp.abs(rel)
    nf = jnp.maximum(n, 1).astype(jnp.float32)
    large = max_exact + (jnp.log(nf / max_exact) / math.log(T5_MAX_DIST / max_exact)
                         * (nb - max_exact)).astype(jnp.int32)
    large = jnp.minimum(large, nb - 1)
    return ret + jnp.where(n < max_exact, n, large)


def _lane_tiles(x):
    return [x[:, j * LANES:(j + 1) * LANES] for j in range(x.shape[1] // LANES)]


def _diff_attn_kernel(lam_ref, q_ref, k_ref, v_ref, e_ref, g_ref, o_ref,
                      s1_ref, s2_ref, p1_ref, p2_ref, va_ref, *, tq, tk, seq, pad, width,
                      out_scale):
    i = pl.program_id(2)

    @pl.when(i == 0)
    def _():
        va_ref[:, :A_V_DIM] = v_ref[0]
        va_ref[:, A_V_DIM:] = jnp.ones((seq, A_V_DIM), BF16)

    q = q_ref[0]
    lane = lax.broadcasted_iota(jnp.int32, q.shape, 1)
    zero = jnp.zeros_like(q)
    q1 = jnp.where(lane < A_QK_DIM, q, zero)
    q2 = jnp.where(lane < A_QK_DIM, zero, q)

    def bias_tile(j):
        c = j * LANES // tk
        start = jnp.clip(c * tk - i * tq + pad, 0, width - tk) + (j * LANES - c * tk)
        return e_ref[0, :, pl.ds(pl.multiple_of(start, LANES), LANES)]

    m1 = m2 = None
    for c in range(seq // tk):
        cols = slice(c * tk, (c + 1) * tk)
        kc = k_ref[0, cols, :]
        x1 = _dot_nt(q1, kc)
        x2 = _dot_nt(q2, kc)
        s1_ref[:, cols] = x1
        s2_ref[:, cols] = x2
        for jj, (t1, t2) in enumerate(zip(_lane_tiles(x1), _lane_tiles(x2))):
            bias = bias_tile(c * (tk // LANES) + jj)
            m1 = t1 + bias if m1 is None else jnp.maximum(m1, t1 + bias)
            m2 = t2 + bias if m2 is None else jnp.maximum(m2, t2 + bias)
    m1 = jnp.broadcast_to(jnp.max(m1, axis=-1, keepdims=True), (tq, LANES))
    m2 = jnp.broadcast_to(jnp.max(m2, axis=-1, keepdims=True), (tq, LANES))
    for j in range(seq // LANES):
        cols = slice(j * LANES, (j + 1) * LANES)
        shift = bias_tile(j)
        p1_ref[:, cols] = jnp.exp2(s1_ref[:, cols] + (shift - m1)).astype(BF16)
        p2_ref[:, cols] = jnp.exp2(s2_ref[:, cols] + (shift - m2)).astype(BF16)
    o1 = _dot(p1_ref[...], va_ref[...])
    o2 = _dot(p2_ref[...], va_ref[...])
    o = (o1[:, :A_V_DIM] * (1.0 / o1[:, A_V_DIM:])
         - o2[:, :A_V_DIM] * (lam_ref[0] / o2[:, A_V_DIM:]))
    o = _rms(o, g_ref[...]) * out_scale
    o_ref[0] = o.astype(o_ref.dtype)


def _band_kernel(w_ref, o_ref, *, rows, cols):
    n = rows + cols
    x = jnp.broadcast_to(w_ref[0], (rows, n))
    y = pltpu.roll(x, n - (rows - 1), 1, stride=1, stride_axis=0)
    o_ref[0] = y[:, :cols]


def _toeplitz(w, rows, cols):
    heads, n = w.shape
    assert n == rows + cols and n % LANES == 0
    return pl.pallas_call(
        functools.partial(_band_kernel, rows=rows, cols=cols),
        grid=(heads,),
        in_specs=[pl.BlockSpec((1, 1, n), lambda h: (h, 0, 0))],
        out_specs=pl.BlockSpec((1, rows, cols), lambda h: (h, 0, 0)),
        out_shape=jax.ShapeDtypeStruct((heads, rows, cols), F32),
        compiler_params=_cparams("parallel"),
        name="toeplitz_band",
    )(w.reshape(heads, 1, n))


def diff_attention(z, lam, rel_table, subln_g, lam_init, tq=512, tk=512):
    b, seq, _ = z.shape
    pad = tk + T5_MAX_DIST
    width = 2 * tk + tq + 2 * T5_MAX_DIST
    rel = jnp.arange(tq + width, dtype=jnp.int32) - (tq - 1) - pad
    onehot = (t5_bucket(rel)[:, None] == jnp.arange(T5_BUCKETS)[None, :]).astype(F32)
    diag = jnp.dot(onehot, rel_table.astype(F32), precision=lax.Precision.HIGHEST)
    band = _toeplitz(diag.T * LOG2E, tq, width)
    hq = A_WIDTH // LANES
    return pl.pallas_call(
        functools.partial(_diff_attn_kernel, tq=tq, tk=tk, seq=seq, pad=pad, width=width,
                          out_scale=1.0 - lam_init),
        grid=(b, A_HEADS, seq // tq),
        in_specs=[
            pl.BlockSpec(memory_space=pltpu.SMEM),
            pl.BlockSpec((1, tq, LANES), lambda bi, h, i: (bi, i, h)),
            pl.BlockSpec((1, seq, LANES), lambda bi, h, i: (bi, 0, hq + h)),
            pl.BlockSpec((1, seq, LANES), lambda bi, h, i: (bi, 0, 2 * hq + h)),
            pl.BlockSpec((1, tq, width), lambda bi, h, i: (h, 0, 0)),
            pl.BlockSpec((1, A_V_DIM), lambda bi, h, i: (0, 0)),
        ],
        out_specs=pl.BlockSpec((1, tq, LANES), lambda bi, h, i: (bi, i, h)),
        out_shape=jax.ShapeDtypeStruct((b, seq, A_WIDTH), BF16),
        scratch_shapes=[pltpu.VMEM((tq, seq), F32), pltpu.VMEM((tq, seq), F32),
                        pltpu.VMEM((tq, seq), BF16), pltpu.VMEM((tq, seq), BF16),
                        pltpu.VMEM((seq, 2 * A_V_DIM), BF16)],
        compiler_params=_cparams("parallel", "parallel", "arbitrary"),
        name="diff_attention",
    )(lam.reshape(1).astype(F32), z, z, z, band, subln_g.reshape(1, A_V_DIM))


def _na_kernel(q_ref, k_ref, v_ref, nb_ref, o_ref, *, rows):
    def row_group(gi, carry):
        geom = []
        for u in range(NA_UNROLL):
            r = gi * NA_UNROLL + u
            r_start = jnp.clip(r - NA_ROWS // 2, 0, rows - NA_ROWS)
            geom.append((pl.multiple_of(r * GRID_W, GRID_W),
                         pl.multiple_of(r_start * GRID_W, GRID_W), r - r_start))
        scores = [_dot_nt(q_ref[0, pl.ds(q0, GRID_W), :], k_ref[0, pl.ds(k0, NA_KEYS), :])
                  + nb_ref[0, var] for q0, k0, var in geom]
        probs = []
        for s in scores:
            p = jnp.exp2(s - jnp.max(s, axis=-1, keepdims=True))
            probs.append((p * (1.0 / jnp.sum(p, axis=-1, keepdims=True))).astype(BF16))
        for p, (q0, k0, _) in zip(probs, geom):
            o = _dot(p, v_ref[0, pl.ds(k0, NA_KEYS), :])
            o_ref[0, pl.ds(q0, GRID_W), :] = o.astype(o_ref.dtype)
        return carry

    lax.fori_loop(0, rows // NA_UNROLL, row_group, 0)


def _na_bias(na_table):
    var = np.arange(NA_ROWS)[:, None, None]
    kr = np.arange(NA_ROWS)[None, :, None]
    row_sel = (kr - var + (NA_ROWS - 1) == np.arange(2 * NA_ROWS - 1)).astype(np.float32)
    c = np.arange(GRID_W)[:, None, None]
    kc = np.arange(GRID_W)[None, :, None]
    col_sel = (kc - c + (NA_COLS - 1) == np.arange(2 * NA_COLS - 1)).astype(np.float32)
    c_start = np.clip(c - NA_COLS // 2, 0, GRID_W - NA_COLS)
    valid = ((kc >= c_start) & (kc < c_start + NA_COLS))[None, None, :, None, :, 0]
    bias = jnp.einsum('vka,hab,cqb->hvckq', row_sel, na_table.astype(F32), col_sel,
                      precision=lax.Precision.HIGHEST)
    bias = jnp.where(valid, bias * LOG2E, MASK_VALUE)
    return bias.reshape(B_HEADS, NA_ROWS, GRID_W, NA_KEYS)


def neighbourhood_attention(z, na_table):
    b, seq, _ = z.shape
    rows = seq // GRID_W
    assert rows >= NA_ROWS and rows % NA_UNROLL == 0
    hq = B_WIDTH // LANES
    base = 3 * A_WIDTH // LANES
    return pl.pallas_call(
        functools.partial(_na_kernel, rows=rows),
        grid=(b, B_HEADS),
        in_specs=[
            pl.BlockSpec((1, seq, LANES), lambda bi, h: (bi, 0, base + h)),
            pl.BlockSpec((1, seq, LANES), lambda bi, h: (bi, 0, base + hq + h)),
            pl.BlockSpec((1, seq, LANES), lambda bi, h: (bi, 0, base + 2 * hq + h)),
            pl.BlockSpec((1, NA_ROWS, GRID_W, NA_KEYS), lambda bi, h: (h, 0, 0, 0)),
        ],
        out_specs=pl.BlockSpec((1, seq, LANES), lambda bi, h: (bi, 0, h)),
        out_shape=jax.ShapeDtypeStruct((b, seq, B_WIDTH), BF16),
        compiler_params=_cparams("parallel", "parallel"),
        name="neighbourhood_attention",
    )(z, z, z, _na_bias(na_table))


HALO = 16
N_LAGS = 2 * POLY - 1


def _hyena_in_kernel(x_ref, xp_ref, xq_ref, g_ref, w0_ref, w1_ref, wv_ref,
                     c0_ref, c1_ref, cv_ref, b0_ref, b1_ref, bv_ref,
                     x0_out, vv_out, xn_ref, *, tiles_per_seq):
    i = pl.program_id(0)
    tm = x_ref.shape[0]
    hk = tm // POLY

    @pl.when(pl.program_id(1) == 0)
    def _():
        g = g_ref[...]
        xn = _rms(x_ref[...], g).astype(BF16)
        xn_ref[HALO:HALO + tm, :] = lax.dot_general(
            _phase_permutation(tm), xn, (((0,), (0,)), ((), ())),
            preferred_element_type=F32).astype(BF16)
        first = (i % tiles_per_seq) == 0
        last = (i % tiles_per_seq) == tiles_per_seq - 1
        xp = _rms(xp_ref[...], g)
        xq = _rms(xq_ref[...], g)
        xn_ref[0:HALO, :] = jnp.where(first, 0.0, xp).astype(BF16)
        xn_ref[HALO + tm:2 * HALO + tm, :] = jnp.where(last, 0.0, xq).astype(BF16)

    xn = xn_ref[...]
    row = lax.broadcasted_iota(jnp.int32, (hk, w0_ref.shape[1]), 0)

    def conv(w_ref, c_ref, b_ref):
        z = _dot(xn, w_ref[...])
        zp = [z[HALO + p * hk:HALO + (p + 1) * hk] for p in range(POLY)]
        before = jnp.where(row == 0, z[HALO - 1:HALO], pltpu.roll(zp[POLY - 1], 1, 0))
        after = jnp.where(row == hk - 1, z[HALO + tm:HALO + tm + 1], pltpu.roll(zp[0], hk - 1, 0))
        ext = [before] + zp + [after]
        c0, c1, c2, bias = c_ref[0:1, :], c_ref[1:2, :], c_ref[2:3, :], b_ref[...]
        return [ext[p] * c0 + ext[p + 1] * c1 + ext[p + 2] * c2 + bias for p in range(POLY)]

    x0 = conv(w0_ref, c0_ref, b0_ref)
    x1 = conv(w1_ref, c1_ref, b1_ref)
    v = conv(wv_ref, cv_ref, bv_ref)
    for p in range(POLY):
        x0_out[0, p] = x0[p].astype(x0_out.dtype)
        vv_out[0, p] = (v[p] * x1[p]).astype(vv_out.dtype)


def hyena_in(h, seq, g, w_in, conv_w, conv_b, tc=1024):
    n, d = h.shape
    tm = SPLIT_CHUNK
    nc = C_WIDTH // tc
    blocks = tm // HALO
    last_block = n // HALO - 1
    wspec = lambda o: pl.BlockSpec((d, tc), lambda i, c: (0, o * nc + c))
    cspec = lambda o: pl.BlockSpec((3, tc), lambda i, c: (0, o * nc + c))
    bspec = lambda o: pl.BlockSpec((1, tc), lambda i, c: (0, o * nc + c))
    cb = conv_b.reshape(1, -1)
    out = jax.ShapeDtypeStruct((n // tm, POLY, tm // POLY, C_WIDTH), BF16)
    ospec = pl.BlockSpec((1, POLY, tm // POLY, tc), lambda i, c: (i, 0, 0, c))
    return pl.pallas_call(
        functools.partial(_hyena_in_kernel, tiles_per_seq=seq // tm),
        grid=(n // tm, nc),
        in_specs=[
            pl.BlockSpec((tm, d), lambda i, c: (i, 0)),
            pl.BlockSpec((HALO, d), lambda i, c: (jnp.maximum(i * blocks - 1, 0), 0)),
            pl.BlockSpec((HALO, d), lambda i, c: (jnp.minimum((i + 1) * blocks, last_block), 0)),
            pl.BlockSpec((1, d), lambda i, c: (0, 0)),
            wspec(0), wspec(1), wspec(2), cspec(0), cspec(1), cspec(2),
            bspec(0), bspec(1), bspec(2),
        ],
        out_specs=[ospec, ospec],
        out_shape=[out, out],
        scratch_shapes=[pltpu.VMEM((tm + 2 * HALO, d), BF16)],
        compiler_params=_cparams("parallel", "arbitrary"),
        name="hyena_in",
    )(h, h, h, g.reshape(1, d), w_in, w_in, w_in, conv_w, conv_w, conv_w, cb, cb, cb)


def _filter_kernel(hd_ref, wf_ref, wb_ref, dl_ref, taps_ref, *, tl, seq):
    hd = hd_ref[...].astype(BF16)
    lane_group = lax.broadcasted_iota(jnp.int32, hd.shape, 1) // FILTER_HIDDEN
    zero = jnp.zeros_like(hd)
    m = pl.program_id(0) * tl + lax.broadcasted_iota(jnp.int32, taps_ref.shape[1:], 0)
    dl = dl_ref[...]

    def tap(w_ref, shift, r):
        token = (POLY * (m - 1 + shift) + r).astype(F32)
        x = jnp.where(lane_group == shift * POLY + r, hd, zero)
        return _dot(x, w_ref[...]) * jnp.exp(token * (-1.0 / (seq - 1)) * dl)

    for d in range(-POLY + 1, POLY):
        if d == 0:
            c = tap(wf_ref, 1, 0)
            a = jnp.where(m == 0, 0.0, tap(wb_ref, 1, 0))
            total, diff = c + a, c - a
        elif d > 0:
            c = tap(wf_ref, 1, d)
            a = tap(wb_ref, 0, POLY - d)
            total, diff = c + a, c - a
        else:
            c = tap(wf_ref, 0, POLY + d)
            a = tap(wb_ref, 1, -d)
            total, diff = c + a, c - jnp.where(m == 0, -a, a)
        taps_ref[2 * (d + POLY - 1)] = total.astype(taps_ref.dtype)
        taps_ref[2 * (d + POLY - 1) + 1] = diff.astype(taps_ref.dtype)


def hyena_filter_taps(hdn, w3, deltas, tl=512, tc=512):
    seq, hidden = hdn.shape
    sub = seq // POLY
    tl = min(tl, sub)
    nc = C_WIDTH // tc
    cur = hdn.reshape(sub, POLY * hidden)
    prev = jnp.concatenate([jnp.zeros_like(cur[:1]), cur[:-1]], axis=0)
    hd = jnp.concatenate([prev, cur], axis=1)
    w3s = jnp.tile(w3, (2 * POLY, 1))
    return pl.pallas_call(
        functools.partial(_filter_kernel, tl=tl, seq=seq),
        grid=(sub // tl, nc),
        in_specs=[
            pl.BlockSpec((tl, 2 * POLY * hidden), lambda l, c: (l, 0)),
            pl.BlockSpec((2 * POLY * hidden, tc), lambda l, c: (0, c)),
            pl.BlockSpec((2 * POLY * hidden, tc), lambda l, c: (0, nc + c)),
            pl.BlockSpec((1, tc), lambda l, c: (0, c)),
        ],
        out_specs=pl.BlockSpec((2 * N_LAGS, tl, tc), lambda l, c: (0, l, c)),
        out_shape=jax.ShapeDtypeStruct((2 * N_LAGS, sub, C_WIDTH), BF16),
        compiler_params=_cparams("parallel", "parallel"),
        name="hyena_filter_taps",
    )(hd, w3s, w3s, deltas.reshape(1, C_WIDTH))


NYQ_ROWS = 16


def _filter_dft_kernel(m_ref, taps_ref, coef_ref, nyq_ref, *, tf, sub):
    fi = pl.program_id(1)
    cos_m, sin_m = m_ref[0], m_ref[1]
    row0 = (fi * tf + lax.broadcasted_iota(jnp.int32, coef_ref.shape[1:], 0)) == 0
    wt = jnp.where(row0, 0.5 / sub, 1.0 / sub)
    for d in range(N_LAGS):
        coef_ref[2 * d] = _dot(cos_m, taps_ref[2 * d]) * wt
        coef_ref[2 * d + 1] = jnp.where(row0, 0.0, _dot(sin_m, taps_ref[2 * d + 1]) * wt)

    @pl.when(fi == 0)
    def _():
        for d in range(N_LAGS):
            nyq_ref[d] = _dot(sin_m[0:NYQ_ROWS, :], taps_ref[2 * d]) * (0.5 / sub)


def hyena_filter_dft(mmat, taps, tf=256, tc=512):
    sub = mmat.shape[1]
    tf = min(tf, sub)
    return pl.pallas_call(
        functools.partial(_filter_dft_kernel, tf=tf, sub=sub),
        grid=(C_WIDTH // tc, sub // tf),
        in_specs=[pl.BlockSpec((2, tf, sub), lambda c, f: (0, f, 0)),
                  pl.BlockSpec((2 * N_LAGS, sub, tc), lambda c, f: (0, 0, c))],
        out_specs=[pl.BlockSpec((2 * N_LAGS, tf, tc), lambda c, f: (0, f, c)),
                   pl.BlockSpec((N_LAGS, NYQ_ROWS, tc), lambda c, f: (0, 0, c))],
        out_shape=[jax.ShapeDtypeStruct((2 * N_LAGS, sub, C_WIDTH), F32),
                   jax.ShapeDtypeStruct((N_LAGS, NYQ_ROWS, C_WIDTH), F32)],
        compiler_params=_cparams("parallel", "arbitrary"),
        name="hyena_filter_dft",
    )(mmat, taps)


def _dft_fwd_kernel(m_ref, *refs, tf):
    v_refs, (coef_ref, nyq_ref, z_ref) = refs[:POLY], refs[POLY:]
    sub, tc = m_ref.shape[2], z_ref.shape[-1]
    m = m_ref[...].reshape(2 * tf, sub)
    spectra = [_dot(m, v_ref[0, :, 0].reshape(sub, tc)) for v_ref in v_refs]
    a_re = [a[:tf] for a in spectra]
    a_im = [a[tf:] for a in spectra]
    row0 = (pl.program_id(2) * tf + lax.broadcasted_iota(jnp.int32, (tf, tc), 0)) == 0
    h_re = [coef_ref[2 * d] for d in range(N_LAGS)]
    h_im = [coef_ref[2 * d + 1] for d in range(N_LAGS)]
    h_ny = [jnp.where(row0, nyq_ref[d, 0:1, :], h_re[d]) for d in range(N_LAGS)]
    for p in range(POLY):
        lag = [p - r + POLY - 1 for r in range(POLY)]
        z_re = sum(h_re[lag[r]] * a_re[r] - h_im[lag[r]] * a_im[r] for r in range(POLY))
        z_im = sum(h_ny[lag[r]] * a_im[r] + h_im[lag[r]] * a_re[r] for r in range(POLY))
        z_ref[0, p, 0] = z_re.astype(z_ref.dtype)
        z_ref[0, p, 1] = z_im.astype(z_ref.dtype)


def hyena_dft_fwd(mmat, vv, coefs, nyq, tf=256, tc=512):
    b, nck, _, hk, _ = vv.shape
    sub = nck * hk
    tf = min(tf, sub)
    v_spec = lambda p: pl.BlockSpec((1, nck, 1, hk, tc), lambda bi, c, f: (bi, 0, p, 0, c))
    return pl.pallas_call(
        functools.partial(_dft_fwd_kernel, tf=tf),
        grid=(b, C_WIDTH // tc, sub // tf),
        in_specs=[pl.BlockSpec((2, tf, sub), lambda bi, c, f: (0, f, 0))]
                 + [v_spec(p) for p in range(POLY)]
                 + [pl.BlockSpec((2 * N_LAGS, tf, tc), lambda bi, c, f: (0, f, c)),
                    pl.BlockSpec((N_LAGS, NYQ_ROWS, tc), lambda bi, c, f: (0, 0, c))],
        out_specs=pl.BlockSpec((1, POLY, 2, tf, tc), lambda bi, c, f: (bi, 0, 0, f, c)),
        out_shape=jax.ShapeDtypeStruct((b, POLY, 2, sub, C_WIDTH), BF16),
        compiler_params=_cparams("parallel", "parallel", "parallel"),
        name="hyena_dft_fwd",
    )(mmat, *([vv] * POLY), coefs, nyq)


def _dft_inv_kernel(g_ref, z_ref, vv_ref, x0_ref, skip_ref, u_ref):
    g = g_ref[...]
    nk, _, hk, tc = u_ref.shape[1:]
    for p in range(POLY):
        y = _dot(g, z_ref[0, p])
        vv = vv_ref[0, :, p].reshape(nk * hk, tc).astype(F32)
        x0 = x0_ref[0, :, p].reshape(nk * hk, tc).astype(F32)
        u = ((y + vv * skip_ref[...]) * x0).astype(u_ref.dtype)
        u_ref[0, :, p] = u.reshape(nk, hk, tc)


def hyena_dft_inv(gmat, zs, vv, x0, skip, tt=1024, tc=512):
    b, nck, _, hk, _ = vv.shape
    sub = nck * hk
    tt = min(tt, sub)
    io_spec = pl.BlockSpec((1, tt // hk, POLY, hk, tc), lambda bi, c, t: (bi, t, 0, 0, c))
    return pl.pallas_call(
        _dft_inv_kernel,
        grid=(b, C_WIDTH // tc, sub // tt),
        in_specs=[
            pl.BlockSpec((tt, 2 * sub), lambda bi, c, t: (t, 0)),
            pl.BlockSpec((1, POLY, 2 * sub, tc), lambda bi, c, t: (bi, 0, 0, c)),
            io_spec,
            io_spec,
            pl.BlockSpec((1, tc), lambda bi, c, t: (0, c)),
        ],
        out_specs=io_spec,
        out_shape=jax.ShapeDtypeStruct(vv.shape, BF16),
        compiler_params=_cparams("parallel", "parallel", "parallel"),
        name="hyena_dft_inv",
    )(gmat, zs, vv, x0, skip.reshape(1, C_WIDTH))


def _dft_matrices(sub):
    idx = jnp.arange(sub, dtype=jnp.int32)
    ang = ((idx[:, None] * idx[None, :]) % (2 * sub)).astype(F32) * (math.pi / sub)
    sin_part = (-jnp.sin(ang)).at[0].set((1 - 2 * (idx % 2)).astype(F32))
    mmat = jnp.stack([jnp.cos(ang), sin_part]).astype(BF16)
    return mmat, mmat.reshape(2 * sub, sub).T


def _filter_hidden(seq, w1, b1, freq, w2, b2):
    t = jnp.linspace(0.0, 1.0, seq, dtype=F32)[:, None]
    bands = (FILTER_EMB - 1) // 2
    w = 2.0 * math.pi * jnp.arange(seq, dtype=F32) / seq
    f = jnp.linspace(1e-4, bands - 1, bands, dtype=F32)
    ang = w[:, None] * f[None, :]
    feats = jnp.concatenate([t, jnp.cos(ang), -jnp.sin(ang)], axis=-1)
    hdn = jnp.sin(freq[0] * (feats @ w1 + b1))
    return jnp.sin(freq[1] * (hdn @ w2 + b2))


def hyena_mixer(h, batch, seq, g, w_in, conv_w, conv_b, w1, b1, freq, w2, b2, w3, skip, w_out):
    sub = seq // POLY
    split_shape = (batch, seq // SPLIT_CHUNK, POLY, SPLIT_CHUNK // POLY, C_WIDTH)
    x0, vv = (a.reshape(split_shape) for a in hyena_in(h, seq, g, w_in, conv_w, conv_b))
    hdn = _filter_hidden(seq, w1, b1, freq, w2, b2)
    min_decay = math.log(FILTER_TARGET) / FAST_DECAY_PCT
    max_decay = math.log(FILTER_TARGET) / SLOW_DECAY_PCT
    deltas = jnp.abs(jnp.linspace(min_decay, max_decay, C_WIDTH, dtype=F32))
    taps = hyena_filter_taps(hdn, w3, deltas)
    mmat, gmat = _dft_matrices(sub)
    coefs, nyq = hyena_filter_dft(mmat, taps)
    zs = hyena_dft_fwd(mmat, vv, coefs, nyq).reshape(batch, POLY, 2 * sub, C_WIDTH)
    u = hyena_dft_inv(gmat, zs, vv, x0, skip)
    u = u.reshape(batch * seq // SPLIT_CHUNK, POLY, SPLIT_CHUNK // POLY, C_WIDTH)
    return proj_residual([u], [w_out], h, interleave=True, tm=SPLIT_CHUNK)


def even_mixer(h, batch, seq, g, w_in, w_out, lam_vec, subln_g, na_table, rel_table, layer_idx):
    col_scale = np.ones((6, A_WIDTH), np.float32)
    col_scale[0] = A_QK_DIM ** -0.5 * LOG2E
    col_scale[3] = B_HEAD_DIM ** -0.5 * LOG2E
    z = norm_matmul(h, g, w_in, jnp.asarray(col_scale.reshape(-1)), BF16)
    z = z.reshape(batch, seq, w_in.shape[1])
    lam_init = 0.8 - 0.6 * math.exp(-0.3 * layer_idx)
    lv = lam_vec.astype(F32)
    lam = jnp.exp(jnp.sum(lv[0] * lv[1])) - jnp.exp(jnp.sum(lv[2] * lv[3])) + lam_init
    oa = diff_attention(z, lam, rel_table, subln_g, lam_init)
    ob = neighbourhood_attention(z, na_table)
    n = batch * seq
    return proj_residual([oa.reshape(n, A_WIDTH), ob.reshape(n, B_WIDTH)],
                         [w_out[:A_WIDTH], w_out[A_WIDTH:]], h)


def _trunk(x, p, W):
    batch, seq, d = x.shape
    depth = p.shape[0]
    h = x.reshape(batch * seq, d)
    for i in range(depth):
        j = i // 2
        if i % 2 == 0:
            h = even_mixer(h, batch, seq, W['norm_mix'][i], W['ab_w_in'][j], W['ab_w_out'][j],
                           W['diff_lambda'][j], W['diff_subln'][j], W['na_bias'][j],
                           W['rel_bias_table'], i)
        else:
            h = hyena_mixer(h, batch, seq, W['norm_mix'][i], W['c_w_in'][j], W['c_conv_w'][j],
                            W['c_conv_b'][j], W['c_filt_w1'][j], W['c_filt_b1'][j],
                            W['c_filt_freq'][j], W['c_filt_w2'][j], W['c_filt_b2'][j],
                            W['c_filt_w3'][j], W['c_skip'][j], W['c_w_out'][j])
        h = ffn(h, W['norm_ffn'][i], W['ffn_w_in'][i], W['ffn_w_out'][i])
        h = ple(h, p[i].reshape(batch * seq, -1), W['norm_ple'][i], W['ple_w_gate'][i],
                W['ple_w_proj'][i], W['final_norm'], final_norm=(i == depth - 1))
    return h.reshape(batch, seq, d)


def kernel(x_prompt, x_sample, p_prompt, p_sample, rel_bias_table, norm_mix, norm_ffn, norm_ple, final_norm, ab_w_in, ab_w_out, diff_lambda, diff_subln, na_bias, c_w_in, c_conv_w, c_conv_b, c_filt_w1, c_filt_b1, c_filt_freq, c_filt_w2, c_filt_b2, c_filt_w3, c_skip, c_w_out, ffn_w_in, ffn_w_out, ple_w_proj, ple_w_gate):
    bf = lambda w: w.astype(BF16)
    W = dict(rel_bias_table=rel_bias_table, norm_mix=norm_mix, norm_ffn=norm_ffn,
             norm_ple=norm_ple, final_norm=final_norm, ab_w_in=bf(ab_w_in), ab_w_out=bf(ab_w_out),
             diff_lambda=diff_lambda, diff_subln=diff_subln, na_bias=na_bias,
             c_w_in=bf(c_w_in), c_conv_w=c_conv_w, c_conv_b=c_conv_b, c_filt_w1=c_filt_w1,
             c_filt_b1=c_filt_b1, c_filt_freq=c_filt_freq, c_filt_w2=c_filt_w2,
             c_filt_b2=c_filt_b2, c_filt_w3=bf(c_filt_w3), c_skip=c_skip, c_w_out=bf(c_w_out),
             ffn_w_in=bf(ffn_w_in), ffn_w_out=bf(ffn_w_out), ple_w_proj=bf(ple_w_proj),
             ple_w_gate=bf(ple_w_gate))
    return (_trunk(x_prompt, p_prompt, W), _trunk(x_sample, p_sample, W))
```

```python
import functools
import math

import numpy as np
import jax
import jax.numpy as jnp
from jax import lax
from jax.experimental import pallas as pl
from jax.experimental.pallas import tpu as pltpu

F32 = jnp.float32
BF16 = jnp.bfloat16

D_MODEL = 2048
GRID_W = 64
A_HEADS = 8
A_QK_DIM = 64
A_V_DIM = 2 * A_QK_DIM
A_WIDTH = A_HEADS * A_V_DIM
T5_BUCKETS = 32
T5_MAX_DIST = 128
B_HEADS = 8
B_HEAD_DIM = 128
B_WIDTH = B_HEADS * B_HEAD_DIM
NA_ROWS = 8
NA_COLS = 16
C_WIDTH = D_MODEL
FILTER_EMB = 33
FILTER_HIDDEN = 64
FILTER_TARGET = 1e-2
FAST_DECAY_PCT = 0.3
SLOW_DECAY_PCT = 1.5
EPS = 1e-6

LANES = 128
NA_KEYS = NA_ROWS * GRID_W
MASK_VALUE = -1e30
LOG2E = math.log2(math.e)
POLY = 4
SPLIT_CHUNK = 512
NA_UNROLL = 32
VMEM_LIMIT = 56 * 1024 * 1024


def _cparams(*sem):
    return pltpu.CompilerParams(dimension_semantics=sem, vmem_limit_bytes=VMEM_LIMIT)


def _rms(x, g):
    ms = jnp.mean(x * x, axis=-1, keepdims=True)
    return x * lax.rsqrt(ms + EPS) * g


def _dot(a, b):
    return jnp.dot(a, b, preferred_element_type=F32)


def _dot_nt(a, b):
    return lax.dot_general(a, b, (((1,), (1,)), ((), ())), preferred_element_type=F32)


def _norm_matmul_kernel(x_ref, g_ref, w_ref, cs_ref, o_ref, xn_ref):
    @pl.when(pl.program_id(1) == 0)
    def _():
        xn_ref[...] = _rms(x_ref[...], g_ref[...]).astype(BF16)

    o_ref[...] = (_dot(xn_ref[...], w_ref[...]) * cs_ref[...]).astype(o_ref.dtype)


def norm_matmul(x, g, w, col_scale, out_dtype, tm=1024, tn=1024):
    n, d = x.shape
    d_out = w.shape[1]
    return pl.pallas_call(
        _norm_matmul_kernel,
        grid=(n // tm, d_out // tn),
        in_specs=[
            pl.BlockSpec((tm, d), lambda i, j: (i, 0)),
            pl.BlockSpec((1, d), lambda i, j: (0, 0)),
            pl.BlockSpec((d, tn), lambda i, j: (0, j)),
            pl.BlockSpec((1, tn), lambda i, j: (0, j)),
        ],
        out_specs=pl.BlockSpec((tm, tn), lambda i, j: (i, j)),
        out_shape=jax.ShapeDtypeStruct((n, d_out), out_dtype),
        scratch_shapes=[pltpu.VMEM((tm, d), BF16)],
        compiler_params=_cparams("parallel", "arbitrary"),
        name="norm_matmul",
    )(x, g.reshape(1, d), w, col_scale.reshape(1, d_out))


def _phase_permutation(n):
    t = lax.broadcasted_iota(jnp.int32, (n, n), 0)
    j = lax.broadcasted_iota(jnp.int32, (n, n), 1)
    return jnp.where(j == (t % POLY) * (n // POLY) + t // POLY, 1.0, 0.0).astype(BF16)


def _proj_res_kernel(*refs, n_in, interleave):
    a_refs = refs[:n_in]
    w_refs = refs[n_in:2 * n_in]
    res_ref, o_ref = refs[2 * n_in], refs[2 * n_in + 1]
    acc = res_ref[...]
    for a_ref, w_ref in zip(a_refs, w_refs):
        if interleave:
            n = POLY * a_ref.shape[2]
            a = _dot(_phase_permutation(n), a_ref[0].reshape(n, a_ref.shape[3])).astype(BF16)
        else:
            a = a_ref[...]
        acc = acc + _dot(a, w_ref[...])
    o_ref[...] = acc


def proj_residual(a_list, w_list, res, interleave=False, tm=512):
    n, d = res.shape
    n_in = len(a_list)
    if interleave:
        in_specs = [pl.BlockSpec((1, POLY, tm // POLY, a.shape[-1]), lambda i: (i, 0, 0, 0))
                    for a in a_list]
    else:
        in_specs = [pl.BlockSpec((tm, a.shape[1]), lambda i: (i, 0)) for a in a_list]
    in_specs += [pl.BlockSpec(w.shape, lambda i: (0, 0)) for w in w_list]
    in_specs += [pl.BlockSpec((tm, d), lambda i: (i, 0))]
    return pl.pallas_call(
        functools.partial(_proj_res_kernel, n_in=n_in, interleave=interleave),
        grid=(n // tm,),
        in_specs=in_specs,
        out_specs=pl.BlockSpec((tm, d), lambda i: (i, 0)),
        out_shape=jax.ShapeDtypeStruct((n, d), F32),
        compiler_params=_cparams("parallel"),
        name="proj_residual",
    )(*a_list, *w_list, res)


def _ffn_kernel(x_ref, g_ref, wg_ref, wu_ref, wo_ref, o_ref, xn_ref):
    @pl.when(pl.program_id(1) == 0)
    def _():
        x = x_ref[...]
        xn_ref[...] = _rms(x, g_ref[...]).astype(BF16)
        o_ref[...] = x

    xn = xn_ref[...]
    gate = _dot(xn, wg_ref[...])
    up = _dot(xn, wu_ref[...])
    act = (gate * jax.nn.sigmoid(gate) * up).astype(BF16)
    o_ref[...] += _dot(act, wo_ref[...])


def ffn(x, g, w_in, w_out, tm=1024, tf=512):
    n, d = x.shape
    d_ff = w_out.shape[0]
    nf = d_ff // tf
    return pl.pallas_call(
        _ffn_kernel,
        grid=(n // tm, nf),
        in_specs=[
            pl.BlockSpec((tm, d), lambda i, f: (i, 0)),
            pl.BlockSpec((1, d), lambda i, f: (0, 0)),
            pl.BlockSpec((d, tf), lambda i, f: (0, f)),
            pl.BlockSpec((d, tf), lambda i, f: (0, nf + f)),
            pl.BlockSpec((tf, d), lambda i, f: (f, 0)),
        ],
        out_specs=pl.BlockSpec((tm, d), lambda i, f: (i, 0)),
        out_shape=jax.ShapeDtypeStruct((n, d), F32),
        scratch_shapes=[pltpu.VMEM((tm, d), BF16)],
        compiler_params=_cparams("parallel", "arbitrary"),
        name="ffn",
    )(x, g.reshape(1, d), w_in, w_in, w_out)


def _ple_kernel(x_ref, p_ref, g_ref, wg_ref, wp_ref, gf_ref, o_ref, *, final_norm):
    x = x_ref[...]
    gate = jax.nn.sigmoid(_dot(_rms(x, g_ref[...]).astype(BF16), wg_ref[...]))
    proj = _dot(p_ref[...].astype(BF16), wp_ref[...])
    y = x + proj * gate
    if final_norm:
        y = _rms(y, gf_ref[...])
    o_ref[...] = y


def ple(x, p, g, w_gate, w_proj, g_final, final_norm, tm=512):
    n, d = x.shape
    dp = p.shape[1]
    return pl.pallas_call(
        functools.partial(_ple_kernel, final_norm=final_norm),
        grid=(n // tm,),
        in_specs=[
            pl.BlockSpec((tm, d), lambda i: (i, 0)),
            pl.BlockSpec((tm, dp), lambda i: (i, 0)),
            pl.BlockSpec((1, d), lambda i: (0, 0)),
            pl.BlockSpec((d, d), lambda i: (0, 0)),
            pl.BlockSpec((dp, d), lambda i: (0, 0)),
            pl.BlockSpec((1, d), lambda i: (0, 0)),
        ],
        out_specs=pl.BlockSpec((tm, d), lambda i: (i, 0)),
        out_shape=jax.ShapeDtypeStruct((n, d), F32),
        compiler_params=_cparams("parallel"),
        name="ple",
    )(x, p, g.reshape(1, d), w_gate, w_proj, g_final.reshape(1, d))


def t5_bucket(rel):
    nb = T5_BUCKETS // 2
    max_exact = nb // 2
    ret = jnp.where(rel > 0, nb, 0)
    n = jnp.abs(rel)
    nf = jnp.maximum(n, 1).astype(jnp.float32)
    large = max_exact + (jnp.log(nf / max_exact) / math.log(T5_MAX_DIST / max_exact)
                         * (nb - max_exact)).astype(jnp.int32)
    large = jnp.minimum(large, nb - 1)
    return ret + jnp.where(n < max_exact, n, large)


def _lane_tiles(x):
    return [x[:, j * LANES:(j + 1) * LANES] for j in range(x.shape[1] // LANES)]


def _diff_attn_kernel(lam_ref, q_ref, k_ref, v_ref, e_ref, g_ref, o_ref,
                      s1_ref, s2_ref, p1_ref, p2_ref, va_ref, *, tq, tk, seq, pad, width,
                      out_scale):
    i = pl.program_id(2)

    @pl.when(i == 0)
    def _():
        va_ref[:, :A_V_DIM] = v_ref[0]
        va_ref[:, A_V_DIM:] = jnp.ones((seq, A_V_DIM), BF16)

    q = q_ref[0]
    lane = lax.broadcasted_iota(jnp.int32, q.shape, 1)
    zero = jnp.zeros_like(q)
    q1 = jnp.where(lane < A_QK_DIM, q, zero)
    q2 = jnp.where(lane < A_QK_DIM, zero, q)

    def bias_tile(j):
        c = j * LANES // tk
        start = jnp.clip(c * tk - i * tq + pad, 0, width - tk) + (j * LANES - c * tk)
        return e_ref[0, :, pl.ds(pl.multiple_of(start, LANES), LANES)]

    m1 = m2 = None
    for c in range(seq // tk):
        cols = slice(c * tk, (c + 1) * tk)
        kc = k_ref[0, cols, :]
        x1 = _dot_nt(q1, kc)
        x2 = _dot_nt(q2, kc)
        s1_ref[:, cols] = x1
        s2_ref[:, cols] = x2
        for jj, (t1, t2) in enumerate(zip(_lane_tiles(x1), _lane_tiles(x2))):
            bias = bias_tile(c * (tk // LANES) + jj)
            m1 = t1 + bias if m1 is None else jnp.maximum(m1, t1 + bias)
            m2 = t2 + bias if m2 is None else jnp.maximum(m2, t2 + bias)
    m1 = jnp.broadcast_to(jnp.max(m1, axis=-1, keepdims=True), (tq, LANES))
    m2 = jnp.broadcast_to(jnp.max(m2, axis=-1, keepdims=True), (tq, LANES))
    for j in range(seq // LANES):
        cols = slice(j * LANES, (j + 1) * LANES)
        shift = bias_tile(j)
        p1_ref[:, cols] = jnp.exp2(s1_ref[:, cols] + (shift - m1)).astype(BF16)
        p2_ref[:, LANES + j * LANES:LANES + (j + 1) * LANES] = jnp.exp2(
            s2_ref[:, cols] + (shift - m2)).astype(BF16)
    o1 = _dot(p1_ref[:, :seq], va_ref[...])
    o2 = _dot(p2_ref[:, LANES:LANES + seq], va_ref[...])
    o = (o1[:, :A_V_DIM] * (1.0 / o1[:, A_V_DIM:])
         - o2[:, :A_V_DIM] * (lam_ref[0] / o2[:, A_V_DIM:]))
    o = _rms(o, g_ref[...]) * out_scale
    o_ref[0] = o.astype(o_ref.dtype)


def _band_kernel(w_ref, o_ref, *, rows, cols):
    n = rows + cols
    x = jnp.broadcast_to(w_ref[0], (rows, n))
    y = pltpu.roll(x, n - (rows - 1), 1, stride=1, stride_axis=0)
    o_ref[0] = y[:, :cols]


def _toeplitz(w, rows, cols):
    heads, n = w.shape
    assert n == rows + cols and n % LANES == 0
    return pl.pallas_call(
        functools.partial(_band_kernel, rows=rows, cols=cols),
        grid=(heads,),
        in_specs=[pl.BlockSpec((1, 1, n), lambda h: (h, 0, 0))],
        out_specs=pl.BlockSpec((1, rows, cols), lambda h: (h, 0, 0)),
        out_shape=jax.ShapeDtypeStruct((heads, rows, cols), F32),
        compiler_params=_cparams("parallel"),
        name="toeplitz_band",
    )(w.reshape(heads, 1, n))


def diff_attention(z, lam, rel_table, subln_g, lam_init, tq=512, tk=512):
    b, seq, _ = z.shape
    pad = tk + T5_MAX_DIST
    width = 2 * tk + tq + 2 * T5_MAX_DIST
    cols = width + LANES
    rel = jnp.arange(tq + cols, dtype=jnp.int32) - (tq - 1) - pad
    onehot = (t5_bucket(rel)[:, None] == jnp.arange(T5_BUCKETS)[None, :]).astype(F32)
    diag = jnp.dot(onehot, rel_table.astype(F32), precision=lax.Precision.HIGHEST)
    band = _toeplitz(diag.T * LOG2E, tq, cols)
    hq = A_WIDTH // LANES
    return pl.pallas_call(
        functools.partial(_diff_attn_kernel, tq=tq, tk=tk, seq=seq, pad=pad, width=width,
                          out_scale=1.0 - lam_init),
        grid=(b, A_HEADS, seq // tq),
        in_specs=[
            pl.BlockSpec(memory_space=pltpu.SMEM),
            pl.BlockSpec((1, tq, LANES), lambda bi, h, i: (bi, i, h)),
            pl.BlockSpec((1, seq, LANES), lambda bi, h, i: (bi, 0, hq + h)),
            pl.BlockSpec((1, seq, LANES), lambda bi, h, i: (bi, 0, 2 * hq + h)),
            pl.BlockSpec((1, tq, cols), lambda bi, h, i: (h, 0, 0)),
            pl.BlockSpec((1, A_V_DIM), lambda bi, h, i: (0, 0)),
        ],
        out_specs=pl.BlockSpec((1, tq, LANES), lambda bi, h, i: (bi, i, h)),
        out_shape=jax.ShapeDtypeStruct((b, seq, A_WIDTH), BF16),
        scratch_shapes=[pltpu.VMEM((tq, seq), F32), pltpu.VMEM((tq, seq), F32),
                        pltpu.VMEM((tq, seq + 3 * LANES), BF16),
                        pltpu.VMEM((tq, seq + 3 * LANES), BF16),
                        pltpu.VMEM((seq, 2 * A_V_DIM), BF16)],
        compiler_params=_cparams("parallel", "parallel", "arbitrary"),
        name="diff_attention",
    )(lam.reshape(1).astype(F32), z, z, z, band, subln_g.reshape(1, A_V_DIM))


def _na_kernel(q_ref, k_ref, v_ref, nb_ref, o_ref, *, rows):
    def row_group(gi, carry):
        geom = []
        for u in range(NA_UNROLL):
            r = gi * NA_UNROLL + u
            r_start = jnp.clip(r - NA_ROWS // 2, 0, rows - NA_ROWS)
            geom.append((pl.multiple_of(r * GRID_W, GRID_W),
                         pl.multiple_of(r_start * GRID_W, GRID_W), r - r_start))
        scores = [_dot_nt(q_ref[0, pl.ds(q0, GRID_W), :], k_ref[0, pl.ds(k0, NA_KEYS), :])
                  + nb_ref[0, var] for q0, k0, var in geom]
        probs = []
        for s in scores:
            p = jnp.exp2(s - jnp.max(s, axis=-1, keepdims=True))
            probs.append((p * (1.0 / jnp.sum(p, axis=-1, keepdims=True))).astype(BF16))
        for p, (q0, k0, _) in zip(probs, geom):
            o = _dot(p, v_ref[0, pl.ds(k0, NA_KEYS), :])
            o_ref[0, pl.ds(q0, GRID_W), :] = o.astype(o_ref.dtype)
        return carry

    lax.fori_loop(0, rows // NA_UNROLL, row_group, 0)


def _na_bias(na_table):
    var = np.arange(NA_ROWS)[:, None, None]
    kr = np.arange(NA_ROWS)[None, :, None]
    row_sel = (kr - var + (NA_ROWS - 1) == np.arange(2 * NA_ROWS - 1)).astype(np.float32)
    c = np.arange(GRID_W)[:, None, None]
    kc = np.arange(GRID_W)[None, :, None]
    col_sel = (kc - c + (NA_COLS - 1) == np.arange(2 * NA_COLS - 1)).astype(np.float32)
    c_start = np.clip(c - NA_COLS // 2, 0, GRID_W - NA_COLS)
    valid = ((kc >= c_start) & (kc < c_start + NA_COLS))[None, None, :, None, :, 0]
    bias = jnp.einsum('vka,hab,cqb->hvckq', row_sel, na_table.astype(F32), col_sel,
                      precision=lax.Precision.HIGHEST)
    bias = jnp.where(valid, bias * LOG2E, MASK_VALUE)
    return bias.reshape(B_HEADS, NA_ROWS, GRID_W, NA_KEYS)


def neighbourhood_attention(z, na_table):
    b, seq, _ = z.shape
    rows = seq // GRID_W
    assert rows >= NA_ROWS and rows % NA_UNROLL == 0
    hq = B_WIDTH // LANES
    base = 3 * A_WIDTH // LANES
    return pl.pallas_call(
        functools.partial(_na_kernel, rows=rows),
        grid=(b, B_HEADS),
        in_specs=[
            pl.BlockSpec((1, seq, LANES), lambda bi, h: (bi, 0, base + h)),
            pl.BlockSpec((1, seq, LANES), lambda bi, h: (bi, 0, base + hq + h)),
            pl.BlockSpec((1, seq, LANES), lambda bi, h: (bi, 0, base + 2 * hq + h)),
            pl.BlockSpec((1, NA_ROWS, GRID_W, NA_KEYS), lambda bi, h: (h, 0, 0, 0)),
        ],
        out_specs=pl.BlockSpec((1, seq, LANES), lambda bi, h: (bi, 0, h)),
        out_shape=jax.ShapeDtypeStruct((b, seq, B_WIDTH), BF16),
        compiler_params=_cparams("parallel", "parallel"),
        name="neighbourhood_attention",
    )(z, z, z, _na_bias(na_table))


HALO = 16
N_LAGS = 2 * POLY - 1


def _hyena_in_kernel(x_ref, xp_ref, xq_ref, g_ref, w0_ref, w1_ref, wv_ref,
                     c0_ref, c1_ref, cv_ref, b0_ref, b1_ref, bv_ref,
                     x0_out, vv_out, xn_ref, *, tiles_per_seq):
    i = pl.program_id(0)
    tm = x_ref.shape[0]
    hk = tm // POLY

    @pl.when(pl.program_id(1) == 0)
    def _():
        g = g_ref[...]
        xn = _rms(x_ref[...], g).astype(BF16)
        xn_ref[HALO:HALO + tm, :] = lax.dot_general(
            _phase_permutation(tm), xn, (((0,), (0,)), ((), ())),
            preferred_element_type=F32).astype(BF16)
        first = (i % tiles_per_seq) == 0
        last = (i % tiles_per_seq) == tiles_per_seq - 1
        xp = _rms(xp_ref[...], g)
        xq = _rms(xq_ref[...], g)
        xn_ref[0:HALO, :] = jnp.where(first, 0.0, xp).astype(BF16)
        xn_ref[HALO + tm:2 * HALO + tm, :] = jnp.where(last, 0.0, xq).astype(BF16)

    xn = xn_ref[...]
    row = lax.broadcasted_iota(jnp.int32, (hk, w0_ref.shape[1]), 0)

    def conv(w_ref, c_ref, b_ref):
        z = _dot(xn, w_ref[...])
        zp = [z[HALO + p * hk:HALO + (p + 1) * hk] for p in range(POLY)]
        before = jnp.where(row == 0, z[HALO - 1:HALO], pltpu.roll(zp[POLY - 1], 1, 0))
        after = jnp.where(row == hk - 1, z[HALO + tm:HALO + tm + 1], pltpu.roll(zp[0], hk - 1, 0))
        ext = [before] + zp + [after]
        c0, c1, c2, bias = c_ref[0:1, :], c_ref[1:2, :], c_ref[2:3, :], b_ref[...]
        return [ext[p] * c0 + ext[p + 1] * c1 + ext[p + 2] * c2 + bias for p in range(POLY)]

    x0 = conv(w0_ref, c0_ref, b0_ref)
    x1 = conv(w1_ref, c1_ref, b1_ref)
    v = conv(wv_ref, cv_ref, bv_ref)
    for p in range(POLY):
        x0_out[0, p] = x0[p].astype(x0_out.dtype)
        vv_out[0, p] = (v[p] * x1[p]).astype(vv_out.dtype)


def hyena_in(h, seq, g, w_in, conv_w, conv_b, tc=1024):
    n, d = h.shape
    tm = SPLIT_CHUNK
    nc = C_WIDTH // tc
    blocks = tm // HALO
    last_block = n // HALO - 1
    wspec = lambda o: pl.BlockSpec((d, tc), lambda i, c: (0, o * nc + c))
    cspec = lambda o: pl.BlockSpec((3, tc), lambda i, c: (0, o * nc + c))
    bspec = lambda o: pl.BlockSpec((1, tc), lambda i, c: (0, o * nc + c))
    cb = conv_b.reshape(1, -1)
    out = jax.ShapeDtypeStruct((n // tm, POLY, tm // POLY, C_WIDTH), BF16)
    ospec = pl.BlockSpec((1, POLY, tm // POLY, tc), lambda i, c: (i, 0, 0, c))
    return pl.pallas_call(
        functools.partial(_hyena_in_kernel, tiles_per_seq=seq // tm),
        grid=(n // tm, nc),
        in_specs=[
            pl.BlockSpec((tm, d), lambda i, c: (i, 0)),
            pl.BlockSpec((HALO, d), lambda i, c: (jnp.maximum(i * blocks - 1, 0), 0)),
            pl.BlockSpec((HALO, d), lambda i, c: (jnp.minimum((i + 1) * blocks, last_block), 0)),
            pl.BlockSpec((1, d), lambda i, c: (0, 0)),
            wspec(0), wspec(1), wspec(2), cspec(0), cspec(1), cspec(2),
            bspec(0), bspec(1), bspec(2),
        ],
        out_specs=[ospec, ospec],
        out_shape=[out, out],
        scratch_shapes=[pltpu.VMEM((tm + 2 * HALO, d), BF16)],
        compiler_params=_cparams("parallel", "arbitrary"),
        name="hyena_in",
    )(h, h, h, g.reshape(1, d), w_in, w_in, w_in, conv_w, conv_w, conv_w, cb, cb, cb)


def _filter_kernel(hd_ref, wf_ref, wb_ref, dl_ref, taps_ref, *, tl, seq):
    hd = hd_ref[...].astype(BF16)
    lane_group = lax.broadcasted_iota(jnp.int32, hd.shape, 1) // FILTER_HIDDEN
    zero = jnp.zeros_like(hd)
    m = pl.program_id(0) * tl + lax.broadcasted_iota(jnp.int32, taps_ref.shape[1:], 0)
    dl = dl_ref[...]

    def tap(w_ref, shift, r):
        token = (POLY * (m - 1 + shift) + r).astype(F32)
        x = jnp.where(lane_group == shift * POLY + r, hd, zero)
        return _dot(x, w_ref[...]) * jnp.exp(token * (-1.0 / (seq - 1)) * dl)

    for d in range(-POLY + 1, POLY):
        if d == 0:
            c = tap(wf_ref, 1, 0)
            a = jnp.where(m == 0, 0.0, tap(wb_ref, 1, 0))
            total, diff = c + a, c - a
        elif d > 0:
            c = tap(wf_ref, 1, d)
            a = tap(wb_ref, 0, POLY - d)
            total, diff = c + a, c - a
        else:
            c = tap(wf_ref, 0, POLY + d)
            a = tap(wb_ref, 1, -d)
            total, diff = c + a, c - jnp.where(m == 0, -a, a)
        taps_ref[2 * (d + POLY - 1)] = total.astype(taps_ref.dtype)
        taps_ref[2 * (d + POLY - 1) + 1] = diff.astype(taps_ref.dtype)


def hyena_filter_taps(hdn, w3, deltas, tl=512, tc=512):
    seq, hidden = hdn.shape
    sub = seq // POLY
    tl = min(tl, sub)
    nc = C_WIDTH // tc
    cur = hdn.reshape(sub, POLY * hidden)
    prev = jnp.concatenate([jnp.zeros_like(cur[:1]), cur[:-1]], axis=0)
    hd = jnp.concatenate([prev, cur], axis=1)
    w3s = jnp.tile(w3, (2 * POLY, 1))
    return pl.pallas_call(
        functools.partial(_filter_kernel, tl=tl, seq=seq),
        grid=(sub // tl, nc),
        in_specs=[
            pl.BlockSpec((tl, 2 * POLY * hidden), lambda l, c: (l, 0)),
            pl.BlockSpec((2 * POLY * hidden, tc), lambda l, c: (0, c)),
            pl.BlockSpec((2 * POLY * hidden, tc), lambda l, c: (0, nc + c)),
            pl.BlockSpec((1, tc), lambda l, c: (0, c)),
        ],
        out_specs=pl.BlockSpec((2 * N_LAGS, tl, tc), lambda l, c: (0, l, c)),
        out_shape=jax.ShapeDtypeStruct((2 * N_LAGS, sub, C_WIDTH), BF16),
        compiler_params=_cparams("parallel", "parallel"),
        name="hyena_filter_taps",
    )(hd, w3s, w3s, deltas.reshape(1, C_WIDTH))


NYQ_ROWS = 16


def _filter_dft_kernel(m_ref, taps_ref, coef_ref, nyq_ref, *, tf, sub):
    fi = pl.program_id(1)
    cos_m, sin_m = m_ref[0], m_ref[1]
    row0 = (fi * tf + lax.broadcasted_iota(jnp.int32, coef_ref.shape[1:], 0)) == 0
    wt = jnp.where(row0, 0.5 / sub, 1.0 / sub)
    for d in range(N_LAGS):
        coef_ref[2 * d] = _dot(cos_m, taps_ref[2 * d]) * wt
        coef_ref[2 * d + 1] = jnp.where(row0, 0.0, _dot(sin_m, taps_ref[2 * d + 1]) * wt)

    @pl.when(fi == 0)
    def _():
        for d in range(N_LAGS):
            nyq_ref[d] = _dot(sin_m[0:NYQ_ROWS, :], taps_ref[2 * d]) * (0.5 / sub)


def hyena_filter_dft(mmat, taps, tf=256, tc=512):
    sub = mmat.shape[1]
    tf = min(tf, sub)
    return pl.pallas_call(
        functools.partial(_filter_dft_kernel, tf=tf, sub=sub),
        grid=(C_WIDTH // tc, sub // tf),
        in_specs=[pl.BlockSpec((2, tf, sub), lambda c, f: (0, f, 0)),
                  pl.BlockSpec((2 * N_LAGS, sub, tc), lambda c, f: (0, 0, c))],
        out_specs=[pl.BlockSpec((2 * N_LAGS, tf, tc), lambda c, f: (0, f, c)),
                   pl.BlockSpec((N_LAGS, NYQ_ROWS, tc), lambda c, f: (0, 0, c))],
        out_shape=[jax.ShapeDtypeStruct((2 * N_LAGS, sub, C_WIDTH), F32),
                   jax.ShapeDtypeStruct((N_LAGS, NYQ_ROWS, C_WIDTH), F32)],
        compiler_params=_cparams("parallel", "arbitrary"),
        name="hyena_filter_dft",
    )(mmat, taps)


def _dft_fwd_kernel(m_ref, *refs, tf):
    v_refs, (coef_ref, nyq_ref, z_ref) = refs[:POLY], refs[POLY:]
    sub, tc = m_ref.shape[2], z_ref.shape[-1]
    m = m_ref[...].reshape(2 * tf, sub)
    spectra = [_dot(m, v_ref[0, :, 0].reshape(sub, tc)) for v_ref in v_refs]
    a_re = [a[:tf] for a in spectra]
    a_im = [a[tf:] for a in spectra]
    row0 = (pl.program_id(2) * tf + lax.broadcasted_iota(jnp.int32, (tf, tc), 0)) == 0
    h_re = [coef_ref[2 * d] for d in range(N_LAGS)]
    h_im = [coef_ref[2 * d + 1] for d in range(N_LAGS)]
    h_ny = [jnp.where(row0, nyq_ref[d, 0:1, :], h_re[d]) for d in range(N_LAGS)]
    for p in range(POLY):
        lag = [p - r + POLY - 1 for r in range(POLY)]
        z_re = sum(h_re[lag[r]] * a_re[r] - h_im[lag[r]] * a_im[r] for r in range(POLY))
        z_im = sum(h_ny[lag[r]] * a_im[r] + h_im[lag[r]] * a_re[r] for r in range(POLY))
        z_ref[0, p, 0] = z_re.astype(z_ref.dtype)
        z_ref[0, p, 1] = z_im.astype(z_ref.dtype)


def hyena_dft_fwd(mmat, vv, coefs, nyq, tf=256, tc=512):
    b, nck, _, hk, _ = vv.shape
    sub = nck * hk
    tf = min(tf, sub)
    v_spec = lambda p: pl.BlockSpec((1, nck, 1, hk, tc), lambda bi, c, f: (bi, 0, p, 0, c))
    return pl.pallas_call(
        functools.partial(_dft_fwd_kernel, tf=tf),
        grid=(b, C_WIDTH // tc, sub // tf),
        in_specs=[pl.BlockSpec((2, tf, sub), lambda bi, c, f: (0, f, 0))]
                 + [v_spec(p) for p in range(POLY)]
                 + [pl.BlockSpec((2 * N_LAGS, tf, tc), lambda bi, c, f: (0, f, c)),
                    pl.BlockSpec((N_LAGS, NYQ_ROWS, tc), lambda bi, c, f: (0, 0, c))],
        out_specs=pl.BlockSpec((1, POLY, 2, tf, tc), lambda bi, c, f: (bi, 0, 0, f, c)),
        out_shape=jax.ShapeDtypeStruct((b, POLY, 2, sub, C_WIDTH), BF16),
        compiler_params=_cparams("parallel", "parallel", "parallel"),
        name="hyena_dft_fwd",
    )(mmat, *([vv] * POLY), coefs, nyq)


def _dft_inv_kernel(g_ref, z_ref, vv_ref, x0_ref, skip_ref, u_ref):
    g = g_ref[...]
    nk, _, hk, tc = u_ref.shape[1:]
    for p in range(POLY):
        y = _dot(g, z_ref[0, p])
        vv = vv_ref[0, :, p].reshape(nk * hk, tc).astype(F32)
        x0 = x0_ref[0, :, p].reshape(nk * hk, tc).astype(F32)
        u = ((y + vv * skip_ref[...]) * x0).astype(u_ref.dtype)
        u_ref[0, :, p] = u.reshape(nk, hk, tc)


def hyena_dft_inv(gmat, zs, vv, x0, skip, tt=1024, tc=512):
    b, nck, _, hk, _ = vv.shape
    sub = nck * hk
    tt = min(tt, sub)
    io_spec = pl.BlockSpec((1, tt // hk, POLY, hk, tc), lambda bi, c, t: (bi, t, 0, 0, c))
    return pl.pallas_call(
        _dft_inv_kernel,
        grid=(b, C_WIDTH // tc, sub // tt),
        in_specs=[
            pl.BlockSpec((tt, 2 * sub), lambda bi, c, t: (t, 0)),
            pl.BlockSpec((1, POLY, 2 * sub, tc), lambda bi, c, t: (bi, 0, 0, c)),
            io_spec,
            io_spec,
            pl.BlockSpec((1, tc), lambda bi, c, t: (0, c)),
        ],
        out_specs=io_spec,
        out_shape=jax.ShapeDtypeStruct(vv.shape, BF16),
        compiler_params=_cparams("parallel", "parallel", "parallel"),
        name="hyena_dft_inv",
    )(gmat, zs, vv, x0, skip.reshape(1, C_WIDTH))


def _dft_matrices(sub):
    idx = jnp.arange(sub, dtype=jnp.int32)
    ang = ((idx[:, None] * idx[None, :]) % (2 * sub)).astype(F32) * (math.pi / sub)
    sin_part = (-jnp.sin(ang)).at[0].set((1 - 2 * (idx % 2)).astype(F32))
    mmat = jnp.stack([jnp.cos(ang), sin_part]).astype(BF16)
    return mmat, mmat.reshape(2 * sub, sub).T


def _filter_hidden(seq, w1, b1, freq, w2, b2):
    t = jnp.linspace(0.0, 1.0, seq, dtype=F32)[:, None]
    bands = (FILTER_EMB - 1) // 2
    w = 2.0 * math.pi * jnp.arange(seq, dtype=F32) / seq
    f = jnp.linspace(1e-4, bands - 1, bands, dtype=F32)
    ang = w[:, None] * f[None, :]
    feats = jnp.concatenate([t, jnp.cos(ang), -jnp.sin(ang)], axis=-1)
    hdn = jnp.sin(freq[0] * (feats @ w1 + b1))
    return jnp.sin(freq[1] * (hdn @ w2 + b2))


def hyena_mixer(h, batch, seq, g, w_in, conv_w, conv_b, w1, b1, freq, w2, b2, w3, skip, w_out):
    sub = seq // POLY
    split_shape = (batch, seq // SPLIT_CHUNK, POLY, SPLIT_CHUNK // POLY, C_WIDTH)
    x0, vv = (a.reshape(split_shape) for a in hyena_in(h, seq, g, w_in, conv_w, conv_b))
    hdn = _filter_hidden(seq, w1, b1, freq, w2, b2)
    min_decay = math.log(FILTER_TARGET) / FAST_DECAY_PCT
    max_decay = math.log(FILTER_TARGET) / SLOW_DECAY_PCT
    deltas = jnp.abs(jnp.linspace(min_decay, max_decay, C_WIDTH, dtype=F32))
    taps = hyena_filter_taps(hdn, w3, deltas)
    mmat, gmat = _dft_matrices(sub)
    coefs, nyq = hyena_filter_dft(mmat, taps)
    zs = hyena_dft_fwd(mmat, vv, coefs, nyq).reshape(batch, POLY, 2 * sub, C_WIDTH)
    u = hyena_dft_inv(gmat, zs, vv, x0, skip)
    u = u.reshape(batch * seq // SPLIT_CHUNK, POLY, SPLIT_CHUNK // POLY, C_WIDTH)
    return proj_residual([u], [w_out], h, interleave=True, tm=SPLIT_CHUNK)


def even_mixer(h, batch, seq, g, w_in, w_out, lam_vec, subln_g, na_table, rel_table, layer_idx):
    col_scale = np.ones((6, A_WIDTH), np.float32)
    col_scale[0] = A_QK_DIM ** -0.5 * LOG2E
    col_scale[3] = B_HEAD_DIM ** -0.5 * LOG2E
    z = norm_matmul(h, g, w_in, jnp.asarray(col_scale.reshape(-1)), BF16)
    z = z.reshape(batch, seq, w_in.shape[1])
    lam_init = 0.8 - 0.6 * math.exp(-0.3 * layer_idx)
    lv = lam_vec.astype(F32)
    lam = jnp.exp(jnp.sum(lv[0] * lv[1])) - jnp.exp(jnp.sum(lv[2] * lv[3])) + lam_init
    oa = diff_attention(z, lam, rel_table, subln_g, lam_init)
    ob = neighbourhood_attention(z, na_table)
    n = batch * seq
    return proj_residual([oa.reshape(n, A_WIDTH), ob.reshape(n, B_WIDTH)],
                         [w_out[:A_WIDTH], w_out[A_WIDTH:]], h)


def _trunk(x, p, W):
    batch, seq, d = x.shape
    depth = p.shape[0]
    h = x.reshape(batch * seq, d)
    for i in range(depth):
        j = i // 2
        if i % 2 == 0:
            h = even_mixer(h, batch, seq, W['norm_mix'][i], W['ab_w_in'][j], W['ab_w_out'][j],
                           W['diff_lambda'][j], W['diff_subln'][j], W['na_bias'][j],
                           W['rel_bias_table'], i)
        else:
            h = hyena_mixer(h, batch, seq, W['norm_mix'][i], W['c_w_in'][j], W['c_conv_w'][j],
                            W['c_conv_b'][j], W['c_filt_w1'][j], W['c_filt_b1'][j],
                            W['c_filt_freq'][j], W['c_filt_w2'][j], W['c_filt_b2'][j],
                            W['c_filt_w3'][j], W['c_skip'][j], W['c_w_out'][j])
        h = ffn(h, W['norm_ffn'][i], W['ffn_w_in'][i], W['ffn_w_out'][i])
        h = ple(h, p[i].reshape(batch * seq, -1), W['norm_ple'][i], W['ple_w_gate'][i],
                W['ple_w_proj'][i], W['final_norm'], final_norm=(i == depth - 1))
    return h.reshape(batch, seq, d)


def kernel(x_prompt, x_sample, p_prompt, p_sample, rel_bias_table, norm_mix, norm_ffn, norm_ple, final_norm, ab_w_in, ab_w_out, diff_lambda, diff_subln, na_bias, c_w_in, c_conv_w, c_conv_b, c_filt_w1, c_filt_b1, c_filt_freq, c_filt_w2, c_filt_b2, c_filt_w3, c_skip, c_w_out, ffn_w_in, ffn_w_out, ple_w_proj, ple_w_gate):
    bf = lambda w: w.astype(BF16)
    W = dict(rel_bias_table=rel_bias_table, norm_mix=norm_mix, norm_ffn=norm_ffn,
             norm_ple=norm_ple, final_norm=final_norm, ab_w_in=bf(ab_w_in), ab_w_out=bf(ab_w_out),
             diff_lambda=diff_lambda, diff_subln=diff_subln, na_bias=na_bias,
             c_w_in=bf(c_w_in), c_conv_w=c_conv_w, c_conv_b=c_conv_b, c_filt_w1=c_filt_w1,
             c_filt_b1=c_filt_b1, c_filt_freq=c_filt_freq, c_filt_w2=c_filt_w2,
             c_filt_b2=c_filt_b2, c_filt_w3=bf(c_filt_w3), c_skip=c_skip, c_w_out=bf(c_w_out),
             ffn_w_in=bf(ffn_w_in), ffn_w_out=bf(ffn_w_out), ple_w_proj=bf(ple_w_proj),
             ple_w_gate=bf(ple_w_gate))
    return (_trunk(x_prompt, p_prompt, W), _trunk(x_sample, p_sample, W))
```

```python
import functools
import math

import numpy as np
import jax
import jax.numpy as jnp
from jax import lax
from jax.experimental import pallas as pl
from jax.experimental.pallas import tpu as pltpu

F32 = jnp.float32
BF16 = jnp.bfloat16

D_MODEL = 2048
GRID_W = 64
A_HEADS = 8
A_QK_DIM = 64
A_V_DIM = 2 * A_QK_DIM
A_WIDTH = A_HEADS * A_V_DIM
T5_BUCKETS = 32
T5_MAX_DIST = 128
B_HEADS = 8
B_HEAD_DIM = 128
B_WIDTH = B_HEADS * B_HEAD_DIM
NA_ROWS = 8
NA_COLS = 16
C_WIDTH = D_MODEL
FILTER_EMB = 33
FILTER_HIDDEN = 64
FILTER_TARGET = 1e-2
FAST_DECAY_PCT = 0.3
SLOW_DECAY_PCT = 1.5
EPS = 1e-6

LANES = 128
NA_KEYS = NA_ROWS * GRID_W
MASK_VALUE = -1e30
LOG2E = math.log2(math.e)
POLY = 4
SPLIT_CHUNK = 512
NA_UNROLL = 32
VMEM_LIMIT = 56 * 1024 * 1024


def _cparams(*sem):
    return pltpu.CompilerParams(dimension_semantics=sem, vmem_limit_bytes=VMEM_LIMIT)


def _rms(x, g):
    ms = jnp.mean(x * x, axis=-1, keepdims=True)
    return x * lax.rsqrt(ms + EPS) * g


def _dot(a, b):
    return jnp.dot(a, b, preferred_element_type=F32)


def _dot_nt(a, b):
    return lax.dot_general(a, b, (((1,), (1,)), ((), ())), preferred_element_type=F32)


def _norm_matmul_kernel(x_ref, g_ref, w_ref, cs_ref, o_ref, xn_ref):
    @pl.when(pl.program_id(1) == 0)
    def _():
        xn_ref[...] = _rms(x_ref[...], g_ref[...]).astype(BF16)

    o_ref[...] = (_dot(xn_ref[...], w_ref[...]) * cs_ref[...]).astype(o_ref.dtype)


def norm_matmul(x, g, w, col_scale, out_dtype, tm=1024, tn=1024):
    n, d = x.shape
    d_out = w.shape[1]
    return pl.pallas_call(
        _norm_matmul_kernel,
        grid=(n // tm, d_out // tn),
        in_specs=[
            pl.BlockSpec((tm, d), lambda i, j: (i, 0)),
            pl.BlockSpec((1, d), lambda i, j: (0, 0)),
            pl.BlockSpec((d, tn), lambda i, j: (0, j)),
            pl.BlockSpec((1, tn), lambda i, j: (0, j)),
        ],
        out_specs=pl.BlockSpec((tm, tn), lambda i, j: (i, j)),
        out_shape=jax.ShapeDtypeStruct((n, d_out), out_dtype),
        scratch_shapes=[pltpu.VMEM((tm, d), BF16)],
        compiler_params=_cparams("parallel", "arbitrary"),
        name="norm_matmul",
    )(x, g.reshape(1, d), w, col_scale.reshape(1, d_out))


def _phase_permutation(n):
    t = lax.broadcasted_iota(jnp.int32, (n, n), 0)
    j = lax.broadcasted_iota(jnp.int32, (n, n), 1)
    return jnp.where(j == (t % POLY) * (n // POLY) + t // POLY, 1.0, 0.0).astype(BF16)


def _proj_res_kernel(*refs, n_in, interleave):
    a_refs = refs[:n_in]
    w_refs = refs[n_in:2 * n_in]
    res_ref, o_ref = refs[2 * n_in], refs[2 * n_in + 1]
    acc = res_ref[...]
    for a_ref, w_ref in zip(a_refs, w_refs):
        if interleave:
            n = POLY * a_ref.shape[2]
            a = _dot(_phase_permutation(n), a_ref[0].reshape(n, a_ref.shape[3])).astype(BF16)
        else:
            a = a_ref[...]
        acc = acc + _dot(a, w_ref[...])
    o_ref[...] = acc


def proj_residual(a_list, w_list, res, interleave=False, tm=512):
    n, d = res.shape
    n_in = len(a_list)
    if interleave:
        in_specs = [pl.BlockSpec((1, POLY, tm // POLY, a.shape[-1]), lambda i: (i, 0, 0, 0))
                    for a in a_list]
    else:
        in_specs = [pl.BlockSpec((tm, a.shape[1]), lambda i: (i, 0)) for a in a_list]
    in_specs += [pl.BlockSpec(w.shape, lambda i: (0, 0)) for w in w_list]
    in_specs += [pl.BlockSpec((tm, d), lambda i: (i, 0))]
    return pl.pallas_call(
        functools.partial(_proj_res_kernel, n_in=n_in, interleave=interleave),
        grid=(n // tm,),
        in_specs=in_specs,
        out_specs=pl.BlockSpec((tm, d), lambda i: (i, 0)),
        out_shape=jax.ShapeDtypeStruct((n, d), F32),
        compiler_params=_cparams("parallel"),
        name="proj_residual",
    )(*a_list, *w_list, res)


def _ffn_kernel(x_ref, g_ref, wg_ref, wu_ref, wo_ref, o_ref, xn_ref):
    @pl.when(pl.program_id(1) == 0)
    def _():
        x = x_ref[...]
        xn_ref[...] = _rms(x, g_ref[...]).astype(BF16)
        o_ref[...] = x

    xn = xn_ref[...]
    gate = _dot(xn, wg_ref[...])
    up = _dot(xn, wu_ref[...])
    act = (gate * jax.nn.sigmoid(gate) * up).astype(BF16)
    o_ref[...] += _dot(act, wo_ref[...])


def ffn(x, g, w_in, w_out, tm=1024, tf=512):
    n, d = x.shape
    d_ff = w_out.shape[0]
    nf = d_ff // tf
    return pl.pallas_call(
        _ffn_kernel,
        grid=(n // tm, nf),
        in_specs=[
            pl.BlockSpec((tm, d), lambda i, f: (i, 0)),
            pl.BlockSpec((1, d), lambda i, f: (0, 0)),
            pl.BlockSpec((d, tf), lambda i, f: (0, f)),
            pl.BlockSpec((d, tf), lambda i, f: (0, nf + f)),
            pl.BlockSpec((tf, d), lambda i, f: (f, 0)),
        ],
        out_specs=pl.BlockSpec((tm, d), lambda i, f: (i, 0)),
        out_shape=jax.ShapeDtypeStruct((n, d), F32),
        scratch_shapes=[pltpu.VMEM((tm, d), BF16)],
        compiler_params=_cparams("parallel", "arbitrary"),
        name="ffn",
    )(x, g.reshape(1, d), w_in, w_in, w_out)


def _ple_kernel(x_ref, p_ref, g_ref, wg_ref, wp_ref, gf_ref, o_ref, *, final_norm):
    x = x_ref[...]
    gate = jax.nn.sigmoid(_dot(_rms(x, g_ref[...]).astype(BF16), wg_ref[...]))
    proj = _dot(p_ref[...].astype(BF16), wp_ref[...])
    y = x + proj * gate
    if final_norm:
        y = _rms(y, gf_ref[...])
    o_ref[...] = y


def ple(x, p, g, w_gate, w_proj, g_final, final_norm, tm=512):
    n, d = x.shape
    dp = p.shape[1]
    return pl.pallas_call(
        functools.partial(_ple_kernel, final_norm=final_norm),
        grid=(n // tm,),
        in_specs=[
            pl.BlockSpec((tm, d), lambda i: (i, 0)),
            pl.BlockSpec((tm, dp), lambda i: (i, 0)),
            pl.BlockSpec((1, d), lambda i: (0, 0)),
            pl.BlockSpec((d, d), lambda i: (0, 0)),
            pl.BlockSpec((dp, d), lambda i: (0, 0)),
            pl.BlockSpec((1, d), lambda i: (0, 0)),
        ],
        out_specs=pl.BlockSpec((tm, d), lambda i: (i, 0)),
        out_shape=jax.ShapeDtypeStruct((n, d), F32),
        compiler_params=_cparams("parallel"),
        name="ple",
    )(x, p, g.reshape(1, d), w_gate, w_proj, g_final.reshape(1, d))


def t5_bucket(rel):
    nb = T5_BUCKETS // 2
    max_exact = nb // 2
    ret = jnp.where(rel > 0, nb, 0)
    n = jnp.abs(rel)
    nf = jnp.maximum(n, 1).astype(jnp.float32)
    large = max_exact + (jnp.log(nf / max_exact) / math.log(T5_MAX_DIST / max_exact)
                         * (nb - max_exact)).astype(jnp.int32)
    large = jnp.minimum(large, nb - 1)
    return ret + jnp.where(n < max_exact, n, large)


def _lane_tiles(x):
    return [x[:, j * LANES:(j + 1) * LANES] for j in range(x.shape[1] // LANES)]


def _diff_attn_kernel(lam_ref, zero_ref, q_ref, k_ref, v_ref, e_ref, g_ref, o_ref,
                      s1_ref, s2_ref, p1_ref, p2_ref, va_ref, *, tq, tk, seq, pad, width,
                      out_scale):
    i = pl.program_id(2)

    @pl.when(i == 0)
    def _():
        va_ref[:, :A_V_DIM] = v_ref[0]
        va_ref[:, A_V_DIM:] = jnp.ones((seq, A_V_DIM), BF16)

    q = q_ref[0]
    lane = lax.broadcasted_iota(jnp.int32, q.shape, 1)
    zero = jnp.zeros_like(q)
    q1 = jnp.where(lane < A_QK_DIM, q, zero)
    q2 = jnp.where(lane < A_QK_DIM, zero, q)

    def bias_tile(j):
        c = j * LANES // tk
        start = jnp.clip(c * tk - i * tq + pad, 0, width - tk) + (j * LANES - c * tk)
        return e_ref[0, :, pl.ds(pl.multiple_of(start, LANES), LANES)]

    m1 = m2 = None
    for c in range(seq // tk):
        cols = slice(c * tk, (c + 1) * tk)
        kc = k_ref[0, cols, :]
        x1 = _dot_nt(q1, kc)
        x2 = _dot_nt(q2, kc)
        off = zero_ref[0] + c * tk
        s1_ref[:, pl.ds(pl.multiple_of(off, LANES), tk)] = x1
        s2_ref[:, pl.ds(pl.multiple_of(off + LANES, LANES), tk)] = x2
        for jj, (t1, t2) in enumerate(zip(_lane_tiles(x1), _lane_tiles(x2))):
            bias = bias_tile(c * (tk // LANES) + jj)
            m1 = t1 + bias if m1 is None else jnp.maximum(m1, t1 + bias)
            m2 = t2 + bias if m2 is None else jnp.maximum(m2, t2 + bias)
    m1 = jnp.broadcast_to(jnp.max(m1, axis=-1, keepdims=True), (tq, LANES))
    m2 = jnp.broadcast_to(jnp.max(m2, axis=-1, keepdims=True), (tq, LANES))
    for j in range(seq // LANES):
        cols = slice(j * LANES, (j + 1) * LANES)
        shift = bias_tile(j)
        off = zero_ref[0] + j * LANES
        x1 = s1_ref[:, pl.ds(pl.multiple_of(off, LANES), LANES)]
        x2 = s2_ref[:, pl.ds(pl.multiple_of(off + LANES, LANES), LANES)]
        p1_ref[:, cols] = jnp.exp2(x1 + (shift - m1)).astype(BF16)
        p2_ref[:, LANES + j * LANES:LANES + (j + 1) * LANES] = jnp.exp2(
            x2 + (shift - m2)).astype(BF16)
    o1 = _dot(p1_ref[:, :seq], va_ref[...])
    o2 = _dot(p2_ref[:, LANES:LANES + seq], va_ref[...])
    o = (o1[:, :A_V_DIM] * (1.0 / o1[:, A_V_DIM:])
         - o2[:, :A_V_DIM] * (lam_ref[0] / o2[:, A_V_DIM:]))
    o = _rms(o, g_ref[...]) * out_scale
    o_ref[0] = o.astype(o_ref.dtype)


def _band_kernel(w_ref, o_ref, *, rows, cols):
    n = rows + cols
    x = jnp.broadcast_to(w_ref[0], (rows, n))
    y = pltpu.roll(x, n - (rows - 1), 1, stride=1, stride_axis=0)
    o_ref[0] = y[:, :cols]


def _toeplitz(w, rows, cols):
    heads, n = w.shape
    assert n == rows + cols and n % LANES == 0
    return pl.pallas_call(
        functools.partial(_band_kernel, rows=rows, cols=cols),
        grid=(heads,),
        in_specs=[pl.BlockSpec((1, 1, n), lambda h: (h, 0, 0))],
        out_specs=pl.BlockSpec((1, rows, cols), lambda h: (h, 0, 0)),
        out_shape=jax.ShapeDtypeStruct((heads, rows, cols), F32),
        compiler_params=_cparams("parallel"),
        name="toeplitz_band",
    )(w.reshape(heads, 1, n))


def diff_attention(z, lam, rel_table, subln_g, lam_init, tq=512, tk=512):
    b, seq, _ = z.shape
    pad = tk + T5_MAX_DIST
    width = 2 * tk + tq + 2 * T5_MAX_DIST
    cols = width + LANES
    rel = jnp.arange(tq + cols, dtype=jnp.int32) - (tq - 1) - pad
    onehot = (t5_bucket(rel)[:, None] == jnp.arange(T5_BUCKETS)[None, :]).astype(F32)
    diag = jnp.dot(onehot, rel_table.astype(F32), precision=lax.Precision.HIGHEST)
    band = _toeplitz(diag.T * LOG2E, tq, cols)
    hq = A_WIDTH // LANES
    return pl.pallas_call(
        functools.partial(_diff_attn_kernel, tq=tq, tk=tk, seq=seq, pad=pad, width=width,
                          out_scale=1.0 - lam_init),
        grid=(b, A_HEADS, seq // tq),
        in_specs=[
            pl.BlockSpec(memory_space=pltpu.SMEM),
            pl.BlockSpec(memory_space=pltpu.SMEM),
            pl.BlockSpec((1, tq, LANES), lambda bi, h, i: (bi, i, h)),
            pl.BlockSpec((1, seq, LANES), lambda bi, h, i: (bi, 0, hq + h)),
            pl.BlockSpec((1, seq, LANES), lambda bi, h, i: (bi, 0, 2 * hq + h)),
            pl.BlockSpec((1, tq, cols), lambda bi, h, i: (h, 0, 0)),
            pl.BlockSpec((1, A_V_DIM), lambda bi, h, i: (0, 0)),
        ],
        out_specs=pl.BlockSpec((1, tq, LANES), lambda bi, h, i: (bi, i, h)),
        out_shape=jax.ShapeDtypeStruct((b, seq, A_WIDTH), BF16),
        scratch_shapes=[pltpu.VMEM((tq, seq + LANES), F32), pltpu.VMEM((tq, seq + 3 * LANES), F32),
                        pltpu.VMEM((tq, seq + 3 * LANES), BF16),
                        pltpu.VMEM((tq, seq + 3 * LANES), BF16),
                        pltpu.VMEM((seq, 2 * A_V_DIM), BF16)],
        compiler_params=_cparams("parallel", "parallel", "arbitrary"),
        name="diff_attention",
    )(lam.reshape(1).astype(F32), jnp.zeros((1,), jnp.int32), z, z, z, band,
      subln_g.reshape(1, A_V_DIM))


def _na_kernel(q_ref, k_ref, v_ref, nb_ref, o_ref, *, rows):
    def row_group(gi, carry):
        geom = []
        for u in range(NA_UNROLL):
            r = gi * NA_UNROLL + u
            r_start = jnp.clip(r - NA_ROWS // 2, 0, rows - NA_ROWS)
            geom.append((pl.multiple_of(r * GRID_W, GRID_W),
                         pl.multiple_of(r_start * GRID_W, GRID_W), r - r_start))
        scores = [_dot_nt(q_ref[0, pl.ds(q0, GRID_W), :], k_ref[0, pl.ds(k0, NA_KEYS), :])
                  + nb_ref[0, var] for q0, k0, var in geom]
        probs = []
        for s in scores:
            p = jnp.exp2(s - jnp.max(s, axis=-1, keepdims=True))
            probs.append((p * (1.0 / jnp.sum(p, axis=-1, keepdims=True))).astype(BF16))
        for p, (q0, k0, _) in zip(probs, geom):
            o = _dot(p, v_ref[0, pl.ds(k0, NA_KEYS), :])
            o_ref[0, pl.ds(q0, GRID_W), :] = o.astype(o_ref.dtype)
        return carry

    lax.fori_loop(0, rows // NA_UNROLL, row_group, 0)


def _na_bias(na_table):
    var = np.arange(NA_ROWS)[:, None, None]
    kr = np.arange(NA_ROWS)[None, :, None]
    row_sel = (kr - var + (NA_ROWS - 1) == np.arange(2 * NA_ROWS - 1)).astype(np.float32)
    c = np.arange(GRID_W)[:, None, None]
    kc = np.arange(GRID_W)[None, :, None]
    col_sel = (kc - c + (NA_COLS - 1) == np.arange(2 * NA_COLS - 1)).astype(np.float32)
    c_start = np.clip(c - NA_COLS // 2, 0, GRID_W - NA_COLS)
    valid = ((kc >= c_start) & (kc < c_start + NA_COLS))[None, None, :, None, :, 0]
    bias = jnp.einsum('vka,hab,cqb->hvckq', row_sel, na_table.astype(F32), col_sel,
                      precision=lax.Precision.HIGHEST)
    bias = jnp.where(valid, bias * LOG2E, MASK_VALUE)
    return bias.reshape(B_HEADS, NA_ROWS, GRID_W, NA_KEYS)


def neighbourhood_attention(z, na_table):
    b, seq, _ = z.shape
    rows = seq // GRID_W
    assert rows >= NA_ROWS and rows % NA_UNROLL == 0
    hq = B_WIDTH // LANES
    base = 3 * A_WIDTH // LANES
    return pl.pallas_call(
        functools.partial(_na_kernel, rows=rows),
        grid=(b, B_HEADS),
        in_specs=[
            pl.BlockSpec((1, seq, LANES), lambda bi, h: (bi, 0, base + h)),
            pl.BlockSpec((1, seq, LANES), lambda bi, h: (bi, 0, base + hq + h)),
            pl.BlockSpec((1, seq, LANES), lambda bi, h: (bi, 0, base + 2 * hq + h)),
            pl.BlockSpec((1, NA_ROWS, GRID_W, NA_KEYS), lambda bi, h: (h, 0, 0, 0)),
        ],
        out_specs=pl.BlockSpec((1, seq, LANES), lambda bi, h: (bi, 0, h)),
        out_shape=jax.ShapeDtypeStruct((b, seq, B_WIDTH), BF16),
        compiler_params=_cparams("parallel", "parallel"),
        name="neighbourhood_attention",
    )(z, z, z, _na_bias(na_table))


HALO = 16
N_LAGS = 2 * POLY - 1


def _hyena_in_kernel(x_ref, xp_ref, xq_ref, g_ref, w0_ref, w1_ref, wv_ref,
                     c0_ref, c1_ref, cv_ref, b0_ref, b1_ref, bv_ref,
                     x0_out, vv_out, xn_ref, *, tiles_per_seq):
    i = pl.program_id(0)
    tm = x_ref.shape[0]
    hk = tm // POLY

    @pl.when(pl.program_id(1) == 0)
    def _():
        g = g_ref[...]
        xn = _rms(x_ref[...], g).astype(BF16)
        xn_ref[HALO:HALO + tm, :] = lax.dot_general(
            _phase_permutation(tm), xn, (((0,), (0,)), ((), ())),
            preferred_element_type=F32).astype(BF16)
        first = (i % tiles_per_seq) == 0
        last = (i % tiles_per_seq) == tiles_per_seq - 1
        xp = _rms(xp_ref[...], g)
        xq = _rms(xq_ref[...], g)
        xn_ref[0:HALO, :] = jnp.where(first, 0.0, xp).astype(BF16)
        xn_ref[HALO + tm:2 * HALO + tm, :] = jnp.where(last, 0.0, xq).astype(BF16)

    xn = xn_ref[...]
    row = lax.broadcasted_iota(jnp.int32, (hk, w0_ref.shape[1]), 0)

    def conv(w_ref, c_ref, b_ref):
        z = _dot(xn, w_ref[...])
        zp = [z[HALO + p * hk:HALO + (p + 1) * hk] for p in range(POLY)]
        before = jnp.where(row == 0, z[HALO - 1:HALO], pltpu.roll(zp[POLY - 1], 1, 0))
        after = jnp.where(row == hk - 1, z[HALO + tm:HALO + tm + 1], pltpu.roll(zp[0], hk - 1, 0))
        ext = [before] + zp + [after]
        c0, c1, c2, bias = c_ref[0:1, :], c_ref[1:2, :], c_ref[2:3, :], b_ref[...]
        return [ext[p] * c0 + ext[p + 1] * c1 + ext[p + 2] * c2 + bias for p in range(POLY)]

    x0 = conv(w0_ref, c0_ref, b0_ref)
    x1 = conv(w1_ref, c1_ref, b1_ref)
    v = conv(wv_ref, cv_ref, bv_ref)
    for p in range(POLY):
        x0_out[0, p] = x0[p].astype(x0_out.dtype)
        vv_out[0, p] = (v[p] * x1[p]).astype(vv_out.dtype)


def hyena_in(h, seq, g, w_in, conv_w, conv_b, tc=1024):
    n, d = h.shape
    tm = SPLIT_CHUNK
    nc = C_WIDTH // tc
    blocks = tm // HALO
    last_block = n // HALO - 1
    wspec = lambda o: pl.BlockSpec((d, tc), lambda i, c: (0, o * nc + c))
    cspec = lambda o: pl.BlockSpec((3, tc), lambda i, c: (0, o * nc + c))
    bspec = lambda o: pl.BlockSpec((1, tc), lambda i, c: (0, o * nc + c))
    cb = conv_b.reshape(1, -1)
    out = jax.ShapeDtypeStruct((n // tm, POLY, tm // POLY, C_WIDTH), BF16)
    ospec = pl.BlockSpec((1, POLY, tm // POLY, tc), lambda i, c: (i, 0, 0, c))
    return pl.pallas_call(
        functools.partial(_hyena_in_kernel, tiles_per_seq=seq // tm),
        grid=(n // tm, nc),
        in_specs=[
            pl.BlockSpec((tm, d), lambda i, c: (i, 0)),
            pl.BlockSpec((HALO, d), lambda i, c: (jnp.maximum(i * blocks - 1, 0), 0)),
            pl.BlockSpec((HALO, d), lambda i, c: (jnp.minimum((i + 1) * blocks, last_block), 0)),
            pl.BlockSpec((1, d), lambda i, c: (0, 0)),
            wspec(0), wspec(1), wspec(2), cspec(0), cspec(1), cspec(2),
            bspec(0), bspec(1), bspec(2),
        ],
        out_specs=[ospec, ospec],
        out_shape=[out, out],
        scratch_shapes=[pltpu.VMEM((tm + 2 * HALO, d), BF16)],
        compiler_params=_cparams("parallel", "arbitrary"),
        name="hyena_in",
    )(h, h, h, g.reshape(1, d), w_in, w_in, w_in, conv_w, conv_w, conv_w, cb, cb, cb)


def _filter_kernel(hd_ref, wf_ref, wb_ref, dl_ref, taps_ref, *, tl, seq):
    hd = hd_ref[...].astype(BF16)
    lane_group = lax.broadcasted_iota(jnp.int32, hd.shape, 1) // FILTER_HIDDEN
    zero = jnp.zeros_like(hd)
    m = pl.program_id(0) * tl + lax.broadcasted_iota(jnp.int32, taps_ref.shape[1:], 0)
    dl = dl_ref[...]

    def tap(w_ref, shift, r):
        token = (POLY * (m - 1 + shift) + r).astype(F32)
        x = jnp.where(lane_group == shift * POLY + r, hd, zero)
        return _dot(x, w_ref[...]) * jnp.exp(token * (-1.0 / (seq - 1)) * dl)

    for d in range(-POLY + 1, POLY):
        if d == 0:
            c = tap(wf_ref, 1, 0)
            a = jnp.where(m == 0, 0.0, tap(wb_ref, 1, 0))
            total, diff = c + a, c - a
        elif d > 0:
            c = tap(wf_ref, 1, d)
            a = tap(wb_ref, 0, POLY - d)
            total, diff = c + a, c - a
        else:
            c = tap(wf_ref, 0, POLY + d)
            a = tap(wb_ref, 1, -d)
            total, diff = c + a, c - jnp.where(m == 0, -a, a)
        taps_ref[2 * (d + POLY - 1)] = total.astype(taps_ref.dtype)
        taps_ref[2 * (d + POLY - 1) + 1] = diff.astype(taps_ref.dtype)


def hyena_filter_taps(hdn, w3, deltas, tl=512, tc=512):
    seq, hidden = hdn.shape
    sub = seq // POLY
    tl = min(tl, sub)
    nc = C_WIDTH // tc
    cur = hdn.reshape(sub, POLY * hidden)
    prev = jnp.concatenate([jnp.zeros_like(cur[:1]), cur[:-1]], axis=0)
    hd = jnp.concatenate([prev, cur], axis=1)
    w3s = jnp.tile(w3, (2 * POLY, 1))
    return pl.pallas_call(
        functools.partial(_filter_kernel, tl=tl, seq=seq),
        grid=(sub // tl, nc),
        in_specs=[
            pl.BlockSpec((tl, 2 * POLY * hidden), lambda l, c: (l, 0)),
            pl.BlockSpec((2 * POLY * hidden, tc), lambda l, c: (0, c)),
            pl.BlockSpec((2 * POLY * hidden, tc), lambda l, c: (0, nc + c)),
            pl.BlockSpec((1, tc), lambda l, c: (0, c)),
        ],
        out_specs=pl.BlockSpec((2 * N_LAGS, tl, tc), lambda l, c: (0, l, c)),
        out_shape=jax.ShapeDtypeStruct((2 * N_LAGS, sub, C_WIDTH), BF16),
        compiler_params=_cparams("parallel", "parallel"),
        name="hyena_filter_taps",
    )(hd, w3s, w3s, deltas.reshape(1, C_WIDTH))


NYQ_ROWS = 16


def _filter_dft_kernel(m_ref, taps_ref, coef_ref, nyq_ref, *, tf, sub):
    fi = pl.program_id(1)
    cos_m, sin_m = m_ref[0], m_ref[1]
    row0 = (fi * tf + lax.broadcasted_iota(jnp.int32, coef_ref.shape[1:], 0)) == 0
    wt = jnp.where(row0, 0.5 / sub, 1.0 / sub)
    for d in range(N_LAGS):
        coef_ref[2 * d] = _dot(cos_m, taps_ref[2 * d]) * wt
        coef_ref[2 * d + 1] = jnp.where(row0, 0.0, _dot(sin_m, taps_ref[2 * d + 1]) * wt)

    @pl.when(fi == 0)
    def _():
        for d in range(N_LAGS):
            nyq_ref[d] = _dot(sin_m[0:NYQ_ROWS, :], taps_ref[2 * d]) * (0.5 / sub)


def hyena_filter_dft(mmat, taps, tf=256, tc=512):
    sub = mmat.shape[1]
    tf = min(tf, sub)
    return pl.pallas_call(
        functools.partial(_filter_dft_kernel, tf=tf, sub=sub),
        grid=(C_WIDTH // tc, sub // tf),
        in_specs=[pl.BlockSpec((2, tf, sub), lambda c, f: (0, f, 0)),
                  pl.BlockSpec((2 * N_LAGS, sub, tc), lambda c, f: (0, 0, c))],
        out_specs=[pl.BlockSpec((2 * N_LAGS, tf, tc), lambda c, f: (0, f, c)),
                   pl.BlockSpec((N_LAGS, NYQ_ROWS, tc), lambda c, f: (0, 0, c))],
        out_shape=[jax.ShapeDtypeStruct((2 * N_LAGS, sub, C_WIDTH), F32),
                   jax.ShapeDtypeStruct((N_LAGS, NYQ_ROWS, C_WIDTH), F32)],
        compiler_params=_cparams("parallel", "arbitrary"),
        name="hyena_filter_dft",
    )(mmat, taps)


def _dft_fwd_kernel(m_ref, *refs, tf):
    v_refs, (coef_ref, nyq_ref, z_ref) = refs[:POLY], refs[POLY:]
    sub, tc = m_ref.shape[2], z_ref.shape[-1]
    m = m_ref[...].reshape(2 * tf, sub)
    spectra = [_dot(m, v_ref[0, :, 0].reshape(sub, tc)) for v_ref in v_refs]
    a_re = [a[:tf] for a in spectra]
    a_im = [a[tf:] for a in spectra]
    row0 = (pl.program_id(2) * tf + lax.broadcasted_iota(jnp.int32, (tf, tc), 0)) == 0
    h_re = [coef_ref[2 * d] for d in range(N_LAGS)]
    h_im = [coef_ref[2 * d + 1] for d in range(N_LAGS)]
    h_ny = [jnp.where(row0, nyq_ref[d, 0:1, :], h_re[d]) for d in range(N_LAGS)]
    for p in range(POLY):
        lag = [p - r + POLY - 1 for r in range(POLY)]
        z_re = sum(h_re[lag[r]] * a_re[r] - h_im[lag[r]] * a_im[r] for r in range(POLY))
        z_im = sum(h_ny[lag[r]] * a_im[r] + h_im[lag[r]] * a_re[r] for r in range(POLY))
        z_ref[0, p, 0] = z_re.astype(z_ref.dtype)
        z_ref[0, p, 1] = z_im.astype(z_ref.dtype)


def hyena_dft_fwd(mmat, vv, coefs, nyq, tf=256, tc=512):
    b, nck, _, hk, _ = vv.shape
    sub = nck * hk
    tf = min(tf, sub)
    v_spec = lambda p: pl.BlockSpec((1, nck, 1, hk, tc), lambda bi, c, f: (bi, 0, p, 0, c))
    return pl.pallas_call(
        functools.partial(_dft_fwd_kernel, tf=tf),
        grid=(b, C_WIDTH // tc, sub // tf),
        in_specs=[pl.BlockSpec((2, tf, sub), lambda bi, c, f: (0, f, 0))]
                 + [v_spec(p) for p in range(POLY)]
                 + [pl.BlockSpec((2 * N_LAGS, tf, tc), lambda bi, c, f: (0, f, c)),
                    pl.BlockSpec((N_LAGS, NYQ_ROWS, tc), lambda bi, c, f: (0, 0, c))],
        out_specs=pl.BlockSpec((1, POLY, 2, tf, tc), lambda bi, c, f: (bi, 0, 0, f, c)),
        out_shape=jax.ShapeDtypeStruct((b, POLY, 2, sub, C_WIDTH), BF16),
        compiler_params=_cparams("parallel", "parallel", "parallel"),
        name="hyena_dft_fwd",
    )(mmat, *([vv] * POLY), coefs, nyq)


def _dft_inv_kernel(g_ref, z_ref, vv_ref, x0_ref, skip_ref, u_ref):
    g = g_ref[...]
    nk, _, hk, tc = u_ref.shape[1:]
    for p in range(POLY):
        y = _dot(g, z_ref[0, p])
        vv = vv_ref[0, :, p].reshape(nk * hk, tc).astype(F32)
        x0 = x0_ref[0, :, p].reshape(nk * hk, tc).astype(F32)
        u = ((y + vv * skip_ref[...]) * x0).astype(u_ref.dtype)
        u_ref[0, :, p] = u.reshape(nk, hk, tc)


def hyena_dft_inv(gmat, zs, vv, x0, skip, tt=1024, tc=512):
    b, nck, _, hk, _ = vv.shape
    sub = nck * hk
    tt = min(tt, sub)
    io_spec = pl.BlockSpec((1, tt // hk, POLY, hk, tc), lambda bi, c, t: (bi, t, 0, 0, c))
    return pl.pallas_call(
        _dft_inv_kernel,
        grid=(b, C_WIDTH // tc, sub // tt),
        in_specs=[
            pl.BlockSpec((tt, 2 * sub), lambda bi, c, t: (t, 0)),
            pl.BlockSpec((1, POLY, 2 * sub, tc), lambda bi, c, t: (bi, 0, 0, c)),
            io_spec,
            io_spec,
            pl.BlockSpec((1, tc), lambda bi, c, t: (0, c)),
        ],
        out_specs=io_spec,
        out_shape=jax.ShapeDtypeStruct(vv.shape, BF16),
        compiler_params=_cparams("parallel", "parallel", "parallel"),
        name="hyena_dft_inv",
    )(gmat, zs, vv, x0, skip.reshape(1, C_WIDTH))


def _dft_matrices(sub):
    idx = jnp.arange(sub, dtype=jnp.int32)
    ang = ((idx[:, None] * idx[None, :]) % (2 * sub)).astype(F32) * (math.pi / sub)
    sin_part = (-jnp.sin(ang)).at[0].set((1 - 2 * (idx % 2)).astype(F32))
    mmat = jnp.stack([jnp.cos(ang), sin_part]).astype(BF16)
    return mmat, mmat.reshape(2 * sub, sub).T


def _filter_hidden(seq, w1, b1, freq, w2, b2):
    t = jnp.linspace(0.0, 1.0, seq, dtype=F32)[:, None]
    bands = (FILTER_EMB - 1) // 2
    w = 2.0 * math.pi * jnp.arange(seq, dtype=F32) / seq
    f = jnp.linspace(1e-4, bands - 1, bands, dtype=F32)
    ang = w[:, None] * f[None, :]
    feats = jnp.concatenate([t, jnp.cos(ang), -jnp.sin(ang)], axis=-1)
    hdn = jnp.sin(freq[0] * (feats @ w1 + b1))
    return jnp.sin(freq[1] * (hdn @ w2 + b2))


def hyena_mixer(h, batch, seq, g, w_in, conv_w, conv_b, w1, b1, freq, w2, b2, w3, skip, w_out):
    sub = seq // POLY
    split_shape = (batch, seq // SPLIT_CHUNK, POLY, SPLIT_CHUNK // POLY, C_WIDTH)
    x0, vv = (a.reshape(split_shape) for a in hyena_in(h, seq, g, w_in, conv_w, conv_b))
    hdn = _filter_hidden(seq, w1, b1, freq, w2, b2)
    min_decay = math.log(FILTER_TARGET) / FAST_DECAY_PCT
    max_decay = math.log(FILTER_TARGET) / SLOW_DECAY_PCT
    deltas = jnp.abs(jnp.linspace(min_decay, max_decay, C_WIDTH, dtype=F32))
    taps = hyena_filter_taps(hdn, w3, deltas)
    mmat, gmat = _dft_matrices(sub)
    coefs, nyq = hyena_filter_dft(mmat, taps)
    zs = hyena_dft_fwd(mmat, vv, coefs, nyq).reshape(batch, POLY, 2 * sub, C_WIDTH)
    u = hyena_dft_inv(gmat, zs, vv, x0, skip)
    u = u.reshape(batch * seq // SPLIT_CHUNK, POLY, SPLIT_CHUNK // POLY, C_WIDTH)
    return proj_residual([u], [w_out], h, interleave=True, tm=SPLIT_CHUNK)


def even_mixer(h, batch, seq, g, w_in, w_out, lam_vec, subln_g, na_table, rel_table, layer_idx):
    col_scale = np.ones((6, A_WIDTH), np.float32)
    col_scale[0] = A_QK_DIM ** -0.5 * LOG2E
    col_scale[3] = B_HEAD_DIM ** -0.5 * LOG2E
    z = norm_matmul(h, g, w_in, jnp.asarray(col_scale.reshape(-1)), BF16)
    z = z.reshape(batch, seq, w_in.shape[1])
    lam_init = 0.8 - 0.6 * math.exp(-0.3 * layer_idx)
    lv = lam_vec.astype(F32)
    lam = jnp.exp(jnp.sum(lv[0] * lv[1])) - jnp.exp(jnp.sum(lv[2] * lv[3])) + lam_init
    oa = diff_attention(z, lam, rel_table, subln_g, lam_init)
    ob = neighbourhood_attention(z, na_table)
    n = batch * seq
    return proj_residual([oa.reshape(n, A_WIDTH), ob.reshape(n, B_WIDTH)],
                         [w_out[:A_WIDTH], w_out[A_WIDTH:]], h)


def _trunk(x, p, W):
    batch, seq, d = x.shape
    depth = p.shape[0]
    h = x.reshape(batch * seq, d)
    for i in range(depth):
        j = i // 2
        if i % 2 == 0:
            h = even_mixer(h, batch, seq, W['norm_mix'][i], W['ab_w_in'][j], W['ab_w_out'][j],
                           W['diff_lambda'][j], W['diff_subln'][j], W['na_bias'][j],
                           W['rel_bias_table'], i)
        else:
            h = hyena_mixer(h, batch, seq, W['norm_mix'][i], W['c_w_in'][j], W['c_conv_w'][j],
                            W['c_conv_b'][j], W['c_filt_w1'][j], W['c_filt_b1'][j],
                            W['c_filt_freq'][j], W['c_filt_w2'][j], W['c_filt_b2'][j],
                            W['c_filt_w3'][j], W['c_skip'][j], W['c_w_out'][j])
        h = ffn(h, W['norm_ffn'][i], W['ffn_w_in'][i], W['ffn_w_out'][i])
        h = ple(h, p[i].reshape(batch * seq, -1), W['norm_ple'][i], W['ple_w_gate'][i],
                W['ple_w_proj'][i], W['final_norm'], final_norm=(i == depth - 1))
    return h.reshape(batch, seq, d)


def kernel(x_prompt, x_sample, p_prompt, p_sample, rel_bias_table, norm_mix, norm_ffn, norm_ple, final_norm, ab_w_in, ab_w_out, diff_lambda, diff_subln, na_bias, c_w_in, c_conv_w, c_conv_b, c_filt_w1, c_filt_b1, c_filt_freq, c_filt_w2, c_filt_b2, c_filt_w3, c_skip, c_w_out, ffn_w_in, ffn_w_out, ple_w_proj, ple_w_gate):
    bf = lambda w: w.astype(BF16)
    W = dict(rel_bias_table=rel_bias_table, norm_mix=norm_mix, norm_ffn=norm_ffn,
             norm_ple=norm_ple, final_norm=final_norm, ab_w_in=bf(ab_w_in), ab_w_out=bf(ab_w_out),
             diff_lambda=diff_lambda, diff_subln=diff_subln, na_bias=na_bias,
             c_w_in=bf(c_w_in), c_conv_w=c_conv_w, c_conv_b=c_conv_b, c_filt_w1=c_filt_w1,
             c_filt_b1=c_filt_b1, c_filt_freq=c_filt_freq, c_filt_w2=c_filt_w2,
             c_filt_b2=c_filt_b2, c_filt_w3=bf(c_filt_w3), c_skip=c_skip, c_w_out=bf(c_w_out),
             ffn_w_in=bf(ffn_w_in), ffn_w_out=bf(ffn_w_out), ple_w_proj=bf(ple_w_proj),
             ple_w_gate=bf(ple_w_gate))
    return (_trunk(x_prompt, p_prompt, W), _trunk(x_sample, p_sample, W))
```

```python
import functools
import math

import numpy as np
import jax
import jax.numpy as jnp
from jax import lax
from jax.experimental import pallas as pl
from jax.experimental.pallas import tpu as pltpu

F32 = jnp.float32
BF16 = jnp.bfloat16

D_MODEL = 2048
GRID_W = 64
A_HEADS = 8
A_QK_DIM = 64
A_V_DIM = 2 * A_QK_DIM
A_WIDTH = A_HEADS * A_V_DIM
T5_BUCKETS = 32
T5_MAX_DIST = 128
B_HEADS = 8
B_HEAD_DIM = 128
B_WIDTH = B_HEADS * B_HEAD_DIM
NA_ROWS = 8
NA_COLS = 16
C_WIDTH = D_MODEL
FILTER_EMB = 33
FILTER_HIDDEN = 64
FILTER_TARGET = 1e-2
FAST_DECAY_PCT = 0.3
SLOW_DECAY_PCT = 1.5
EPS = 1e-6

LANES = 128
NA_KEYS = NA_ROWS * GRID_W
MASK_VALUE = -1e30
LOG2E = math.log2(math.e)
POLY = 4
SPLIT_CHUNK = 512
NA_UNROLL = 32
VMEM_LIMIT = 56 * 1024 * 1024


def _cparams(*sem):
    return pltpu.CompilerParams(dimension_semantics=sem, vmem_limit_bytes=VMEM_LIMIT)


def _rms(x, g):
    ms = jnp.mean(x * x, axis=-1, keepdims=True)
    return x * lax.rsqrt(ms + EPS) * g


def _dot(a, b):
    return jnp.dot(a, b, preferred_element_type=F32)


def _dot_nt(a, b):
    return lax.dot_general(a, b, (((1,), (1,)), ((), ())), preferred_element_type=F32)


def _norm_matmul_kernel(x_ref, g_ref, w_ref, cs_ref, o_ref, xn_ref):
    @pl.when(pl.program_id(1) == 0)
    def _():
        xn_ref[...] = _rms(x_ref[...], g_ref[...]).astype(BF16)

    o_ref[...] = (_dot(xn_ref[...], w_ref[...]) * cs_ref[...]).astype(o_ref.dtype)


def norm_matmul(x, g, w, col_scale, out_dtype, tm=1024, tn=1024):
    n, d = x.shape
    d_out = w.shape[1]
    return pl.pallas_call(
        _norm_matmul_kernel,
        grid=(n // tm, d_out // tn),
        in_specs=[
            pl.BlockSpec((tm, d), lambda i, j: (i, 0)),
            pl.BlockSpec((1, d), lambda i, j: (0, 0)),
            pl.BlockSpec((d, tn), lambda i, j: (0, j)),
            pl.BlockSpec((1, tn), lambda i, j: (0, j)),
        ],
        out_specs=pl.BlockSpec((tm, tn), lambda i, j: (i, j)),
        out_shape=jax.ShapeDtypeStruct((n, d_out), out_dtype),
        scratch_shapes=[pltpu.VMEM((tm, d), BF16)],
        compiler_params=_cparams("parallel", "arbitrary"),
        name="norm_matmul",
    )(x, g.reshape(1, d), w, col_scale.reshape(1, d_out))


def _phase_permutation(n):
    t = lax.broadcasted_iota(jnp.int32, (n, n), 0)
    j = lax.broadcasted_iota(jnp.int32, (n, n), 1)
    return jnp.where(j == (t % POLY) * (n // POLY) + t // POLY, 1.0, 0.0).astype(BF16)


def _proj_res_kernel(*refs, n_in, interleave):
    a_refs = refs[:n_in]
    w_refs = refs[n_in:2 * n_in]
    res_ref, o_ref = refs[2 * n_in], refs[2 * n_in + 1]
    acc = res_ref[...]
    for a_ref, w_ref in zip(a_refs, w_refs):
        if interleave:
            n = POLY * a_ref.shape[2]
            a = _dot(_phase_permutation(n), a_ref[0].reshape(n, a_ref.shape[3])).astype(BF16)
        else:
            a = a_ref[...]
        acc = acc + _dot(a, w_ref[...])
    o_ref[...] = acc


def proj_residual(a_list, w_list, res, interleave=False, tm=512):
    n, d = res.shape
    n_in = len(a_list)
    if interleave:
        in_specs = [pl.BlockSpec((1, POLY, tm // POLY, a.shape[-1]), lambda i: (i, 0, 0, 0))
                    for a in a_list]
    else:
        in_specs = [pl.BlockSpec((tm, a.shape[1]), lambda i: (i, 0)) for a in a_list]
    in_specs += [pl.BlockSpec(w.shape, lambda i: (0, 0)) for w in w_list]
    in_specs += [pl.BlockSpec((tm, d), lambda i: (i, 0))]
    return pl.pallas_call(
        functools.partial(_proj_res_kernel, n_in=n_in, interleave=interleave),
        grid=(n // tm,),
        in_specs=in_specs,
        out_specs=pl.BlockSpec((tm, d), lambda i: (i, 0)),
        out_shape=jax.ShapeDtypeStruct((n, d), F32),
        compiler_params=_cparams("parallel"),
        name="proj_residual",
    )(*a_list, *w_list, res)


def _ffn_kernel(x_ref, g_ref, wg_ref, wu_ref, wo_ref, o_ref, xn_ref):
    @pl.when(pl.program_id(1) == 0)
    def _():
        x = x_ref[...]
        xn_ref[...] = _rms(x, g_ref[...]).astype(BF16)
        o_ref[...] = x

    xn = xn_ref[...]
    gate = _dot(xn, wg_ref[...])
    up = _dot(xn, wu_ref[...])
    act = (gate * jax.nn.sigmoid(gate) * up).astype(BF16)
    o_ref[...] += _dot(act, wo_ref[...])


def ffn(x, g, w_in, w_out, tm=1024, tf=512):
    n, d = x.shape
    d_ff = w_out.shape[0]
    nf = d_ff // tf
    return pl.pallas_call(
        _ffn_kernel,
        grid=(n // tm, nf),
        in_specs=[
            pl.BlockSpec((tm, d), lambda i, f: (i, 0)),
            pl.BlockSpec((1, d), lambda i, f: (0, 0)),
            pl.BlockSpec((d, tf), lambda i, f: (0, f)),
            pl.BlockSpec((d, tf), lambda i, f: (0, nf + f)),
            pl.BlockSpec((tf, d), lambda i, f: (f, 0)),
        ],
        out_specs=pl.BlockSpec((tm, d), lambda i, f: (i, 0)),
        out_shape=jax.ShapeDtypeStruct((n, d), F32),
        scratch_shapes=[pltpu.VMEM((tm, d), BF16)],
        compiler_params=_cparams("parallel", "arbitrary"),
        name="ffn",
    )(x, g.reshape(1, d), w_in, w_in, w_out)


def _ple_kernel(x_ref, p_ref, g_ref, wg_ref, wp_ref, gf_ref, o_ref, *, final_norm):
    x = x_ref[...]
    gate = jax.nn.sigmoid(_dot(_rms(x, g_ref[...]).astype(BF16), wg_ref[...]))
    proj = _dot(p_ref[...].astype(BF16), wp_ref[...])
    y = x + proj * gate
    if final_norm:
        y = _rms(y, gf_ref[...])
    o_ref[...] = y


def ple(x, p, g, w_gate, w_proj, g_final, final_norm, tm=512):
    n, d = x.shape
    dp = p.shape[1]
    return pl.pallas_call(
        functools.partial(_ple_kernel, final_norm=final_norm),
        grid=(n // tm,),
        in_specs=[
            pl.BlockSpec((tm, d), lambda i: (i, 0)),
            pl.BlockSpec((tm, dp), lambda i: (i, 0)),
            pl.BlockSpec((1, d), lambda i: (0, 0)),
            pl.BlockSpec((d, d), lambda i: (0, 0)),
            pl.BlockSpec((dp, d), lambda i: (0, 0)),
            pl.BlockSpec((1, d), lambda i: (0, 0)),
        ],
        out_specs=pl.BlockSpec((tm, d), lambda i: (i, 0)),
        out_shape=jax.ShapeDtypeStruct((n, d), F32),
        compiler_params=_cparams("parallel"),
        name="ple",
    )(x, p, g.reshape(1, d), w_gate, w_proj, g_final.reshape(1, d))


def t5_bucket(rel):
    nb = T5_BUCKETS // 2
    max_exact = nb // 2
    ret = jnp.where(rel > 0, nb, 0)
    n = jnp.abs(rel)
    nf = jnp.maximum(n, 1).astype(jnp.float32)
    large = max_exact + (jnp.log(nf / max_exact) / math.log(T5_MAX_DIST / max_exact)
                         * (nb - max_exact)).astype(jnp.int32)
    large = jnp.minimum(large, nb - 1)
    return ret + jnp.where(n < max_exact, n, large)


def _lane_tiles(x):
    return [x[:, j * LANES:(j + 1) * LANES] for j in range(x.shape[1] // LANES)]


def _diff_attn_kernel(lam_ref, q_ref, k_ref, v_ref, e_ref, g_ref, o_ref,
                      s1_ref, s2_ref, p1_ref, p2_ref, va_ref, *, tq, tk, seq, pad, width,
                      out_scale):
    i = pl.program_id(2)

    @pl.when(i == 0)
    def _():
        va_ref[:, :A_V_DIM] = v_ref[0]
        va_ref[:, A_V_DIM:] = jnp.ones((seq, A_V_DIM), BF16)

    q = q_ref[0]
    lane = lax.broadcasted_iota(jnp.int32, q.shape, 1)
    zero = jnp.zeros_like(q)
    q1 = jnp.where(lane < A_QK_DIM, q, zero)
    q2 = jnp.where(lane < A_QK_DIM, zero, q)

    def bias_tile(j):
        c = j * LANES // tk
        start = jnp.clip(c * tk - i * tq + pad, 0, width - tk) + (j * LANES - c * tk)
        return e_ref[0, :, pl.ds(pl.multiple_of(start, LANES), LANES)]

    m1 = m2 = None
    for c in range(seq // tk):
        cols = slice(c * tk, (c + 1) * tk)
        kc = k_ref[0, cols, :]
        x1 = _dot_nt(q1, kc)
        x2 = _dot_nt(q2, kc)
        s1_ref[:, cols] = x1
        s2_ref[:, cols] = x2
        for jj, (t1, t2) in enumerate(zip(_lane_tiles(x1), _lane_tiles(x2))):
            bias = bias_tile(c * (tk // LANES) + jj)
            m1 = t1 + bias if m1 is None else jnp.maximum(m1, t1 + bias)
            m2 = t2 + bias if m2 is None else jnp.maximum(m2, t2 + bias)
    m1 = jnp.broadcast_to(jnp.max(m1, axis=-1, keepdims=True), (tq, LANES))
    m2 = jnp.broadcast_to(jnp.max(m2, axis=-1, keepdims=True), (tq, LANES))
    for j in range(seq // LANES):
        cols = slice(j * LANES, (j + 1) * LANES)
        shift = bias_tile(j)
        p1_ref[:, cols] = jnp.exp2(s1_ref[:, cols] + (shift - m1)).astype(BF16)
        p2_ref[:, LANES + j * LANES:LANES + (j + 1) * LANES] = jnp.exp2(
            s2_ref[:, cols] + (shift - m2)).astype(BF16)
    o1 = _dot(p1_ref[:, :seq], va_ref[...])
    o2 = _dot(p2_ref[:, LANES:LANES + seq], va_ref[...])
    o = (o1[:, :A_V_DIM] * (1.0 / o1[:, A_V_DIM:])
         - o2[:, :A_V_DIM] * (lam_ref[0] / o2[:, A_V_DIM:]))
    o = _rms(o, g_ref[...]) * out_scale
    o_ref[0] = o.astype(o_ref.dtype)


def _band_kernel(w_ref, o_ref, *, rows, cols):
    n = rows + cols
    x = jnp.broadcast_to(w_ref[0], (rows, n))
    y = pltpu.roll(x, n - (rows - 1), 1, stride=1, stride_axis=0)
    o_ref[0] = y[:, :cols]


def _toeplitz(w, rows, cols):
    heads, n = w.shape
    assert n == rows + cols and n % LANES == 0
    return pl.pallas_call(
        functools.partial(_band_kernel, rows=rows, cols=cols),
        grid=(heads,),
        in_specs=[pl.BlockSpec((1, 1, n), lambda h: (h, 0, 0))],
        out_specs=pl.BlockSpec((1, rows, cols), lambda h: (h, 0, 0)),
        out_shape=jax.ShapeDtypeStruct((heads, rows, cols), F32),
        compiler_params=_cparams("parallel"),
        name="toeplitz_band",
    )(w.reshape(heads, 1, n))


def diff_attention(z, lam, rel_table, subln_g, lam_init, tq=512, tk=512):
    b, seq, _ = z.shape
    pad = tk + T5_MAX_DIST
    width = 2 * tk + tq + 2 * T5_MAX_DIST
    cols = width + LANES
    rel = jnp.arange(tq + cols, dtype=jnp.int32) - (tq - 1) - pad
    onehot = (t5_bucket(rel)[:, None] == jnp.arange(T5_BUCKETS)[None, :]).astype(F32)
    diag = jnp.dot(onehot, rel_table.astype(F32), precision=lax.Precision.HIGHEST)
    band = _toeplitz(diag.T * LOG2E, tq, cols)
    hq = A_WIDTH // LANES
    return pl.pallas_call(
        functools.partial(_diff_attn_kernel, tq=tq, tk=tk, seq=seq, pad=pad, width=width,
                          out_scale=1.0 - lam_init),
        grid=(b, A_HEADS, seq // tq),
        in_specs=[
            pl.BlockSpec(memory_space=pltpu.SMEM),
            pl.BlockSpec((1, tq, LANES), lambda bi, h, i: (bi, i, h)),
            pl.BlockSpec((1, seq, LANES), lambda bi, h, i: (bi, 0, hq + h)),
            pl.BlockSpec((1, seq, LANES), lambda bi, h, i: (bi, 0, 2 * hq + h)),
            pl.BlockSpec((1, tq, cols), lambda bi, h, i: (h, 0, 0)),
            pl.BlockSpec((1, A_V_DIM), lambda bi, h, i: (0, 0)),
        ],
        out_specs=pl.BlockSpec((1, tq, LANES), lambda bi, h, i: (bi, i, h)),
        out_shape=jax.ShapeDtypeStruct((b, seq, A_WIDTH), BF16),
        scratch_shapes=[pltpu.VMEM((tq, seq), F32), pltpu.VMEM((tq, seq), F32),
                        pltpu.VMEM((tq, seq + 3 * LANES), BF16),
                        pltpu.VMEM((tq, seq + 3 * LANES), BF16),
                        pltpu.VMEM((seq, 2 * A_V_DIM), BF16)],
        compiler_params=_cparams("parallel", "parallel", "arbitrary"),
        name="diff_attention",
    )(lam.reshape(1).astype(F32), z, z, z, band, subln_g.reshape(1, A_V_DIM))


def _na_kernel(q_ref, k_ref, v_ref, nb_ref, o_ref, *, rows):
    def row_group(gi, carry):
        geom = []
        for u in range(NA_UNROLL):
            r = gi * NA_UNROLL + u
            r_start = jnp.clip(r - NA_ROWS // 2, 0, rows - NA_ROWS)
            geom.append((pl.multiple_of(r * GRID_W, GRID_W),
                         pl.multiple_of(r_start * GRID_W, GRID_W), r - r_start))
        scores = [_dot_nt(q_ref[0, pl.ds(q0, GRID_W), :], k_ref[0, pl.ds(k0, NA_KEYS), :])
                  + nb_ref[0, var] for q0, k0, var in geom]
        probs = []
        for s in scores:
            p = jnp.exp2(s - jnp.max(s, axis=-1, keepdims=True))
            probs.append((p * (1.0 / jnp.sum(p, axis=-1, keepdims=True))).astype(BF16))
        for p, (q0, k0, _) in zip(probs, geom):
            o = _dot(p, v_ref[0, pl.ds(k0, NA_KEYS), :])
            o_ref[0, pl.ds(q0, GRID_W), :] = o.astype(o_ref.dtype)
        return carry

    lax.fori_loop(0, rows // NA_UNROLL, row_group, 0)


def _na_bias(na_table):
    var = np.arange(NA_ROWS)[:, None, None]
    kr = np.arange(NA_ROWS)[None, :, None]
    row_sel = (kr - var + (NA_ROWS - 1) == np.arange(2 * NA_ROWS - 1)).astype(np.float32)
    c = np.arange(GRID_W)[:, None, None]
    kc = np.arange(GRID_W)[None, :, None]
    col_sel = (kc - c + (NA_COLS - 1) == np.arange(2 * NA_COLS - 1)).astype(np.float32)
    c_start = np.clip(c - NA_COLS // 2, 0, GRID_W - NA_COLS)
    valid = ((kc >= c_start) & (kc < c_start + NA_COLS))[None, None, :, None, :, 0]
    bias = jnp.einsum('vka,hab,cqb->hvckq', row_sel, na_table.astype(F32), col_sel,
                      precision=lax.Precision.HIGHEST)
    bias = jnp.where(valid, bias * LOG2E, MASK_VALUE)
    return bias.reshape(B_HEADS, NA_ROWS, GRID_W, NA_KEYS)


def neighbourhood_attention(z, na_table):
    b, seq, _ = z.shape
    rows = seq // GRID_W
    assert rows >= NA_ROWS and rows % NA_UNROLL == 0
    hq = B_WIDTH // LANES
    base = 3 * A_WIDTH // LANES
    return pl.pallas_call(
        functools.partial(_na_kernel, rows=rows),
        grid=(b, B_HEADS),
        in_specs=[
            pl.BlockSpec((1, seq, LANES), lambda bi, h: (bi, 0, base + h)),
            pl.BlockSpec((1, seq, LANES), lambda bi, h: (bi, 0, base + hq + h)),
            pl.BlockSpec((1, seq, LANES), lambda bi, h: (bi, 0, base + 2 * hq + h)),
            pl.BlockSpec((1, NA_ROWS, GRID_W, NA_KEYS), lambda bi, h: (h, 0, 0, 0)),
        ],
        out_specs=pl.BlockSpec((1, seq, LANES), lambda bi, h: (bi, 0, h)),
        out_shape=jax.ShapeDtypeStruct((b, seq, B_WIDTH), BF16),
        compiler_params=_cparams("parallel", "parallel"),
        name="neighbourhood_attention",
    )(z, z, z, _na_bias(na_table))


HALO = 16
N_LAGS = 2 * POLY - 1


def _hyena_in_kernel(x_ref, xp_ref, xq_ref, g_ref, w0_ref, w1_ref, wv_ref,
                     c0_ref, c1_ref, cv_ref, b0_ref, b1_ref, bv_ref,
                     x0_out, vv_out, xn_ref, *, tiles_per_seq):
    i = pl.program_id(0)
    tm = x_ref.shape[0]
    hk = tm // POLY

    @pl.when(pl.program_id(1) == 0)
    def _():
        g = g_ref[...]
        xn = _rms(x_ref[...], g).astype(BF16)
        xn_ref[HALO:HALO + tm, :] = lax.dot_general(
            _phase_permutation(tm), xn, (((0,), (0,)), ((), ())),
            preferred_element_type=F32).astype(BF16)
        first = (i % tiles_per_seq) == 0
        last = (i % tiles_per_seq) == tiles_per_seq - 1
        xp = _rms(xp_ref[...], g)
        xq = _rms(xq_ref[...], g)
        xn_ref[0:HALO, :] = jnp.where(first, 0.0, xp).astype(BF16)
        xn_ref[HALO + tm:2 * HALO + tm, :] = jnp.where(last, 0.0, xq).astype(BF16)

    xn = xn_ref[...]
    row = lax.broadcasted_iota(jnp.int32, (hk, w0_ref.shape[1]), 0)

    def conv(w_ref, c_ref, b_ref):
        z = _dot(xn, w_ref[...])
        zp = [z[HALO + p * hk:HALO + (p + 1) * hk] for p in range(POLY)]
        before = jnp.where(row == 0, z[HALO - 1:HALO], pltpu.roll(zp[POLY - 1], 1, 0))
        after = jnp.where(row == hk - 1, z[HALO + tm:HALO + tm + 1], pltpu.roll(zp[0], hk - 1, 0))
        ext = [before] + zp + [after]
        c0, c1, c2, bias = c_ref[0:1, :], c_ref[1:2, :], c_ref[2:3, :], b_ref[...]
        return [ext[p] * c0 + ext[p + 1] * c1 + ext[p + 2] * c2 + bias for p in range(POLY)]

    x0 = conv(w0_ref, c0_ref, b0_ref)
    x1 = conv(w1_ref, c1_ref, b1_ref)
    v = conv(wv_ref, cv_ref, bv_ref)
    for p in range(POLY):
        x0_out[0, p] = x0[p].astype(x0_out.dtype)
        vv_out[0, p] = (v[p] * x1[p]).astype(vv_out.dtype)


def hyena_in(h, seq, g, w_in, conv_w, conv_b, tc=1024):
    n, d = h.shape
    tm = SPLIT_CHUNK
    nc = C_WIDTH // tc
    blocks = tm // HALO
    last_block = n // HALO - 1
    wspec = lambda o: pl.BlockSpec((d, tc), lambda i, c: (0, o * nc + c))
    cspec = lambda o: pl.BlockSpec((3, tc), lambda i, c: (0, o * nc + c))
    bspec = lambda o: pl.BlockSpec((1, tc), lambda i, c: (0, o * nc + c))
    cb = conv_b.reshape(1, -1)
    out = jax.ShapeDtypeStruct((n // tm, POLY, tm // POLY, C_WIDTH), BF16)
    ospec = pl.BlockSpec((1, POLY, tm // POLY, tc), lambda i, c: (i, 0, 0, c))
    return pl.pallas_call(
        functools.partial(_hyena_in_kernel, tiles_per_seq=seq // tm),
        grid=(n // tm, nc),
        in_specs=[
            pl.BlockSpec((tm, d), lambda i, c: (i, 0)),
            pl.BlockSpec((HALO, d), lambda i, c: (jnp.maximum(i * blocks - 1, 0), 0)),
            pl.BlockSpec((HALO, d), lambda i, c: (jnp.minimum((i + 1) * blocks, last_block), 0)),
            pl.BlockSpec((1, d), lambda i, c: (0, 0)),
            wspec(0), wspec(1), wspec(2), cspec(0), cspec(1), cspec(2),
            bspec(0), bspec(1), bspec(2),
        ],
        out_specs=[ospec, ospec],
        out_shape=[out, out],
        scratch_shapes=[pltpu.VMEM((tm + 2 * HALO, d), BF16)],
        compiler_params=_cparams("parallel", "arbitrary"),
        name="hyena_in",
    )(h, h, h, g.reshape(1, d), w_in, w_in, w_in, conv_w, conv_w, conv_w, cb, cb, cb)


def _filter_kernel(hd_ref, wf_ref, wb_ref, dl_ref, taps_ref, *, tl, seq):
    hd = hd_ref[...].astype(BF16)
    lane_group = lax.broadcasted_iota(jnp.int32, hd.shape, 1) // FILTER_HIDDEN
    zero = jnp.zeros_like(hd)
    m = pl.program_id(0) * tl + lax.broadcasted_iota(jnp.int32, taps_ref.shape[1:], 0)
    dl = dl_ref[...]

    def tap(w_ref, shift, r):
        token = (POLY * (m - 1 + shift) + r).astype(F32)
        x = jnp.where(lane_group == shift * POLY + r, hd, zero)
        return _dot(x, w_ref[...]) * jnp.exp(token * (-1.0 / (seq - 1)) * dl)

    for d in range(-POLY + 1, POLY):
        if d == 0:
            c = tap(wf_ref, 1, 0)
            a = jnp.where(m == 0, 0.0, tap(wb_ref, 1, 0))
            total, diff = c + a, c - a
        elif d > 0:
            c = tap(wf_ref, 1, d)
            a = tap(wb_ref, 0, POLY - d)
            total, diff = c + a, c - a
        else:
            c = tap(wf_ref, 0, POLY + d)
            a = tap(wb_ref, 1, -d)
            total, diff = c + a, c - jnp.where(m == 0, -a, a)
        taps_ref[2 * (d + POLY - 1)] = total.astype(taps_ref.dtype)
        taps_ref[2 * (d + POLY - 1) + 1] = diff.astype(taps_ref.dtype)


def hyena_filter_taps(hdn, w3, deltas, tl=512, tc=512):
    seq, hidden = hdn.shape
    sub = seq // POLY
    tl = min(tl, sub)
    nc = C_WIDTH // tc
    cur = hdn.reshape(sub, POLY * hidden)
    prev = jnp.concatenate([jnp.zeros_like(cur[:1]), cur[:-1]], axis=0)
    hd = jnp.concatenate([prev, cur], axis=1)
    w3s = jnp.tile(w3, (2 * POLY, 1))
    return pl.pallas_call(
        functools.partial(_filter_kernel, tl=tl, seq=seq),
        grid=(sub // tl, nc),
        in_specs=[
            pl.BlockSpec((tl, 2 * POLY * hidden), lambda l, c: (l, 0)),
            pl.BlockSpec((2 * POLY * hidden, tc), lambda l, c: (0, c)),
            pl.BlockSpec((2 * POLY * hidden, tc), lambda l, c: (0, nc + c)),
            pl.BlockSpec((1, tc), lambda l, c: (0, c)),
        ],
        out_specs=pl.BlockSpec((2 * N_LAGS, tl, tc), lambda l, c: (0, l, c)),
        out_shape=jax.ShapeDtypeStruct((2 * N_LAGS, sub, C_WIDTH), BF16),
        compiler_params=_cparams("parallel", "parallel"),
        name="hyena_filter_taps",
    )(hd, w3s, w3s, deltas.reshape(1, C_WIDTH))


NYQ_ROWS = 16


def _filter_dft_kernel(m_ref, taps_ref, coef_ref, nyq_ref, *, tf, sub):
    fi = pl.program_id(1)
    cos_m, sin_m = m_ref[0], m_ref[1]
    row0 = (fi * tf + lax.broadcasted_iota(jnp.int32, coef_ref.shape[1:], 0)) == 0
    wt = jnp.where(row0, 0.5 / sub, 1.0 / sub)
    for d in range(N_LAGS):
        coef_ref[2 * d] = _dot(cos_m, taps_ref[2 * d]) * wt
        coef_ref[2 * d + 1] = jnp.where(row0, 0.0, _dot(sin_m, taps_ref[2 * d + 1]) * wt)

    @pl.when(fi == 0)
    def _():
        for d in range(N_LAGS):
            nyq_ref[d] = _dot(sin_m[0:NYQ_ROWS, :], taps_ref[2 * d]) * (0.5 / sub)


def hyena_filter_dft(mmat, taps, tf=256, tc=512):
    sub = mmat.shape[1]
    tf = min(tf, sub)
    return pl.pallas_call(
        functools.partial(_filter_dft_kernel, tf=tf, sub=sub),
        grid=(C_WIDTH // tc, sub // tf),
        in_specs=[pl.BlockSpec((2, tf, sub), lambda c, f: (0, f, 0)),
                  pl.BlockSpec((2 * N_LAGS, sub, tc), lambda c, f: (0, 0, c))],
        out_specs=[pl.BlockSpec((2 * N_LAGS, tf, tc), lambda c, f: (0, f, c)),
                   pl.BlockSpec((N_LAGS, NYQ_ROWS, tc), lambda c, f: (0, 0, c))],
        out_shape=[jax.ShapeDtypeStruct((2 * N_LAGS, sub, C_WIDTH), F32),
                   jax.ShapeDtypeStruct((N_LAGS, NYQ_ROWS, C_WIDTH), F32)],
        compiler_params=_cparams("parallel", "arbitrary"),
        name="hyena_filter_dft",
    )(mmat, taps)


def _dft_fwd_kernel(m_ref, *refs, tf):
    v_refs, (coef_ref, nyq_ref, z_ref) = refs[:POLY], refs[POLY:]
    sub, tc = m_ref.shape[2], z_ref.shape[-1]
    m = m_ref[...].reshape(2 * tf, sub)
    spectra = [_dot(m, v_ref[0, :, 0].reshape(sub, tc)) for v_ref in v_refs]
    a_re = [a[:tf] for a in spectra]
    a_im = [a[tf:] for a in spectra]
    row0 = (pl.program_id(1) * tf + lax.broadcasted_iota(jnp.int32, (tf, tc), 0)) == 0
    h_re = [coef_ref[2 * d] for d in range(N_LAGS)]
    h_im = [coef_ref[2 * d + 1] for d in range(N_LAGS)]
    h_ny = [jnp.where(row0, nyq_ref[d, 0:1, :], h_re[d]) for d in range(N_LAGS)]
    for p in range(POLY):
        lag = [p - r + POLY - 1 for r in range(POLY)]
        z_re = sum(h_re[lag[r]] * a_re[r] - h_im[lag[r]] * a_im[r] for r in range(POLY))
        z_im = sum(h_ny[lag[r]] * a_im[r] + h_im[lag[r]] * a_re[r] for r in range(POLY))
        z_ref[0, p, 0] = z_re.astype(z_ref.dtype)
        z_ref[0, p, 1] = z_im.astype(z_ref.dtype)


def hyena_dft_fwd(mmat, vv, coefs, nyq, tf=256, tc=512):
    b, nck, _, hk, _ = vv.shape
    sub = nck * hk
    tf = min(tf, sub)
    v_spec = lambda p: pl.BlockSpec((1, nck, 1, hk, tc), lambda c, f, bi: (bi, 0, p, 0, c))
    return pl.pallas_call(
        functools.partial(_dft_fwd_kernel, tf=tf),
        grid=(C_WIDTH // tc, sub // tf, b),
        in_specs=[pl.BlockSpec((2, tf, sub), lambda c, f, bi: (0, f, 0))]
                 + [v_spec(p) for p in range(POLY)]
                 + [pl.BlockSpec((2 * N_LAGS, tf, tc), lambda c, f, bi: (0, f, c)),
                    pl.BlockSpec((N_LAGS, NYQ_ROWS, tc), lambda c, f, bi: (0, 0, c))],
        out_specs=pl.BlockSpec((1, POLY, 2, tf, tc), lambda c, f, bi: (bi, 0, 0, f, c)),
        out_shape=jax.ShapeDtypeStruct((b, POLY, 2, sub, C_WIDTH), BF16),
        compiler_params=_cparams("parallel", "parallel", "parallel"),
        name="hyena_dft_fwd",
    )(mmat, *([vv] * POLY), coefs, nyq)


def _dft_inv_kernel(g_ref, z_ref, vv_ref, x0_ref, skip_ref, u_ref):
    g = g_ref[...]
    nk, _, hk, tc = u_ref.shape[1:]
    for p in range(POLY):
        y = _dot(g, z_ref[0, p])
        vv = vv_ref[0, :, p].reshape(nk * hk, tc).astype(F32)
        x0 = x0_ref[0, :, p].reshape(nk * hk, tc).astype(F32)
        u = ((y + vv * skip_ref[...]) * x0).astype(u_ref.dtype)
        u_ref[0, :, p] = u.reshape(nk, hk, tc)


def hyena_dft_inv(gmat, zs, vv, x0, skip, tt=1024, tc=512):
    b, nck, _, hk, _ = vv.shape
    sub = nck * hk
    tt = min(tt, sub)
    io_spec = pl.BlockSpec((1, tt // hk, POLY, hk, tc), lambda bi, c, t: (bi, t, 0, 0, c))
    return pl.pallas_call(
        _dft_inv_kernel,
        grid=(b, C_WIDTH // tc, sub // tt),
        in_specs=[
            pl.BlockSpec((tt, 2 * sub), lambda bi, c, t: (t, 0)),
            pl.BlockSpec((1, POLY, 2 * sub, tc), lambda bi, c, t: (bi, 0, 0, c)),
            io_spec,
            io_spec,
            pl.BlockSpec((1, tc), lambda bi, c, t: (0, c)),
        ],
        out_specs=io_spec,
        out_shape=jax.ShapeDtypeStruct(vv.shape, BF16),
        compiler_params=_cparams("parallel", "parallel", "parallel"),
        name="hyena_dft_inv",
    )(gmat, zs, vv, x0, skip.reshape(1, C_WIDTH))


def _dft_matrices(sub):
    idx = jnp.arange(sub, dtype=jnp.int32)
    ang = ((idx[:, None] * idx[None, :]) % (2 * sub)).astype(F32) * (math.pi / sub)
    sin_part = (-jnp.sin(ang)).at[0].set((1 - 2 * (idx % 2)).astype(F32))
    mmat = jnp.stack([jnp.cos(ang), sin_part]).astype(BF16)
    return mmat, mmat.reshape(2 * sub, sub).T


def _filter_hidden(seq, w1, b1, freq, w2, b2):
    t = jnp.linspace(0.0, 1.0, seq, dtype=F32)[:, None]
    bands = (FILTER_EMB - 1) // 2
    w = 2.0 * math.pi * jnp.arange(seq, dtype=F32) / seq
    f = jnp.linspace(1e-4, bands - 1, bands, dtype=F32)
    ang = w[:, None] * f[None, :]
    feats = jnp.concatenate([t, jnp.cos(ang), -jnp.sin(ang)], axis=-1)
    hdn = jnp.sin(freq[0] * (feats @ w1 + b1))
    return jnp.sin(freq[1] * (hdn @ w2 + b2))


def hyena_mixer(h, batch, seq, g, w_in, conv_w, conv_b, w1, b1, freq, w2, b2, w3, skip, w_out):
    sub = seq // POLY
    split_shape = (batch, seq // SPLIT_CHUNK, POLY, SPLIT_CHUNK // POLY, C_WIDTH)
    x0, vv = (a.reshape(split_shape) for a in hyena_in(h, seq, g, w_in, conv_w, conv_b))
    hdn = _filter_hidden(seq, w1, b1, freq, w2, b2)
    min_decay = math.log(FILTER_TARGET) / FAST_DECAY_PCT
    max_decay = math.log(FILTER_TARGET) / SLOW_DECAY_PCT
    deltas = jnp.abs(jnp.linspace(min_decay, max_decay, C_WIDTH, dtype=F32))
    taps = hyena_filter_taps(hdn, w3, deltas)
    mmat, gmat = _dft_matrices(sub)
    coefs, nyq = hyena_filter_dft(mmat, taps)
    zs = hyena_dft_fwd(mmat, vv, coefs, nyq).reshape(batch, POLY, 2 * sub, C_WIDTH)
    u = hyena_dft_inv(gmat, zs, vv, x0, skip)
    u = u.reshape(batch * seq // SPLIT_CHUNK, POLY, SPLIT_CHUNK // POLY, C_WIDTH)
    return proj_residual([u], [w_out], h, interleave=True, tm=SPLIT_CHUNK)


def even_mixer(h, batch, seq, g, w_in, w_out, lam_vec, subln_g, na_table, rel_table, layer_idx):
    col_scale = np.ones((6, A_WIDTH), np.float32)
    col_scale[0] = A_QK_DIM ** -0.5 * LOG2E
    col_scale[3] = B_HEAD_DIM ** -0.5 * LOG2E
    z = norm_matmul(h, g, w_in, jnp.asarray(col_scale.reshape(-1)), BF16)
    z = z.reshape(batch, seq, w_in.shape[1])
    lam_init = 0.8 - 0.6 * math.exp(-0.3 * layer_idx)
    lv = lam_vec.astype(F32)
    lam = jnp.exp(jnp.sum(lv[0] * lv[1])) - jnp.exp(jnp.sum(lv[2] * lv[3])) + lam_init
    oa = diff_attention(z, lam, rel_table, subln_g, lam_init)
    ob = neighbourhood_attention(z, na_table)
    n = batch * seq
    return proj_residual([oa.reshape(n, A_WIDTH), ob.reshape(n, B_WIDTH)],
                         [w_out[:A_WIDTH], w_out[A_WIDTH:]], h)


def _trunk(x, p, W):
    batch, seq, d = x.shape
    depth = p.shape[0]
    h = x.reshape(batch * seq, d)
    for i in range(depth):
        j = i // 2
        if i % 2 == 0:
            h = even_mixer(h, batch, seq, W['norm_mix'][i], W['ab_w_in'][j], W['ab_w_out'][j],
                           W['diff_lambda'][j], W['diff_subln'][j], W['na_bias'][j],
                           W['rel_bias_table'], i)
        else:
            h = hyena_mixer(h, batch, seq, W['norm_mix'][i], W['c_w_in'][j], W['c_conv_w'][j],
                            W['c_conv_b'][j], W['c_filt_w1'][j], W['c_filt_b1'][j],
                            W['c_filt_freq'][j], W['c_filt_w2'][j], W['c_filt_b2'][j],
                            W['c_filt_w3'][j], W['c_skip'][j], W['c_w_out'][j])
        h = ffn(h, W['norm_ffn'][i], W['ffn_w_in'][i], W['ffn_w_out'][i])
        h = ple(h, p[i].reshape(batch * seq, -1), W['norm_ple'][i], W['ple_w_gate'][i],
                W['ple_w_proj'][i], W['final_norm'], final_norm=(i == depth - 1))
    return h.reshape(batch, seq, d)


def kernel(x_prompt, x_sample, p_prompt, p_sample, rel_bias_table, norm_mix, norm_ffn, norm_ple, final_norm, ab_w_in, ab_w_out, diff_lambda, diff_subln, na_bias, c_w_in, c_conv_w, c_conv_b, c_filt_w1, c_filt_b1, c_filt_freq, c_filt_w2, c_filt_b2, c_filt_w3, c_skip, c_w_out, ffn_w_in, ffn_w_out, ple_w_proj, ple_w_gate):
    bf = lambda w: w.astype(BF16)
    W = dict(rel_bias_table=rel_bias_table, norm_mix=norm_mix, norm_ffn=norm_ffn,
             norm_ple=norm_ple, final_norm=final_norm, ab_w_in=bf(ab_w_in), ab_w_out=bf(ab_w_out),
             diff_lambda=diff_lambda, diff_subln=diff_subln, na_bias=na_bias,
             c_w_in=bf(c_w_in), c_conv_w=c_conv_w, c_conv_b=c_conv_b, c_filt_w1=c_filt_w1,
             c_filt_b1=c_filt_b1, c_filt_freq=c_filt_freq, c_filt_w2=c_filt_w2,
             c_filt_b2=c_filt_b2, c_filt_w3=bf(c_filt_w3), c_skip=c_skip, c_w_out=bf(c_w_out),
             ffn_w_in=bf(ffn_w_in), ffn_w_out=bf(ffn_w_out), ple_w_proj=bf(ple_w_proj),
             ple_w_gate=bf(ple_w_gate))
    return (_trunk(x_prompt, p_prompt, W), _trunk(x_sample, p_sample, W))
```
